```python
import jax, jax.numpy as jnp
from jax import lax
import numpy as np

D_MODEL = 1024
BATCH = 8
SEQ = 2048
DEPTH = 4

HEAD_DIM = 64
H_A = 4
H_B = 4
H_C = 4
H_D = 4
N_BRANCH = 4
BRANCH_WIDTH = 4 * HEAD_DIM
N_ALIBI = H_A + H_B
MOBA_BLOCK = 256
MOBA_TOPK = 3
MOBA_Q_CHUNK = 32
NSA_CMP_LEN = 32
NSA_CMP_STRIDE = 16
NSA_CMP_HIDDEN = 4 * HEAD_DIM
NSA_SLC_BLOCK = 64
NSA_TOP_N = 16
NSA_WINDOW = 512
NSA_Q_CHUNK = 64
NSA_FORCE_BONUS = 1.0e4
ATTN_Q_BLOCK = 128
MLA_Q_RANK = 256
MLA_KV_RANK = 128
MLA_NOPE = 64
MLA_ROPE = 32
MLA_V = 64
ROPE_BASE = 10000.0
MEM_LEN = 256
N_MEM_HEADS = 4
D_FF = 2816
N_SUBLAYERS = 4
ALPHA = (2 * DEPTH) ** 0.25
BETA = (8 * DEPTH) ** -0.25
LN_EPS = 1e-5
RMS_EPS = 1e-6
NEG_INF = -1e30
MAX_POS_OFFSET = 4096
IN_SPLITS = (H_A * HEAD_DIM, H_A * HEAD_DIM, H_A * HEAD_DIM,
             H_B * HEAD_DIM, HEAD_DIM, HEAD_DIM, HEAD_DIM,
             HEAD_DIM, HEAD_DIM, HEAD_DIM, 3 * H_B,
             H_C * HEAD_DIM, H_C * HEAD_DIM, H_C * HEAD_DIM, H_C,
             MLA_Q_RANK, MLA_KV_RANK, MLA_ROPE,
             N_BRANCH * D_MODEL)
N_IN = sum(IN_SPLITS)
FOX_F_OFFSET = 3 * H_A * HEAD_DIM + H_B * HEAD_DIM + 6 * HEAD_DIM + 3 * H_B + 3 * H_C * HEAD_DIM

kernel_name = "hybrid_moba_nsa_fox_mla_deepnorm"


def split_points(sizes):
    return [int(v) for v in np.cumsum(sizes)[:-1]]


def layer_norm(x, g, b):
    xf = x.astype(jnp.float32)
    mu = jnp.mean(xf, -1, keepdims=True)
    var = jnp.mean(jnp.square(xf - mu), -1, keepdims=True)
    return ((xf - mu) * lax.rsqrt(var + LN_EPS) * g + b).astype(x.dtype)


def rms_norm(x, g):
    xf = x.astype(jnp.float32)
    return (xf * lax.rsqrt(jnp.mean(jnp.square(xf), -1, keepdims=True) + RMS_EPS) * g).astype(x.dtype)


def masked_softmax(s, mask):
    s = jnp.where(mask, s, NEG_INF)
    m = jnp.max(s, -1, keepdims=True)
    e = jnp.where(mask, jnp.exp(s - m), 0.0)
    return e / jnp.maximum(jnp.sum(e, -1, keepdims=True), 1e-30)


def swiglu(x, w_in, w_out):
    g, u = jnp.split(x @ w_in, 2, axis=-1)
    return (jax.nn.silu(g) * u) @ w_out


def to_heads(x, h):
    B, S, _ = x.shape
    return x.reshape(B, S, h, -1).transpose(0, 2, 1, 3)


def from_heads(x):
    B, H, S, d = x.shape
    return x.transpose(0, 2, 1, 3).reshape(B, S, H * d)


def alibi_slopes(n):
    return jnp.exp2(-8.0 * jnp.arange(1, n + 1, dtype=jnp.float32) / n)


def apply_rope(x, cos, sin):
    half = x.shape[-1] // 2
    x1, x2 = x[..., :half].astype(jnp.float32), x[..., half:].astype(jnp.float32)
    return jnp.concatenate([x1 * cos - x2 * sin, x1 * sin + x2 * cos], -1).astype(x.dtype)


def swept_causal_attention(q, k, v, decay=None):
    B, H, S, dk = q.shape
    C = ATTN_Q_BLOCK
    nc = S // C
    scale = dk ** -0.5
    s_pos = jnp.arange(S)
    xs = (jnp.arange(nc), q.reshape(B, H, nc, C, dk).transpose(2, 0, 1, 3, 4))
    if decay is not None:
        xs = xs + (decay.reshape(B, H, nc, C).transpose(2, 0, 1, 3),)

    def one_block(args):
        ci, qb, *rest = args
        t = ci * C + jnp.arange(C)
        logits = jnp.einsum('bhqd,bhkd->bhqk', qb, k).astype(jnp.float32) * scale
        if decay is not None:
            logits = logits + rest[0][..., None] - decay[:, :, None, :]
        p = masked_softmax(logits, s_pos[None, :] <= t[:, None])
        return jnp.einsum('bhqk,bhkd->bhqd', p.astype(v.dtype), v)

    o = lax.map(one_block, xs)
    return o.transpose(1, 2, 0, 3, 4).reshape(B, H, S, v.shape[-1])


def moba_attention(q, k, v, slopes):
    B, H, S, dh = q.shape
    nb = -(-S // MOBA_BLOCK)
    pad = nb * MOBA_BLOCK - S
    kb = jnp.pad(k, ((0, 0), (0, 0), (0, pad), (0, 0))).reshape(B, H, nb, MOBA_BLOCK, dh)
    vb = jnp.pad(v, ((0, 0), (0, 0), (0, pad), (0, 0))).reshape(B, H, nb, MOBA_BLOCK, dh)
    n_sel = min(MOBA_TOPK, nb - 1)
    C = MOBA_Q_CHUNK
    nc = S // C
    scale = dh ** -0.5
    blk_pos = jnp.arange(MOBA_BLOCK)
    xs = (jnp.arange(nc), q.reshape(B, H, nc, C, dh).transpose(2, 0, 1, 3, 4))
    if n_sel > 0:
        own_all = jnp.arange(S) // MOBA_BLOCK
        k_mean = jnp.mean(kb.astype(jnp.float32), axis=3)
        gate = jnp.einsum('bhtd,bhnd->bhtn', q.astype(jnp.float32), k_mean)
        cand = jnp.arange(nb)[None, :] < own_all[:, None]
        _, sel = lax.top_k(jnp.where(cand, gate, NEG_INF), n_sel)
        xs = xs + (sel.reshape(B, H, nc, C, n_sel).transpose(2, 0, 1, 3, 4),)
    gather = jax.vmap(jax.vmap(lambda table, idx: table[idx]))

    def one_chunk(args):
        ci, qc, *rest = args
        t = ci * C + jnp.arange(C)
        b_own = (ci * C) // MOBA_BLOCK
        k_own = lax.dynamic_index_in_dim(kb, b_own, axis=2, keepdims=False)
        v_own = lax.dynamic_index_in_dim(vb, b_own, axis=2, keepdims=False)
        pos_own = jnp.broadcast_to(b_own * MOBA_BLOCK + blk_pos, (B, H, C, MOBA_BLOCK))
        s = jnp.einsum('bhqd,bhkd->bhqk', qc, k_own)
        pos = pos_own
        mask = pos_own <= t[:, None]
        if n_sel > 0:
            sc = rest[0]
            k_sel = gather(kb, sc)
            v_sel = gather(vb, sc)
            s_sel = jnp.einsum('bhqd,bhqnkd->bhqnk', qc, k_sel).reshape(B, H, C, n_sel * MOBA_BLOCK)
            pos_sel = (sc[..., None] * MOBA_BLOCK + blk_pos).reshape(B, H, C, n_sel * MOBA_BLOCK)
            mask_sel = jnp.repeat(sc < b_own, MOBA_BLOCK, axis=-1)
            s = jnp.concatenate([s_sel, s], -1)
            pos = jnp.concatenate([pos_sel, pos], -1)
            mask = jnp.concatenate([mask_sel, mask], -1)
        dist = (t[:, None] - pos).astype(jnp.float32)
        logits = s.astype(jnp.float32) * scale - slopes[:, None, None] * dist
        p = masked_softmax(logits, mask).astype(v.dtype)
        o = jnp.einsum('bhqk,bhkd->bhqd', p[..., -MOBA_BLOCK:], v_own)
        if n_sel > 0:
            p_sel = p[..., :n_sel * MOBA_BLOCK].reshape(B, H, C, n_sel, MOBA_BLOCK)
            o = o + jnp.einsum('bhqnk,bhqnkd->bhqd', p_sel, v_sel)
        return o

    o = lax.map(one_chunk, xs)
    return o.transpose(1, 2, 0, 3, 4).reshape(B, H, S, dh)


def nsa_compress(x, pos, w1, w2):
    B, S, dh = x.shape
    n_cmp = (S - NSA_CMP_LEN) // NSA_CMP_STRIDE + 1
    idx = jnp.arange(n_cmp)[:, None] * NSA_CMP_STRIDE + jnp.arange(NSA_CMP_LEN)[None, :]
    blocks = (x[:, idx] + pos).reshape(B, n_cmp, NSA_CMP_LEN * dh)
    return jax.nn.gelu(blocks @ w1) @ w2


def nsa_selected(q, k, v, sel, slopes):
    B, H, S, dh = q.shape
    n_slc = S // NSA_SLC_BLOCK
    n_top = sel.shape[-1]
    kb = k.reshape(B, n_slc, NSA_SLC_BLOCK, dh)
    vb = v.reshape(B, n_slc, NSA_SLC_BLOCK, dh)
    C = NSA_Q_CHUNK
    nc = S // C
    scale = dh ** -0.5
    blk = jnp.arange(NSA_SLC_BLOCK)
    gather = jax.vmap(lambda table, idx: table[idx])
    xs = (jnp.arange(nc), q.reshape(B, H, nc, C, dh).transpose(2, 0, 1, 3, 4),
          sel.reshape(B, nc, C, n_top).transpose(1, 0, 2, 3))

    def one_chunk(args):
        ci, qc, sc = args
        t = ci * C + jnp.arange(C)
        k_sel = gather(kb, sc)
        v_sel = gather(vb, sc)
        pos = (sc[..., None] * NSA_SLC_BLOCK + blk).reshape(B, 1, C, n_top * NSA_SLC_BLOCK)
        dist = (t[:, None] - pos).astype(jnp.float32)
        s = jnp.einsum('bhqd,bqnkd->bhqnk', qc, k_sel).reshape(B, H, C, n_top * NSA_SLC_BLOCK)
        logits = s.astype(jnp.float32) * scale - slopes[:, None, None] * dist
        p = masked_softmax(logits, dist >= 0).astype(v.dtype)
        return jnp.einsum('bhqnk,bqnkd->bhqd', p.reshape(B, H, C, n_top, NSA_SLC_BLOCK), v_sel)

    o = lax.map(one_chunk, xs)
    return o.transpose(1, 2, 0, 3, 4).reshape(B, H, S, dh)


def sliding_window_attention(q, k, v, slopes):
    B, H, S, dh = q.shape
    C, W = ATTN_Q_BLOCK, NSA_WINDOW
    nc = S // C
    kp = jnp.pad(k, ((0, 0), (W, 0), (0, 0)))
    vp = jnp.pad(v, ((0, 0), (W, 0), (0, 0)))
    idx = jnp.arange(nc)[:, None] * C + jnp.arange(C + W)[None, :]
    kw, vw = kp[:, idx], vp[:, idx]
    qb = q.reshape(B, H, nc, C, dh)
    t = jnp.arange(nc)[:, None] * C + jnp.arange(C)[None, :]
    s_pos = idx - W
    dist = t[:, :, None] - s_pos[:, None, :]
    mask = (dist >= 0) & (dist < W) & (s_pos[:, None, :] >= 0)
    s = jnp.einsum('bhcqd,bckd->bhcqk', qb, kw).astype(jnp.float32) * dh ** -0.5
    logits = s - slopes[:, None, None, None] * dist.astype(jnp.float32)
    p = masked_softmax(logits, mask).astype(v.dtype)
    return jnp.einsum('bhcqk,bckd->bhcqd', p, vw).reshape(B, H, S, dh)


def nsa_attention(q, k_c, v_c, k_s, v_s, k_w, v_w, gate_logits, slopes, cmp_pos, cmp_w1, cmp_w2):
    B, H, S, dh = q.shape
    t_all = jnp.arange(S)
    k_cmp = nsa_compress(k_c, cmp_pos[0], cmp_w1[0], cmp_w2[0])
    v_cmp = nsa_compress(v_c, cmp_pos[1], cmp_w1[1], cmp_w2[1])
    n_cmp = k_cmp.shape[1]
    cmp_start = jnp.arange(n_cmp) * NSA_CMP_STRIDE
    dist = (t_all[:, None] - (cmp_start + NSA_CMP_LEN - 1)[None, :]).astype(jnp.float32)
    logits = jnp.einsum('bhtd,bcd->bhtc', q, k_cmp).astype(jnp.float32) * dh ** -0.5 - slopes[:, None, None] * dist
    p_cmp = masked_softmax(logits, dist >= 0)
    o_cmp = jnp.einsum('bhtc,bcd->bhtd', p_cmp.astype(v_cmp.dtype), v_cmp)
    n_slc = S // NSA_SLC_BLOCK
    slc_start = jnp.arange(n_slc) * NSA_SLC_BLOCK
    overlap = ((cmp_start[:, None] < slc_start[None, :] + NSA_SLC_BLOCK)
               & (cmp_start[:, None] + NSA_CMP_LEN > slc_start[None, :])).astype(jnp.float32)
    imp = jnp.einsum('bhtc,cj->btj', p_cmp, overlap)
    own = t_all // NSA_SLC_BLOCK
    j = jnp.arange(n_slc)[None, :]
    forced = (j == 0) | (j == own[:, None]) | (j == own[:, None] - 1)
    imp = jnp.where(j <= own[:, None], imp + NSA_FORCE_BONUS * forced.astype(jnp.float32), NEG_INF)
    _, sel = lax.top_k(imp, min(NSA_TOP_N, n_slc))
    o_slc = nsa_selected(q, k_s, v_s, sel, slopes)
    o_win = sliding_window_attention(q, k_w, v_w, slopes)
    g = jax.nn.sigmoid(gate_logits.astype(jnp.float32)).reshape(B, S, H, 3).transpose(0, 2, 1, 3).astype(q.dtype)
    return g[..., 0:1] * o_cmp + g[..., 1:2] * o_slc + g[..., 2:3] * o_win


def mla_attention(c_q, c_kv, k_rope, positions, q_norm, w_uq, kv_norm, w_ukv):
    B, S, _ = c_q.shape
    q = (rms_norm(c_q, q_norm) @ w_uq).reshape(B, S, H_D, MLA_NOPE + MLA_ROPE)
    kv = (rms_norm(c_kv, kv_norm) @ w_ukv).reshape(B, S, H_D, MLA_NOPE + MLA_V)
    inv_freq = ROPE_BASE ** (-jnp.arange(0, MLA_ROPE, 2, dtype=jnp.float32) / MLA_ROPE)
    ang = positions.astype(jnp.float32)[..., None] * inv_freq
    cos, sin = jnp.cos(ang), jnp.sin(ang)
    q_rope = apply_rope(q[..., MLA_NOPE:], cos[:, :, None], sin[:, :, None])
    k_r = apply_rope(k_rope, cos, sin)[:, :, None, :]
    qh = jnp.concatenate([q[..., :MLA_NOPE], q_rope], -1).transpose(0, 2, 1, 3)
    kh = jnp.concatenate([kv[..., :MLA_NOPE], jnp.broadcast_to(k_r, (B, S, H_D, MLA_ROPE))], -1).transpose(0, 2, 1, 3)
    vh = kv[..., MLA_NOPE:].transpose(0, 2, 1, 3)
    return swept_causal_attention(qh, kh, vh)


def token_mixer(x, positions, w_in, b_in, w_branch, w_o, q_norm, w_uq, kv_norm, w_ukv,
                cmp_pos, cmp_w1, cmp_w2):
    B, S, _ = x.shape
    proj = x @ w_in + b_in
    (a_q, a_k, a_v, b_q, b_kc, b_vc, b_ks, b_vs, b_kw, b_vw, b_g,
     c_q, c_k, c_v, c_f, d_cq, d_ckv, d_kr, g_merge) = jnp.split(proj, split_points(IN_SPLITS), axis=-1)
    slopes = alibi_slopes(N_ALIBI)
    o_a = moba_attention(to_heads(a_q, H_A), to_heads(a_k, H_A), to_heads(a_v, H_A), slopes[0::2])
    o_b = nsa_attention(to_heads(b_q, H_B), b_kc, b_vc, b_ks, b_vs, b_kw, b_vw, b_g, slopes[1::2],
                        cmp_pos, cmp_w1, cmp_w2)
    decay = jnp.cumsum(jax.nn.log_sigmoid(c_f.astype(jnp.float32)), axis=1).transpose(0, 2, 1)
    o_c = swept_causal_attention(to_heads(c_q, H_C), to_heads(c_k, H_C), to_heads(c_v, H_C), decay)
    o_d = mla_attention(d_cq, d_ckv, d_kr, positions, q_norm, w_uq, kv_norm, w_ukv)
    branches = jnp.stack([from_heads(o_a), from_heads(o_b), from_heads(o_c), from_heads(o_d)], axis=2)
    y = jnp.einsum('bsnk,nkd->bsnd', branches, w_branch)
    g = jax.nn.sigmoid(g_merge.reshape(B, S, N_BRANCH, D_MODEL))
    return jnp.sum(g * y, axis=2) @ w_o


def memory_cross_attention(x, mem, w_q, w_kv, w_o):
    q = to_heads(x @ w_q, N_MEM_HEADS)
    k, v = jnp.split(mem @ w_kv, 2, axis=-1)
    k, v = to_heads(k, N_MEM_HEADS), to_heads(v, N_MEM_HEADS)
    logits = jnp.einsum('bhqd,bhkd->bhqk', q, k).astype(jnp.float32) * HEAD_DIM ** -0.5
    p = jax.nn.softmax(logits, axis=-1).astype(v.dtype)
    return from_heads(jnp.einsum('bhqk,bhkd->bhqd', p, v)) @ w_o


def setup_inputs(seed: int = 0) -> dict:
    key = jax.random.key(seed)
    ks = jax.random.split(key, 26)
    L = DEPTH

    def nrm(k, shape, scale):
        return jax.random.normal(k, shape, jnp.float32) * scale

    x = nrm(ks[0], (BATCH, SEQ, D_MODEL), 1.0)
    mem = nrm(ks[1], (BATCH, MEM_LEN, D_MODEL), 1.0)
    positions = (jnp.arange(SEQ, dtype=jnp.int32)[None, :]
                 + jax.random.randint(ks[2], (BATCH, 1), 0, MAX_POS_OFFSET, dtype=jnp.int32))
    ln_g = 1.0 + nrm(ks[3], (L, N_SUBLAYERS, D_MODEL), 0.02)
    ln_b = nrm(ks[4], (L, N_SUBLAYERS, D_MODEL), 0.02)
    ffn1_w_in = nrm(ks[5], (L, D_MODEL, 2 * D_FF), D_MODEL ** -0.5)
    ffn1_w_out = nrm(ks[6], (L, D_FF, D_MODEL), D_FF ** -0.5 * BETA)
    ffn2_w_in = nrm(ks[7], (L, D_MODEL, 2 * D_FF), D_MODEL ** -0.5)
    ffn2_w_out = nrm(ks[8], (L, D_FF, D_MODEL), D_FF ** -0.5 * BETA)
    w_in = nrm(ks[9], (L, D_MODEL, N_IN), D_MODEL ** -0.5)
    b_in = nrm(ks[10], (L, N_IN), 0.02)
    b_in = b_in.at[:, FOX_F_OFFSET:FOX_F_OFFSET + H_C].set(
        jax.random.uniform(ks[11], (L, H_C), jnp.float32, 2.0, 5.0))
    w_branch = nrm(ks[12], (L, N_BRANCH, BRANCH_WIDTH, D_MODEL), BRANCH_WIDTH ** -0.5)
    w_o = nrm(ks[13], (L, D_MODEL, D_MODEL), D_MODEL ** -0.5 * BETA)
    mla_q_norm = 1.0 + nrm(ks[14], (L, MLA_Q_RANK), 0.02)
    mla_w_uq = nrm(ks[15], (L, MLA_Q_RANK, H_D * (MLA_NOPE + MLA_ROPE)), MLA_Q_RANK ** -0.5)
    mla_kv_norm = 1.0 + nrm(ks[16], (L, MLA_KV_RANK), 0.02)
    mla_w_ukv = nrm(ks[17], (L, MLA_KV_RANK, H_D * (MLA_NOPE + MLA_V)), MLA_KV_RANK ** -0.5)
    nsa_cmp_pos = nrm(ks[18], (L, 2, NSA_CMP_LEN, HEAD_DIM), 0.02)
    nsa_cmp_w1 = nrm(ks[19], (L, 2, NSA_CMP_LEN * HEAD_DIM, NSA_CMP_HIDDEN), (NSA_CMP_LEN * HEAD_DIM) ** -0.5)
    nsa_cmp_w2 = nrm(ks[20], (L, 2, NSA_CMP_HIDDEN, HEAD_DIM), NSA_CMP_HIDDEN ** -0.5)
    xa_w_q = nrm(ks[21], (L, D_MODEL, N_MEM_HEADS * HEAD_DIM), D_MODEL ** -0.5)
    xa_w_kv = nrm(ks[22], (L, D_MODEL, 2 * N_MEM_HEADS * HEAD_DIM), D_MODEL ** -0.5)
    xa_w_o = nrm(ks[23], (L, N_MEM_HEADS * HEAD_DIM, D_MODEL), (N_MEM_HEADS * HEAD_DIM) ** -0.5 * BETA)
    return {"x": x, "mem": mem, "positions": positions, "ln_g": ln_g, "ln_b": ln_b,
            "ffn1_w_in": ffn1_w_in, "ffn1_w_out": ffn1_w_out, "ffn2_w_in": ffn2_w_in, "ffn2_w_out": ffn2_w_out,
            "w_in": w_in, "b_in": b_in, "w_branch": w_branch, "w_o": w_o,
            "mla_q_norm": mla_q_norm, "mla_w_uq": mla_w_uq, "mla_kv_norm": mla_kv_norm, "mla_w_ukv": mla_w_ukv,
            "nsa_cmp_pos": nsa_cmp_pos, "nsa_cmp_w1": nsa_cmp_w1, "nsa_cmp_w2": nsa_cmp_w2,
            "xa_w_q": xa_w_q, "xa_w_kv": xa_w_kv, "xa_w_o": xa_w_o}


def reference(x, mem, positions, ln_g, ln_b, ffn1_w_in, ffn1_w_out, ffn2_w_in, ffn2_w_out,
              w_in, b_in, w_branch, w_o, mla_q_norm, mla_w_uq, mla_kv_norm, mla_w_ukv,
              nsa_cmp_pos, nsa_cmp_w1, nsa_cmp_w2, xa_w_q, xa_w_kv, xa_w_o):
    for l in range(DEPTH):
        x = layer_norm(ALPHA * x + 0.5 * swiglu(x, ffn1_w_in[l], ffn1_w_out[l]), ln_g[l, 0], ln_b[l, 0])
        mix = token_mixer(x, positions, w_in[l], b_in[l], w_branch[l], w_o[l],
                          mla_q_norm[l], mla_w_uq[l], mla_kv_norm[l], mla_w_ukv[l],
                          nsa_cmp_pos[l], nsa_cmp_w1[l], nsa_cmp_w2[l])
        x = layer_norm(ALPHA * x + mix, ln_g[l, 1], ln_b[l, 1])
        x = layer_norm(ALPHA * x + memory_cross_attention(x, mem, xa_w_q[l], xa_w_kv[l], xa_w_o[l]),
                       ln_g[l, 2], ln_b[l, 2])
        x = layer_norm(ALPHA * x + 0.5 * swiglu(x, ffn2_w_in[l], ffn2_w_out[l]), ln_g[l, 3], ln_b[l, 3])
    return x
```

```python
import functools

import jax
import jax.numpy as jnp
from jax import lax
from jax.experimental import pallas as pl
from jax.experimental.pallas import tpu as pltpu

F32 = jnp.float32
BF16 = jnp.bfloat16
HIGHEST = lax.Precision.HIGHEST

D_MODEL = 1024
DEPTH = 4
HEAD_DIM = 64
N_HEADS = 4
BRANCH_WIDTH = N_HEADS * HEAD_DIM
N_BRANCH = 4
MOBA_BLOCK = 256
MOBA_TOPK = 3
NSA_CMP_LEN = 32
NSA_CMP_STRIDE = 16
NSA_CMP_HIDDEN = 256
NSA_SLC_BLOCK = 64
NSA_TOP_N = 16
NSA_WINDOW = 512
NSA_FORCE_BONUS = 1.0e4
MLA_Q_RANK = 256
MLA_KV_RANK = 128
MLA_NOPE = 64
MLA_ROPE = 32
ROPE_BASE = 10000.0
MEM_LEN = 256
D_FF = 2816
ALPHA = (2 * DEPTH) ** 0.25
LN_EPS = 1e-5
RMS_EPS = 1e-6
NEG_INF = -1e30
MOBA_SLOPES = tuple(2.0 ** -(2 * h + 1) for h in range(N_HEADS))
NSA_SLOPES = tuple(2.0 ** -(2 * h + 2) for h in range(N_HEADS))

LANES = 128
TQ = 256
TK = 128
QK_WIDTH = 128
ROW_CHUNK = 512
FFN_TILE_F = 256
TOKEN_TILE = 512
VMEM_LIMIT = 56 * 1024 * 1024


def _nn(a, b):
    return jnp.dot(a, b, preferred_element_type=F32)


def _nt(a, b, precision=None):
    return lax.dot_general(a, b, (((1,), (1,)), ((), ())), precision=precision,
                           preferred_element_type=F32)


def _layer_norm(z, g, b):
    mu = jnp.mean(z, -1, keepdims=True)
    d = z - mu
    var = jnp.mean(d * d, -1, keepdims=True)
    return d * lax.rsqrt(var + LN_EPS) * g + b


def _params(n_parallel):
    return pltpu.CompilerParams(dimension_semantics=("parallel",) * n_parallel,
                                vmem_limit_bytes=VMEM_LIMIT)


def _ffn_kernel(x_ref, wg_ref, wu_ref, wo_ref, g_ref, b_ref, o_ref, acc_ref):
    x = x_ref[...]
    xb = x.astype(BF16)
    acc_ref[...] = jnp.zeros_like(acc_ref)

    def chunk(c, carry):
        hg = _nn(xb, wg_ref[c])
        hu = _nn(xb, wu_ref[c])
        a = (hg * jax.nn.sigmoid(hg) * hu).astype(BF16)
        acc_ref[...] += _nn(a, wo_ref[c])
        return carry

    lax.fori_loop(0, wg_ref.shape[0], chunk, 0)
    o_ref[...] = _layer_norm(ALPHA * x + 0.5 * acc_ref[...], g_ref[...], b_ref[...])


def _ffn_sublayer(x2, wg, wu, wo, g, b, layer):
    T, D = x2.shape
    nc, _, tf = wg.shape[1:]
    tm = TOKEN_TILE
    return pl.pallas_call(
        _ffn_kernel,
        grid=(T // tm,),
        in_specs=[
            pl.BlockSpec((tm, D), lambda i: (i, 0)),
            pl.BlockSpec((None, nc, D, tf), lambda i: (layer, 0, 0, 0)),
            pl.BlockSpec((None, nc, D, tf), lambda i: (layer, 0, 0, 0)),
            pl.BlockSpec((None, nc, tf, D), lambda i: (layer, 0, 0, 0)),
            pl.BlockSpec((1, D), lambda i: (0, 0)),
            pl.BlockSpec((1, D), lambda i: (0, 0)),
        ],
        out_specs=pl.BlockSpec((tm, D), lambda i: (i, 0)),
        out_shape=jax.ShapeDtypeStruct((T, D), F32),
        scratch_shapes=[pltpu.VMEM((tm, D), F32)],
        compiler_params=_params(1),
        name="ffn_sublayer",
    )(x2, wg, wu, wo, g, b)


def _tile_update(s, vt_tile, m, l, acc):
    m_new = jnp.maximum(m, jnp.max(s, axis=0, keepdims=True))
    alpha = jnp.exp(m - m_new)
    p = jnp.exp(s - m_new)
    l_new = alpha * l + jnp.sum(p, axis=0, keepdims=True)
    acc_new = alpha * acc + _nn(vt_tile, p.astype(BF16))
    return m_new, l_new, acc_new


def _tile_iotas():
    r = lax.broadcasted_iota(jnp.int32, (TK, TQ), 0)
    c = lax.broadcasted_iota(jnp.int32, (TK, TQ), 1)
    return r, c


def _causal_flash(qt, k_tile, vt_tile, qi, past_mask=None, diag_mask=None):
    r, c = _tile_iotas()
    m = jnp.full((1, TQ), NEG_INF, F32)
    l = jnp.zeros((1, TQ), F32)
    acc = jnp.zeros((HEAD_DIM, TQ), F32)
    for half in range(2):
        t = 2 * qi + half
        s = _nt(k_tile(t), qt)
        mask = (r + half * TK) <= c
        if diag_mask is not None:
            mask = mask & diag_mask(t)
        s = jnp.where(mask, s, NEG_INF)
        m, l, acc = _tile_update(s, vt_tile(t), m, l, acc)

    def body(t, carry):
        s = _nt(k_tile(t), qt)
        if past_mask is not None:
            s = jnp.where(past_mask(t), s, NEG_INF)
        return _tile_update(s, vt_tile(t), *carry)

    m, l, acc = lax.fori_loop(0, 2 * qi, body, (m, l, acc))
    return acc / jnp.maximum(l, 1e-30)


def _position_columns(n_rows, first, step):
    row = lax.broadcasted_iota(jnp.int32, (n_rows, HEAD_DIM), 0)
    lane = lax.broadcasted_iota(jnp.int32, (n_rows, HEAD_DIM), 1)
    pos = first + step * row
    hi = ((pos >> 7) << 7).astype(F32)
    lo = (pos & 127).astype(F32)
    return jnp.where(lane == 0, hi, jnp.where(lane == 1, lo, 0.0))


def _query_columns(n_rows, value, n_cols):
    lane = lax.broadcasted_iota(jnp.int32, (n_rows, HEAD_DIM), 1)
    return jnp.where(lane < n_cols, value, 0.0).astype(F32)


def _project(x_ref, w_ref, b_ref, wt_ref, bt_ref, rm_ref, cm_ref):
    S = x_ref.shape[1]
    for i in range(S // ROW_CHUNK):
        rows = slice(i * ROW_CHUNK, (i + 1) * ROW_CHUNK)
        xb = x_ref[0, rows, :].astype(BF16)
        rm_ref[rows, :] = _nn(xb, w_ref[...]) + b_ref[...]
        cm_ref[:, rows] = _nt(wt_ref[...], xb) + bt_ref[...]


def _stage_vt(cm_ref, vt_ref, n_groups):
    S = cm_ref.shape[1]
    for g in range(n_groups):
        for t in range(S // TK):
            vt_ref[g, t] = cm_ref[g * HEAD_DIM:(g + 1) * HEAD_DIM, t * TK:(t + 1) * TK].astype(BF16)


def _store_heads(o_ref, q0, outs):
    o_ref[0, pl.ds(q0, TQ), :] = jnp.concatenate(outs, axis=0).T


def _moba_kernel(x_ref, w_ref, b_ref, wt_ref, bt_ref, o_ref, rm_ref, cm_ref, qp_ref, kp_ref, vt_ref, sel_ref):
    S = x_ref.shape[1]
    nb = S // MOBA_BLOCK
    _project(x_ref, w_ref, b_ref, wt_ref, bt_ref, rm_ref, cm_ref)
    _stage_vt(cm_ref, vt_ref, N_HEADS)
    kcols = _position_columns(S, 0, 1)
    blk = lax.broadcasted_iota(jnp.int32, (nb, S), 0)
    own = lax.broadcasted_iota(jnp.int32, (nb, S), 1) // MOBA_BLOCK
    cand = blk < own
    for h in range(N_HEADS):
        q = rm_ref[:, h * HEAD_DIM:(h + 1) * HEAD_DIM]
        k = rm_ref[:, BRANCH_WIDTH + h * HEAD_DIM:BRANCH_WIDTH + (h + 1) * HEAD_DIM]
        qp_ref[h] = jnp.concatenate(
            [q * HEAD_DIM ** -0.5, _query_columns(S, MOBA_SLOPES[h], 2)], axis=1).astype(BF16)
        kp_ref[h] = jnp.concatenate([k, kcols], axis=1).astype(BF16)
        k_mean = jnp.sum(k.reshape(nb, MOBA_BLOCK, HEAD_DIM), axis=1) * (1.0 / MOBA_BLOCK)
        gate = jnp.where(cand, _nt(k_mean, q, precision=HIGHEST), NEG_INF)
        beaten = jnp.zeros((nb, S), F32)
        for j in range(nb):
            other = gate[j:j + 1, :]
            wins = (other > gate) | ((other == gate) & (blk > j))
            beaten = beaten + jnp.where(wins, 1.0, 0.0)
        sel = jnp.where(cand & (beaten < MOBA_TOPK), 1.0, 0.0)
        for qi in range(S // TQ):
            sel_ref[h, qi] = sel[:, qi * TQ:(qi + 1) * TQ]

    def q_tile(qi, carry):
        q0 = pl.multiple_of(qi * TQ, TQ)
        outs = []
        for h in range(N_HEADS):
            qt = qp_ref[h, pl.ds(q0, TQ), :]
            outs.append(_causal_flash(
                qt,
                lambda t, h=h: kp_ref[h, pl.ds(pl.multiple_of(t * TK, TK), TK), :],
                lambda t, h=h: vt_ref[h, t],
                qi,
                past_mask=lambda t, h=h: sel_ref[h, qi, pl.ds(t // (MOBA_BLOCK // TK), 1), :] > 0.5))
        _store_heads(o_ref, q0, outs)
        return carry

    lax.fori_loop(0, S // TQ, q_tile, 0)


def _moba(x, w, b, wt, bt, layer):
    B, S, D = x.shape
    n_rm = w.shape[-1]
    n_cm = wt.shape[1]
    nb = S // MOBA_BLOCK
    return pl.pallas_call(
        _moba_kernel,
        grid=(B,),
        in_specs=[
            pl.BlockSpec((1, S, D), lambda i: (i, 0, 0)),
            pl.BlockSpec((None, D, n_rm), lambda i: (layer, 0, 0)),
            pl.BlockSpec((None, 1, n_rm), lambda i: (layer, 0, 0)),
            pl.BlockSpec((None, n_cm, D), lambda i: (layer, 0, 0)),
            pl.BlockSpec((None, n_cm, 1), lambda i: (layer, 0, 0)),
        ],
        out_specs=pl.BlockSpec((1, S, BRANCH_WIDTH), lambda i: (i, 0, 0)),
        out_shape=jax.ShapeDtypeStruct((B, S, BRANCH_WIDTH), F32),
        scratch_shapes=[
            pltpu.VMEM((S, n_rm), F32),
            pltpu.VMEM((n_cm, S), F32),
            pltpu.VMEM((N_HEADS, S, QK_WIDTH), BF16),
            pltpu.VMEM((N_HEADS, S, QK_WIDTH), BF16),
            pltpu.VMEM((N_HEADS, S // TK, HEAD_DIM, TK), BF16),
            pltpu.VMEM((N_HEADS, S // TQ, nb, TQ), F32),
        ],
        compiler_params=_params(1),
        name="moba_mixer",
    )(x, w, b, wt, bt)


def _fox_kernel(x_ref, w_ref, b_ref, wt_ref, bt_ref, o_ref, rm_ref, cm_ref, qp_ref, kp_ref, vt_ref):
    S = x_ref.shape[1]
    _project(x_ref, w_ref, b_ref, wt_ref, bt_ref, rm_ref, cm_ref)
    _stage_vt(cm_ref, vt_ref, N_HEADS)
    f = rm_ref[:, 2 * BRANCH_WIDTH:2 * BRANCH_WIDTH + LANES]
    log_sig = jnp.minimum(f, 0.0) - jnp.log1p(jnp.exp(-jnp.abs(f)))
    blk = MOBA_BLOCK
    tri = jnp.where(lax.broadcasted_iota(jnp.int32, (blk, blk), 0) >= lax.broadcasted_iota(jnp.int32, (blk, blk), 1),
                    1.0, 0.0).astype(F32)
    carry = jnp.zeros((1, LANES), F32)
    pieces = []
    for i in range(S // blk):
        cs = jnp.dot(tri, log_sig[i * blk:(i + 1) * blk, :], precision=HIGHEST,
                     preferred_element_type=F32) + carry
        pieces.append(cs)
        carry = cs[blk - 1:blk, :]
    neg_decay = -jnp.concatenate(pieces, axis=0)
    lane = lax.broadcasted_iota(jnp.int32, (S, HEAD_DIM), 1)
    for h in range(N_HEADS):
        q = rm_ref[:, h * HEAD_DIM:(h + 1) * HEAD_DIM]
        k = rm_ref[:, BRANCH_WIDTH + h * HEAD_DIM:BRANCH_WIDTH + (h + 1) * HEAD_DIM]
        d = neg_decay[:, h:h + 1]
        d_hi = d.astype(BF16).astype(F32)
        d_mid = (d - d_hi).astype(BF16).astype(F32)
        d_lo = d - d_hi - d_mid
        dcols = jnp.where(lane == 0, d_hi, jnp.where(lane == 1, d_mid, jnp.where(lane == 2, d_lo, 0.0)))
        qp_ref[h] = jnp.concatenate([q * HEAD_DIM ** -0.5, _query_columns(S, 1.0, 3)], axis=1).astype(BF16)
        kp_ref[h] = jnp.concatenate([k, dcols], axis=1).astype(BF16)

    def q_tile(qi, carry):
        q0 = pl.multiple_of(qi * TQ, TQ)
        outs = []
        for h in range(N_HEADS):
            qt = qp_ref[h, pl.ds(q0, TQ), :]
            outs.append(_causal_flash(
                qt,
                lambda t, h=h: kp_ref[h, pl.ds(pl.multiple_of(t * TK, TK), TK), :],
                lambda t, h=h: vt_ref[h, t],
                qi))
        _store_heads(o_ref, q0, outs)
        return carry

    lax.fori_loop(0, S // TQ, q_tile, 0)


def _fox(x, w, b, wt, bt, layer):
    B, S, D = x.shape
    n_rm = w.shape[-1]
    n_cm = wt.shape[1]
    return pl.pallas_call(
        _fox_kernel,
        grid=(B,),
        in_specs=[
            pl.BlockSpec((1, S, D), lambda i: (i, 0, 0)),
            pl.BlockSpec((None, D, n_rm), lambda i: (layer, 0, 0)),
            pl.BlockSpec((None, 1, n_rm), lambda i: (layer, 0, 0)),
            pl.BlockSpec((None, n_cm, D), lambda i: (layer, 0, 0)),
            pl.BlockSpec((None, n_cm, 1), lambda i: (layer, 0, 0)),
        ],
        out_specs=pl.BlockSpec((1, S, BRANCH_WIDTH), lambda i: (i, 0, 0)),
        out_shape=jax.ShapeDtypeStruct((B, S, BRANCH_WIDTH), F32),
        scratch_shapes=[
            pltpu.VMEM((S, n_rm), F32),
            pltpu.VMEM((n_cm, S), F32),
            pltpu.VMEM((N_HEADS, S, QK_WIDTH), BF16),
            pltpu.VMEM((N_HEADS, S, QK_WIDTH), BF16),
            pltpu.VMEM((N_HEADS, S // TK, HEAD_DIM, TK), BF16),
        ],
        compiler_params=_params(1),
        name="fox_mixer",
    )(x, w, b, wt, bt)


def _rope_table_kernel(pos_ref, freq_ref, sign_ref, cos_ref, sin_ref):
    ang = pos_ref[0].astype(F32) * freq_ref[...]
    cos_ref[0] = jnp.cos(ang)
    sin_ref[0] = jnp.sin(ang) * sign_ref[...]


def _rope_tables(positions, freq_row, sign_row):
    B, S = positions.shape
    return pl.pallas_call(
        _rope_table_kernel,
        grid=(B,),
        in_specs=[
            pl.BlockSpec((1, S, 1), lambda i: (i, 0, 0)),
            pl.BlockSpec((1, LANES), lambda i: (0, 0)),
            pl.BlockSpec((1, LANES), lambda i: (0, 0)),
        ],
        out_specs=[pl.BlockSpec((1, S, LANES), lambda i: (i, 0, 0))] * 2,
        out_shape=[jax.ShapeDtypeStruct((B, S, LANES), F32)] * 2,
        compiler_params=_params(1),
        name="rope_tables",
    )(positions.reshape(B, S, 1), freq_row, sign_row)


def _rms_norm(x, g):
    return x * lax.rsqrt(jnp.mean(x * x, -1, keepdims=True) + RMS_EPS) * g


def _mla_kernel(x_ref, cos_ref, sin_ref, w_ref, b_ref, qn_ref, kvn_ref, wq_ref, wqs_ref, wk_ref, wvt_ref,
                o_ref, rm_ref, cm_ref, qp_ref, kp_ref, vt_ref):
    S = x_ref.shape[1]
    for i in range(S // ROW_CHUNK):
        rows = slice(i * ROW_CHUNK, (i + 1) * ROW_CHUNK)
        xb = x_ref[0, rows, :].astype(BF16)
        rm_ref[rows, :] = _nn(xb, w_ref[...]) + b_ref[...]
    cos = cos_ref[0]
    sin = sin_ref[0]
    scale = (MLA_NOPE + MLA_ROPE) ** -0.5
    q_lat = _rms_norm(rm_ref[:, 0:MLA_Q_RANK], qn_ref[...]).astype(BF16)
    kv_lat = _rms_norm(rm_ref[:, MLA_Q_RANK:MLA_Q_RANK + MLA_KV_RANK], kvn_ref[...]).astype(BF16)
    c0 = MLA_Q_RANK + MLA_KV_RANK
    k_rope = rm_ref[:, c0:c0 + LANES] * cos + rm_ref[:, c0 + LANES:c0 + 2 * LANES] * sin
    cm_ref[...] = _nt(wvt_ref[...], kv_lat)
    _stage_vt(cm_ref, vt_ref, N_HEADS)
    for h in range(N_HEADS):
        qh = _nn(q_lat, wq_ref[h]) * cos + _nn(q_lat, wqs_ref[h]) * sin
        qp_ref[h] = (qh * scale).astype(BF16)
        kp_ref[h] = (_nn(kv_lat, wk_ref[h]) + k_rope).astype(BF16)

    def q_tile(qi, carry):
        q0 = pl.multiple_of(qi * TQ, TQ)
        outs = []
        for h in range(N_HEADS):
            qt = qp_ref[h, pl.ds(q0, TQ), :]
            outs.append(_causal_flash(
                qt,
                lambda t, h=h: kp_ref[h, pl.ds(pl.multiple_of(t * TK, TK), TK), :],
                lambda t, h=h: vt_ref[h, t],
                qi))
        _store_heads(o_ref, q0, outs)
        return carry

    lax.fori_loop(0, S // TQ, q_tile, 0)


def _mla(x, cos, sin, w, b, qn, kvn, wq, wqs, wk, wvt, layer):
    B, S, D = x.shape
    n_rm = w.shape[-1]
    lay3 = lambda i: (layer, 0, 0)
    lay4 = lambda i: (layer, 0, 0, 0)
    return pl.pallas_call(
        _mla_kernel,
        grid=(B,),
        in_specs=[
            pl.BlockSpec((1, S, D), lambda i: (i, 0, 0)),
            pl.BlockSpec((1, S, LANES), lambda i: (i, 0, 0)),
            pl.BlockSpec((1, S, LANES), lambda i: (i, 0, 0)),
            pl.BlockSpec((None, D, n_rm), lay3),
            pl.BlockSpec((None, 1, n_rm), lay3),
            pl.BlockSpec((None, 1, MLA_Q_RANK), lay3),
            pl.BlockSpec((None, 1, MLA_KV_RANK), lay3),
            pl.BlockSpec((None, N_HEADS, MLA_Q_RANK, QK_WIDTH), lay4),
            pl.BlockSpec((None, N_HEADS, MLA_Q_RANK, QK_WIDTH), lay4),
            pl.BlockSpec((None, N_HEADS, MLA_KV_RANK, QK_WIDTH), lay4),
            pl.BlockSpec((None, BRANCH_WIDTH, MLA_KV_RANK), lay3),
        ],
        out_specs=pl.BlockSpec((1, S, BRANCH_WIDTH), lambda i: (i, 0, 0)),
        out_shape=jax.ShapeDtypeStruct((B, S, BRANCH_WIDTH), F32),
        scratch_shapes=[
            pltpu.VMEM((S, n_rm), F32),
            pltpu.VMEM((BRANCH_WIDTH, S), F32),
            pltpu.VMEM((N_HEADS, S, QK_WIDTH), BF16),
            pltpu.VMEM((N_HEADS, S, QK_WIDTH), BF16),
            pltpu.VMEM((N_HEADS, S // TK, HEAD_DIM, TK), BF16),
        ],
        compiler_params=_params(1),
        name="mla_mixer",
    )(x, cos, sin, w, b, qn, kvn, wq, wqs, wk, wvt)


N_CMP_PAD = 128


def _nsa_compress(src_ref, pos_ref, w1_ref, w2_ref):
    accs = [jnp.zeros((N_CMP_PAD, NSA_CMP_HIDDEN), F32) for _ in range(2)]
    n_cmp = N_CMP_PAD - 1
    for p in range(NSA_CMP_LEN):
        rows = src_ref[pl.ds(p, n_cmp, stride=NSA_CMP_STRIDE), :]
        for which in range(2):
            part = rows[:, which * HEAD_DIM:(which + 1) * HEAD_DIM] + pos_ref[which, p:p + 1, :]
            part = jnp.concatenate([part, jnp.zeros((1, HEAD_DIM), F32)], axis=0)
            accs[which] = accs[which] + jnp.dot(part, w1_ref[which, p], precision=HIGHEST,
                                                preferred_element_type=F32)
    return [jnp.dot(jax.nn.gelu(accs[which]), w2_ref[which], precision=HIGHEST, preferred_element_type=F32)
            for which in range(2)]


def _nsa_kernel(x_ref, w_ref, b_ref, wt_ref, bt_ref, pos_ref, w1_ref, w2_ref, o_ref,
                rm_ref, cm_ref, cin_ref, qp_ref, ks_ref, kw_ref, kc_ref, vt_ref, vtc_ref, sel_ref):
    S = x_ref.shape[1]
    n_slc = S // NSA_SLC_BLOCK
    _project(x_ref, w_ref, b_ref, wt_ref, bt_ref, rm_ref, cm_ref)
    _stage_vt(cm_ref, vt_ref, 2)
    c_ks, c_kw, c_kc, c_vc, c_g = (BRANCH_WIDTH + i * HEAD_DIM for i in range(5))
    kcols = _position_columns(S, 0, 1)
    for h in range(N_HEADS):
        q = rm_ref[:, h * HEAD_DIM:(h + 1) * HEAD_DIM]
        qp_ref[h] = jnp.concatenate(
            [q * HEAD_DIM ** -0.5, _query_columns(S, NSA_SLOPES[h], 2)], axis=1).astype(BF16)
    ks_ref[...] = jnp.concatenate([rm_ref[:, c_ks:c_ks + HEAD_DIM], kcols], axis=1).astype(BF16)
    kw_ref[...] = jnp.concatenate([rm_ref[:, c_kw:c_kw + HEAD_DIM], kcols], axis=1).astype(BF16)
    cin_ref[...] = rm_ref[:, c_kc:c_kc + 2 * HEAD_DIM]
    k_cmp, v_cmp = _nsa_compress(cin_ref, pos_ref, w1_ref, w2_ref)
    kc_ref[...] = jnp.concatenate(
        [k_cmp, _position_columns(N_CMP_PAD, NSA_CMP_LEN - 1, NSA_CMP_STRIDE)], axis=1).astype(BF16)
    vtc_ref[...] = jnp.concatenate([k_cmp, v_cmp], axis=1).T[HEAD_DIM:, :].astype(BF16)

    jj = lax.broadcasted_iota(jnp.int32, (n_slc, N_CMP_PAD), 0) * NSA_SLC_BLOCK
    cc = lax.broadcasted_iota(jnp.int32, (n_slc, N_CMP_PAD), 1) * NSA_CMP_STRIDE
    overlap_t = jnp.where((cc < jj + NSA_SLC_BLOCK) & (cc + NSA_CMP_LEN > jj), 1.0, 0.0).astype(F32)

    def q_tile(qi, carry):
        q0 = pl.multiple_of(qi * TQ, TQ)
        qts = [qp_ref[h, pl.ds(q0, TQ), :] for h in range(N_HEADS)]
        c_idx = lax.broadcasted_iota(jnp.int32, (N_CMP_PAD, TQ), 0)
        t_cmp = q0 + lax.broadcasted_iota(jnp.int32, (N_CMP_PAD, TQ), 1)
        cmp_ok = (c_idx * NSA_CMP_STRIDE + (NSA_CMP_LEN - 1) <= t_cmp) & (c_idx < N_CMP_PAD - 1)
        imp = jnp.zeros((n_slc, TQ), F32)
        o_cmp = []
        for h in range(N_HEADS):
            s = jnp.where(cmp_ok, _nt(kc_ref[...], qts[h]), NEG_INF)
            e = jnp.where(cmp_ok, jnp.exp(s - jnp.max(s, axis=0, keepdims=True)), 0.0)
            p = e / jnp.maximum(jnp.sum(e, axis=0, keepdims=True), 1e-30)
            o_cmp.append(_nn(vtc_ref[...], p.astype(BF16)))
            imp = imp + jnp.dot(overlap_t, p, precision=HIGHEST, preferred_element_type=F32)
        j_idx = lax.broadcasted_iota(jnp.int32, (n_slc, TQ), 0)
        own = (q0 + lax.broadcasted_iota(jnp.int32, (n_slc, TQ), 1)) // NSA_SLC_BLOCK
        forced = (j_idx == 0) | (j_idx == own) | (j_idx == own - 1)
        score = jnp.where(j_idx <= own, imp + NSA_FORCE_BONUS * jnp.where(forced, 1.0, 0.0), NEG_INF)
        beaten = jnp.zeros((n_slc, TQ), F32)
        for j in range(n_slc):
            other = score[j:j + 1, :]
            wins = (other > score) | ((other == score) & (j_idx > j))
            beaten = beaten + jnp.where(wins, 1.0, 0.0)
        sel_ref[...] = jnp.where(beaten < NSA_TOP_N, 1.0, 0.0)

        r, c = _tile_iotas()
        per_tile = TK // NSA_SLC_BLOCK

        def slc_mask(t):
            first = sel_ref[pl.ds(per_tile * t, 1), :]
            second = sel_ref[pl.ds(per_tile * t + 1, 1), :]
            return jnp.where(r < NSA_SLC_BLOCK, first, second) > 0.5

        gates = jax.nn.sigmoid(rm_ref[pl.ds(q0, TQ), c_g:c_g + LANES]).T
        outs = []
        for h in range(N_HEADS):
            o_slc = _causal_flash(
                qts[h],
                lambda t: ks_ref[pl.ds(pl.multiple_of(t * TK, TK), TK), :],
                lambda t: vt_ref[0, t],
                qi, past_mask=slc_mask, diag_mask=slc_mask)
            delta0 = c - r

            def win_tile(t, carry, h=h):
                k0 = pl.multiple_of(t * TK, TK)
                dist = delta0 + (q0 - k0)
                s = jnp.where((dist >= 0) & (dist < NSA_WINDOW), _nt(kw_ref[pl.ds(k0, TK), :], qts[h]), NEG_INF)
                return _tile_update(s, vt_ref[1, t], *carry)

            state = (jnp.full((1, TQ), NEG_INF, F32), jnp.zeros((1, TQ), F32), jnp.zeros((HEAD_DIM, TQ), F32))
            state = win_tile(2 * qi, state)
            state = win_tile(2 * qi + 1, state)
            state = lax.fori_loop(jnp.maximum(2 * qi - NSA_WINDOW // TK, 0), 2 * qi, win_tile, state)
            o_win = state[2] / jnp.maximum(state[1], 1e-30)
            outs.append(gates[3 * h:3 * h + 1, :] * o_cmp[h] + gates[3 * h + 1:3 * h + 2, :] * o_slc
                        + gates[3 * h + 2:3 * h + 3, :] * o_win)
        _store_heads(o_ref, q0, outs)
        return carry

    lax.fori_loop(0, S // TQ, q_tile, 0)


def _nsa(x, w, b, wt, bt, pos, w1, w2, layer):
    B, S, D = x.shape
    n_rm = w.shape[-1]
    n_cm = wt.shape[1]
    lay3 = lambda i: (layer, 0, 0)
    lay4 = lambda i: (layer, 0, 0, 0)
    lay5 = lambda i: (layer, 0, 0, 0, 0)
    return pl.pallas_call(
        _nsa_kernel,
        grid=(B,),
        in_specs=[
            pl.BlockSpec((1, S, D), lambda i: (i, 0, 0)),
            pl.BlockSpec((None, D, n_rm), lay3),
            pl.BlockSpec((None, 1, n_rm), lay3),
            pl.BlockSpec((None, n_cm, D), lay3),
            pl.BlockSpec((None, n_cm, 1), lay3),
            pl.BlockSpec((None, 2, NSA_CMP_LEN, HEAD_DIM), lay4),
            pl.BlockSpec((None, 2, NSA_CMP_LEN, HEAD_DIM, NSA_CMP_HIDDEN), lay5),
            pl.BlockSpec((None, 2, NSA_CMP_HIDDEN, HEAD_DIM), lay4),
        ],
        out_specs=pl.BlockSpec((1, S, BRANCH_WIDTH), lambda i: (i, 0, 0)),
        out_shape=jax.ShapeDtypeStruct((B, S, BRANCH_WIDTH), F32),
        scratch_shapes=[
            pltpu.VMEM((S, n_rm), F32),
            pltpu.VMEM((n_cm, S), F32),
            pltpu.VMEM((S, 2 * HEAD_DIM), F32),
            pltpu.VMEM((N_HEADS, S, QK_WIDTH), BF16),
            pltpu.VMEM((S, QK_WIDTH), BF16),
            pltpu.VMEM((S, QK_WIDTH), BF16),
            pltpu.VMEM((N_CMP_PAD, QK_WIDTH), BF16),
            pltpu.VMEM((2, S // TK, HEAD_DIM, TK), BF16),
            pltpu.VMEM((HEAD_DIM, N_CMP_PAD), BF16),
            pltpu.VMEM((S // NSA_SLC_BLOCK, TQ), F32),
        ],
        compiler_params=_params(1),
        name="nsa_mixer",
    )(x, w, b, wt, bt, pos, w1, w2)


def _merge_kernel(x_ref, oa_ref, ob_ref, oc_ref, od_ref, wg_ref, bg_ref, wb_ref, wo_ref, g_ref, b_ref, o_ref):
    x = x_ref[...]
    xb = x.astype(BF16)
    D = x.shape[-1]
    acc = jnp.zeros(x.shape, F32)
    for n, branch in enumerate((oa_ref, ob_ref, oc_ref, od_ref)):
        gate = jax.nn.sigmoid(_nn(xb, wg_ref[:, n * D:(n + 1) * D]) + bg_ref[:, n * D:(n + 1) * D])
        acc = acc + gate * _nn(branch[...].astype(BF16), wb_ref[n])
    mix = _nn(acc.astype(BF16), wo_ref[...])
    o_ref[...] = _layer_norm(ALPHA * x + mix, g_ref[...], b_ref[...])


def _merge_sublayer(x2, branches, wg, bg, wb, wo, g, b, layer):
    T, D = x2.shape
    tm = TOKEN_TILE
    lay3 = lambda i: (layer, 0, 0)
    return pl.pallas_call(
        _merge_kernel,
        grid=(T // tm,),
        in_specs=[pl.BlockSpec((tm, D), lambda i: (i, 0))]
        + [pl.BlockSpec((tm, BRANCH_WIDTH), lambda i: (i, 0))] * N_BRANCH
        + [
            pl.BlockSpec((None, D, N_BRANCH * D), lay3),
            pl.BlockSpec((None, 1, N_BRANCH * D), lay3),
            pl.BlockSpec((None, N_BRANCH, BRANCH_WIDTH, D), lambda i: (layer, 0, 0, 0)),
            pl.BlockSpec((None, D, D), lay3),
            pl.BlockSpec((1, D), lambda i: (0, 0)),
            pl.BlockSpec((1, D), lambda i: (0, 0)),
        ],
        out_specs=pl.BlockSpec((tm, D), lambda i: (i, 0)),
        out_shape=jax.ShapeDtypeStruct((T, D), F32),
        compiler_params=_params(1),
        name="merge_sublayer",
    )(x2, *branches, wg, bg, wb, wo, g, b)


def _mem_kv_kernel(mem_ref, w_ref, o_ref):
    o_ref[0] = _nn(mem_ref[0].astype(BF16), w_ref[...]).astype(BF16)


def _mem_kv(mem, w_kv, layer):
    B, M, D = mem.shape
    n = w_kv.shape[-1]
    return pl.pallas_call(
        _mem_kv_kernel,
        grid=(B,),
        in_specs=[pl.BlockSpec((1, M, D), lambda i: (i, 0, 0)),
                  pl.BlockSpec((None, D, n), lambda i: (layer, 0, 0))],
        out_specs=pl.BlockSpec((1, M, n), lambda i: (i, 0, 0)),
        out_shape=jax.ShapeDtypeStruct((B, M, n), BF16),
        compiler_params=_params(1),
        name="mem_kv",
    )(mem, w_kv)


def _xattn_kernel(x_ref, kv_ref, wq_ref, wo_ref, g_ref, b_ref, o_ref):
    x = x_ref[0]
    q = (_nn(x.astype(BF16), wq_ref[...]) * HEAD_DIM ** -0.5).astype(BF16)
    kv = kv_ref[0]
    heads = []
    for h in range(N_HEADS):
        s = _nt(q[:, h * HEAD_DIM:(h + 1) * HEAD_DIM], kv[:, h * HEAD_DIM:(h + 1) * HEAD_DIM])
        e = jnp.exp(s - jnp.max(s, axis=-1, keepdims=True))
        p = (e / jnp.sum(e, axis=-1, keepdims=True)).astype(BF16)
        heads.append(_nn(p, kv[:, BRANCH_WIDTH + h * HEAD_DIM:BRANCH_WIDTH + (h + 1) * HEAD_DIM]))
    att = jnp.concatenate(heads, axis=1).astype(BF16)
    o_ref[0] = _layer_norm(ALPHA * x + _nn(att, wo_ref[...]), g_ref[...], b_ref[...])


def _xattn_sublayer(x, kv, wq, wo, g, b, layer):
    B, S, D = x.shape
    M, n = kv.shape[1:]
    tm = TOKEN_TILE
    lay3 = lambda i, j: (layer, 0, 0)
    return pl.pallas_call(
        _xattn_kernel,
        grid=(B, S // tm),
        in_specs=[
            pl.BlockSpec((1, tm, D), lambda i, j: (i, j, 0)),
            pl.BlockSpec((1, M, n), lambda i, j: (i, 0, 0)),
            pl.BlockSpec((None, D, BRANCH_WIDTH), lay3),
            pl.BlockSpec((None, BRANCH_WIDTH, D), lay3),
            pl.BlockSpec((1, D), lambda i, j: (0, 0)),
            pl.BlockSpec((1, D), lambda i, j: (0, 0)),
        ],
        out_specs=pl.BlockSpec((1, tm, D), lambda i, j: (i, j, 0)),
        out_shape=jax.ShapeDtypeStruct((B, S, D), F32),
        compiler_params=_params(2),
        name="xattn_sublayer",
    )(x, kv, wq, wo, g, b)


def _pad_cols(w, n):
    return jnp.pad(w, [(0, 0)] * (w.ndim - 1) + [(0, n - w.shape[-1])])


def _split_w_in(w_in, b_in):
    hd, bw = HEAD_DIM, BRANCH_WIDTH
    sizes = (bw, bw, bw, bw, hd, hd, hd, hd, hd, hd, 3 * N_HEADS, bw, bw, bw, N_HEADS,
             MLA_Q_RANK, MLA_KV_RANK, MLA_ROPE, N_BRANCH * D_MODEL)
    names = ("a_q", "a_k", "a_v", "b_q", "b_kc", "b_vc", "b_ks", "b_vs", "b_kw", "b_vw", "b_g",
             "c_q", "c_k", "c_v", "c_f", "d_cq", "d_ckv", "d_kr", "g_merge")
    w, b, off = {}, {}, 0
    for name, size in zip(names, sizes):
        w[name] = w_in[:, :, off:off + size]
        b[name] = b_in[:, off:off + size]
        off += size
    return w, b


def _row_major(ws, bs, names, pad_to):
    w = _pad_cols(jnp.concatenate([ws[n] for n in names], axis=-1), pad_to).astype(BF16)
    b = _pad_cols(jnp.concatenate([bs[n] for n in names], axis=-1), pad_to)[:, None, :]
    return w, b


def _channel_major(ws, bs, names):
    w = jnp.concatenate([ws[n] for n in names], axis=-1).transpose(0, 2, 1).astype(BF16)
    b = jnp.concatenate([bs[n] for n in names], axis=-1)[:, :, None]
    return w, b


def _rope_slab(a, swap):
    half = MLA_ROPE // 2
    first, second = a[..., :half], a[..., half:]
    body = jnp.concatenate([second, first] if swap else [first, second], axis=-1)
    return jnp.pad(body, [(0, 0)] * (a.ndim - 1) + [(MLA_NOPE, LANES - MLA_NOPE - MLA_ROPE)])


def _mla_weights(ws, bs, mla_w_uq, mla_w_ukv):
    L = mla_w_uq.shape[0]
    half = MLA_ROPE // 2
    w = jnp.concatenate([ws["d_cq"], ws["d_ckv"], _rope_slab(ws["d_kr"], False), _rope_slab(ws["d_kr"], True)],
                        axis=-1).astype(BF16)
    b = jnp.concatenate([bs["d_cq"], bs["d_ckv"], _rope_slab(bs["d_kr"], False), _rope_slab(bs["d_kr"], True)],
                        axis=-1)[:, None, :]
    uq = mla_w_uq.reshape(L, MLA_Q_RANK, N_HEADS, MLA_NOPE + MLA_ROPE).transpose(0, 2, 1, 3)
    wq = jnp.concatenate([uq[..., :MLA_NOPE], _rope_slab(uq[..., MLA_NOPE:], False)[..., MLA_NOPE:]],
                         axis=-1).astype(BF16)
    wqs = jnp.concatenate([jnp.zeros_like(uq[..., :MLA_NOPE]), _rope_slab(uq[..., MLA_NOPE:], True)[..., MLA_NOPE:]],
                          axis=-1).astype(BF16)
    ukv = mla_w_ukv.reshape(L, MLA_KV_RANK, N_HEADS, MLA_NOPE + HEAD_DIM).transpose(0, 2, 1, 3)
    wk = _pad_cols(ukv[..., :MLA_NOPE], QK_WIDTH).astype(BF16)
    wvt = ukv[..., MLA_NOPE:].transpose(0, 1, 3, 2).reshape(L, BRANCH_WIDTH, MLA_KV_RANK).astype(BF16)
    inv_freq = ROPE_BASE ** (-jnp.arange(0, MLA_ROPE, 2, dtype=F32) / MLA_ROPE)
    freq_row = _rope_slab(jnp.concatenate([inv_freq, inv_freq])[None, :], False)
    sign_row = _rope_slab(jnp.concatenate([-jnp.ones((half,), F32), jnp.ones((half,), F32)])[None, :], False)
    return w, b, wq, wqs, wk, wvt, freq_row, sign_row


def kernel(x, mem, positions, ln_g, ln_b, ffn1_w_in, ffn1_w_out, ffn2_w_in, ffn2_w_out, w_in, b_in, w_branch, w_o,
           mla_q_norm, mla_w_uq, mla_kv_norm, mla_w_ukv, nsa_cmp_pos, nsa_cmp_w1, nsa_cmp_w2, xa_w_q, xa_w_kv, xa_w_o):
    B, S, D = x.shape
    L = ln_g.shape[0]
    nc = D_FF // FFN_TILE_F

    def ffn_weights(w_i, w_out):
        wg = w_i[:, :, :D_FF].reshape(L, D, nc, FFN_TILE_F).transpose(0, 2, 1, 3).astype(BF16)
        wu = w_i[:, :, D_FF:].reshape(L, D, nc, FFN_TILE_F).transpose(0, 2, 1, 3).astype(BF16)
        return wg, wu, w_out.reshape(L, nc, FFN_TILE_F, D).astype(BF16)

    ffn1 = ffn_weights(ffn1_w_in, ffn1_w_out)
    ffn2 = ffn_weights(ffn2_w_in, ffn2_w_out)

    ws, bs = _split_w_in(w_in, b_in)
    moba_w, moba_b = _row_major(ws, bs, ("a_q", "a_k"), 2 * BRANCH_WIDTH)
    moba_wt, moba_bt = _channel_major(ws, bs, ("a_v",))
    nsa_w, nsa_b = _row_major(ws, bs, ("b_q", "b_ks", "b_kw", "b_kc", "b_vc", "b_g"), 5 * LANES)
    nsa_wt, nsa_bt = _channel_major(ws, bs, ("b_vs", "b_vw"))
    fox_w, fox_b = _row_major(ws, bs, ("c_q", "c_k", "c_f"), 5 * LANES)
    fox_wt, fox_bt = _channel_major(ws, bs, ("c_v",))

    mla_w, mla_b, mla_wq, mla_wqs, mla_wk, mla_wvt, freq_row, sign_row = _mla_weights(ws, bs, mla_w_uq, mla_w_ukv)
    cos, sin = _rope_tables(positions, freq_row, sign_row)

    merge_wg = ws["g_merge"].astype(BF16)
    merge_bg = bs["g_merge"][:, None, :]
    merge_wb = w_branch.astype(BF16)
    merge_wo = w_o.astype(BF16)
    nsa_w1 = nsa_cmp_w1.reshape(L, 2, NSA_CMP_LEN, HEAD_DIM, NSA_CMP_HIDDEN)
    xa_wq, xa_wkv, xa_wo = xa_w_q.astype(BF16), xa_w_kv.astype(BF16), xa_w_o.astype(BF16)

    for l in range(L):
        ln = lambda i: (ln_g[l, i][None, :], ln_b[l, i][None, :])
        x2 = _ffn_sublayer(x.reshape(B * S, D), *ffn1, *ln(0), l)
        x = x2.reshape(B, S, D)
        branches = (
            _moba(x, moba_w, moba_b, moba_wt, moba_bt, l),
            _nsa(x, nsa_w, nsa_b, nsa_wt, nsa_bt, nsa_cmp_pos, nsa_w1, nsa_cmp_w2, l),
            _fox(x, fox_w, fox_b, fox_wt, fox_bt, l),
            _mla(x, cos, sin, mla_w, mla_b, mla_q_norm[:, None, :], mla_kv_norm[:, None, :],
                 mla_wq, mla_wqs, mla_wk, mla_wvt, l),
        )
        x2 = _merge_sublayer(x2, [o.reshape(B * S, BRANCH_WIDTH) for o in branches],
                             merge_wg, merge_bg, merge_wb, merge_wo, *ln(1), l)
        x = _xattn_sublayer(x2.reshape(B, S, D), _mem_kv(mem, xa_wkv, l), xa_wq, xa_wo, *ln(2), l)
        x = _ffn_sublayer(x.reshape(B * S, D), *ffn2, *ln(3), l).reshape(B, S, D)
    return x
```

```python
import functools

import jax
import jax.numpy as jnp
from jax import lax
from jax.experimental import pallas as pl
from jax.experimental.pallas import tpu as pltpu

F32 = jnp.float32
BF16 = jnp.bfloat16
HIGHEST = lax.Precision.HIGHEST

D_MODEL = 1024
DEPTH = 4
HEAD_DIM = 64
N_HEADS = 4
BRANCH_WIDTH = N_HEADS * HEAD_DIM
N_BRANCH = 4
MOBA_BLOCK = 256
MOBA_TOPK = 3
NSA_CMP_LEN = 32
NSA_CMP_STRIDE = 16
NSA_CMP_HIDDEN = 256
NSA_SLC_BLOCK = 64
NSA_TOP_N = 16
NSA_WINDOW = 512
NSA_FORCE_BONUS = 1.0e4
MLA_Q_RANK = 256
MLA_KV_RANK = 128
MLA_NOPE = 64
MLA_ROPE = 32
ROPE_BASE = 10000.0
MEM_LEN = 256
D_FF = 2816
ALPHA = (2 * DEPTH) ** 0.25
LN_EPS = 1e-5
RMS_EPS = 1e-6
NEG_INF = -1e30
MOBA_SLOPES = tuple(2.0 ** -(2 * h + 1) for h in range(N_HEADS))
NSA_SLOPES = tuple(2.0 ** -(2 * h + 2) for h in range(N_HEADS))

LANES = 128
TQ = 256
TK = 256
QK_WIDTH = 128
ROW_CHUNK = 512
FFN_TILE_F = 256
TOKEN_TILE = 512
VMEM_LIMIT = 56 * 1024 * 1024


def _nn(a, b):
    return jnp.dot(a, b, preferred_element_type=F32)


def _nt(a, b, precision=None):
    return lax.dot_general(a, b, (((1,), (1,)), ((), ())), precision=precision,
                           preferred_element_type=F32)


def _layer_norm(z, g, b):
    mu = jnp.mean(z, -1, keepdims=True)
    d = z - mu
    var = jnp.mean(d * d, -1, keepdims=True)
    return d * lax.rsqrt(var + LN_EPS) * g + b


def _params(n_parallel):
    return pltpu.CompilerParams(dimension_semantics=("parallel",) * n_parallel,
                                vmem_limit_bytes=VMEM_LIMIT)


def _ffn_kernel(x_ref, wg_ref, wu_ref, wo_ref, g_ref, b_ref, o_ref, acc_ref):
    x = x_ref[...]
    xb = x.astype(BF16)
    acc_ref[...] = jnp.zeros_like(acc_ref)

    def chunk(c, carry):
        hg = _nn(xb, wg_ref[c])
        hu = _nn(xb, wu_ref[c])
        a = (hg * jax.nn.sigmoid(hg) * hu).astype(BF16)
        acc_ref[...] += _nn(a, wo_ref[c])
        return carry

    lax.fori_loop(0, wg_ref.shape[0], chunk, 0)
    o_ref[...] = _layer_norm(ALPHA * x + 0.5 * acc_ref[...], g_ref[...], b_ref[...])


def _ffn_sublayer(x2, wg, wu, wo, g, b, layer):
    T, D = x2.shape
    nc, _, tf = wg.shape[1:]
    tm = TOKEN_TILE
    return pl.pallas_call(
        _ffn_kernel,
        grid=(T // tm,),
        in_specs=[
            pl.BlockSpec((tm, D), lambda i: (i, 0)),
            pl.BlockSpec((None, nc, D, tf), lambda i: (layer, 0, 0, 0)),
            pl.BlockSpec((None, nc, D, tf), lambda i: (layer, 0, 0, 0)),
            pl.BlockSpec((None, nc, tf, D), lambda i: (layer, 0, 0, 0)),
            pl.BlockSpec((1, D), lambda i: (0, 0)),
            pl.BlockSpec((1, D), lambda i: (0, 0)),
        ],
        out_specs=pl.BlockSpec((tm, D), lambda i: (i, 0)),
        out_shape=jax.ShapeDtypeStruct((T, D), F32),
        scratch_shapes=[pltpu.VMEM((tm, D), F32)],
        compiler_params=_params(1),
        name="ffn_sublayer",
    )(x2, wg, wu, wo, g, b)


def _tile_iotas():
    r = lax.broadcasted_iota(jnp.int32, (TK, TQ), 0)
    c = lax.broadcasted_iota(jnp.int32, (TK, TQ), 1)
    return r, c


def _scores(s_ref, n_tiles, score_tile, h):
    m = None
    for j in range(n_tiles):
        s = score_tile(h, j)
        s_ref[j * TK:(j + 1) * TK, :] = s
        mj = jnp.max(s, axis=0, keepdims=True)
        m = mj if m is None else jnp.maximum(m, mj)
    return m


def _weighted_values(s_ref, n_tiles, m, vt_cols):
    ps = [jnp.exp(s_ref[j * TK:(j + 1) * TK, :] - m) for j in range(n_tiles)]
    l = functools.reduce(jnp.add, [jnp.sum(p, axis=0, keepdims=True) for p in ps])
    p_all = jnp.concatenate([p.astype(BF16) for p in ps], axis=0)
    return _nn(vt_cols, p_all) / jnp.maximum(l, 1e-30)


def _attend_heads(n_tiles, score_tile, vt_cols, out_ref, s_refs):
    s_a, s_b = s_refs
    m_first = _scores(s_a, n_tiles, score_tile, 0)

    def pair(i, m_a):
        h_a, h_b = 2 * i, 2 * i + 1
        h_next = jnp.minimum(2 * i + 2, N_HEADS - 1)
        m_b = _scores(s_b, n_tiles, score_tile, h_b)
        out_ref[h_a] = _weighted_values(s_a, n_tiles, m_a, vt_cols(h_a, n_tiles))
        m_next = _scores(s_a, n_tiles, score_tile, h_next)
        out_ref[h_b] = _weighted_values(s_b, n_tiles, m_b, vt_cols(h_b, n_tiles))
        return m_next

    lax.fori_loop(0, N_HEADS // 2, pair, m_first)


def _position_columns(n_rows, first, step):
    row = lax.broadcasted_iota(jnp.int32, (n_rows, HEAD_DIM), 0)
    lane = lax.broadcasted_iota(jnp.int32, (n_rows, HEAD_DIM), 1)
    pos = first + step * row
    hi = ((pos >> 7) << 7).astype(F32)
    lo = (pos & 127).astype(F32)
    return jnp.where(lane == 0, hi, jnp.where(lane == 1, lo, 0.0))


def _query_columns(n_rows, value, n_cols):
    lane = lax.broadcasted_iota(jnp.int32, (n_rows, HEAD_DIM), 1)
    return jnp.where(lane < n_cols, value, 0.0).astype(F32)


def _project(x_ref, w_ref, b_ref, wt_ref, bt_ref, rm_ref, cm_ref):
    S = x_ref.shape[1]
    for i in range(S // ROW_CHUNK):
        rows = slice(i * ROW_CHUNK, (i + 1) * ROW_CHUNK)
        xb = x_ref[0, rows, :].astype(BF16)
        rm_ref[rows, :] = _nn(xb, w_ref[...]) + b_ref[...]
        cm_ref[:, rows] = _nt(wt_ref[...], xb) + bt_ref[...]


def _causal_attention(o_ref, qp_ref, kp_ref, vt_ref, ot_ref, s_refs):
    S = qp_ref.shape[1]
    r, c = _tile_iotas()
    for qi in range(S // TQ):
        q0 = qi * TQ

        def score_tile(h, j, qi=qi, q0=q0):
            s = _nt(kp_ref[h, j * TK:(j + 1) * TK, :], qp_ref[h, q0:q0 + TQ, :])
            return jnp.where(r <= c, s, NEG_INF) if j == qi else s

        _attend_heads(qi + 1, score_tile, lambda h, n: vt_ref[h, :, 0:n * TK], ot_ref, s_refs)
        _store_heads(o_ref, q0, ot_ref[...].reshape(BRANCH_WIDTH, TQ))


def _store_heads(o_ref, q0, heads_t):
    o_ref[0, q0:q0 + TQ, :] = heads_t.T


def _attention_scratch(S):
    return [pltpu.VMEM((N_HEADS, HEAD_DIM, TQ), F32), pltpu.VMEM((S, TQ), F32), pltpu.VMEM((S, TQ), F32)]


def _moba_kernel(x_ref, w_ref, b_ref, wt_ref, bt_ref, o_ref, rm_ref, cm_ref, qp_ref, kp_ref, vt_ref, sel_ref,
                 ot_ref, sa_ref, sb_ref):
    S = x_ref.shape[1]
    nb = S // MOBA_BLOCK
    _project(x_ref, w_ref, b_ref, wt_ref, bt_ref, rm_ref, cm_ref)
    vt_ref[...] = cm_ref[...].astype(BF16).reshape(N_HEADS, HEAD_DIM, S)
    kcols = _position_columns(S, 0, 1)
    blk = lax.broadcasted_iota(jnp.int32, (nb, S), 0)
    own = lax.broadcasted_iota(jnp.int32, (nb, S), 1) // MOBA_BLOCK
    cand = blk < own
    for h in range(N_HEADS):
        q = rm_ref[:, h * HEAD_DIM:(h + 1) * HEAD_DIM]
        k = rm_ref[:, BRANCH_WIDTH + h * HEAD_DIM:BRANCH_WIDTH + (h + 1) * HEAD_DIM]
        qp_ref[h] = jnp.concatenate(
            [q * HEAD_DIM ** -0.5, _query_columns(S, MOBA_SLOPES[h], 2)], axis=1).astype(BF16)
        kp_ref[h] = jnp.concatenate([k, kcols], axis=1).astype(BF16)
        k_mean = jnp.sum(k.reshape(nb, MOBA_BLOCK, HEAD_DIM), axis=1) * (1.0 / MOBA_BLOCK)
        gate = jnp.where(cand, _nt(k_mean, q, precision=HIGHEST), NEG_INF)
        beaten = jnp.zeros((nb, S), F32)
        for j in range(nb):
            other = gate[j:j + 1, :]
            wins = (other > gate) | ((other == gate) & (blk > j))
            beaten = beaten + jnp.where(wins, 1.0, 0.0)
        sel = jnp.where(cand & (beaten < MOBA_TOPK), 1.0, 0.0)
        for qi in range(S // TQ):
            sel_ref[h, qi] = sel[:, qi * TQ:(qi + 1) * TQ]

    r, c = _tile_iotas()
    for qi in range(S // TQ):
        q0 = qi * TQ

        def score_tile(h, j, qi=qi, q0=q0):
            s = _nt(kp_ref[h, j * TK:(j + 1) * TK, :], qp_ref[h, q0:q0 + TQ, :])
            return jnp.where(r <= c if j == qi else sel_ref[h, qi, j:j + 1, :] > 0.5, s, NEG_INF)

        _attend_heads(qi + 1, score_tile, lambda h, n: vt_ref[h, :, 0:n * TK], ot_ref, (sa_ref, sb_ref))
        _store_heads(o_ref, q0, ot_ref[...].reshape(BRANCH_WIDTH, TQ))


def _moba(x, w, b, wt, bt, layer):
    B, S, D = x.shape
    n_rm = w.shape[-1]
    n_cm = wt.shape[1]
    nb = S // MOBA_BLOCK
    return pl.pallas_call(
        _moba_kernel,
        grid=(B,),
        in_specs=[
            pl.BlockSpec((1, S, D), lambda i: (i, 0, 0)),
            pl.BlockSpec((None, D, n_rm), lambda i: (layer, 0, 0)),
            pl.BlockSpec((None, 1, n_rm), lambda i: (layer, 0, 0)),
            pl.BlockSpec((None, n_cm, D), lambda i: (layer, 0, 0)),
            pl.BlockSpec((None, n_cm, 1), lambda i: (layer, 0, 0)),
        ],
        out_specs=pl.BlockSpec((1, S, BRANCH_WIDTH), lambda i: (i, 0, 0)),
        out_shape=jax.ShapeDtypeStruct((B, S, BRANCH_WIDTH), F32),
        scratch_shapes=[
            pltpu.VMEM((S, n_rm), F32),
            pltpu.VMEM((n_cm, S), F32),
            pltpu.VMEM((N_HEADS, S, QK_WIDTH), BF16),
            pltpu.VMEM((N_HEADS, S, QK_WIDTH), BF16),
            pltpu.VMEM((N_HEADS, HEAD_DIM, S), BF16),
            pltpu.VMEM((N_HEADS, S // TQ, nb, TQ), F32),
        ] + _attention_scratch(S),
        compiler_params=_params(1),
        name="moba_mixer",
    )(x, w, b, wt, bt)


def _fox_kernel(x_ref, w_ref, b_ref, wt_ref, bt_ref, o_ref, rm_ref, cm_ref, qp_ref, kp_ref, vt_ref,
                ot_ref, sa_ref, sb_ref):
    S = x_ref.shape[1]
    _project(x_ref, w_ref, b_ref, wt_ref, bt_ref, rm_ref, cm_ref)
    vt_ref[...] = cm_ref[...].astype(BF16).reshape(N_HEADS, HEAD_DIM, S)
    f = rm_ref[:, 2 * BRANCH_WIDTH:2 * BRANCH_WIDTH + LANES]
    log_sig = jnp.minimum(f, 0.0) - jnp.log1p(jnp.exp(-jnp.abs(f)))
    blk = MOBA_BLOCK
    tri = jnp.where(lax.broadcasted_iota(jnp.int32, (blk, blk), 0) >= lax.broadcasted_iota(jnp.int32, (blk, blk), 1),
                    1.0, 0.0).astype(F32)
    carry = jnp.zeros((1, LANES), F32)
    pieces = []
    for i in range(S // blk):
        cs = jnp.dot(tri, log_sig[i * blk:(i + 1) * blk, :], precision=HIGHEST,
                     preferred_element_type=F32) + carry
        pieces.append(cs)
        carry = cs[blk - 1:blk, :]
    neg_decay = -jnp.concatenate(pieces, axis=0)
    lane = lax.broadcasted_iota(jnp.int32, (S, HEAD_DIM), 1)
    for h in range(N_HEADS):
        q = rm_ref[:, h * HEAD_DIM:(h + 1) * HEAD_DIM]
        k = rm_ref[:, BRANCH_WIDTH + h * HEAD_DIM:BRANCH_WIDTH + (h + 1) * HEAD_DIM]
        d = neg_decay[:, h:h + 1]
        d_hi = d.astype(BF16).astype(F32)
        d_mid = (d - d_hi).astype(BF16).astype(F32)
        d_lo = d - d_hi - d_mid
        dcols = jnp.where(lane == 0, d_hi, jnp.where(lane == 1, d_mid, jnp.where(lane == 2, d_lo, 0.0)))
        qp_ref[h] = jnp.concatenate([q * HEAD_DIM ** -0.5, _query_columns(S, 1.0, 3)], axis=1).astype(BF16)
        kp_ref[h] = jnp.concatenate([k, dcols], axis=1).astype(BF16)

    _causal_attention(o_ref, qp_ref, kp_ref, vt_ref, ot_ref, (sa_ref, sb_ref))


def _fox(x, w, b, wt, bt, layer):
    B, S, D = x.shape
    n_rm = w.shape[-1]
    n_cm = wt.shape[1]
    return pl.pallas_call(
        _fox_kernel,
        grid=(B,),
        in_specs=[
            pl.BlockSpec((1, S, D), lambda i: (i, 0, 0)),
            pl.BlockSpec((None, D, n_rm), lambda i: (layer, 0, 0)),
            pl.BlockSpec((None, 1, n_rm), lambda i: (layer, 0, 0)),
            pl.BlockSpec((None, n_cm, D), lambda i: (layer, 0, 0)),
            pl.BlockSpec((None, n_cm, 1), lambda i: (layer, 0, 0)),
        ],
        out_specs=pl.BlockSpec((1, S, BRANCH_WIDTH), lambda i: (i, 0, 0)),
        out_shape=jax.ShapeDtypeStruct((B, S, BRANCH_WIDTH), F32),
        scratch_shapes=[
            pltpu.VMEM((S, n_rm), F32),
            pltpu.VMEM((n_cm, S), F32),
            pltpu.VMEM((N_HEADS, S, QK_WIDTH), BF16),
            pltpu.VMEM((N_HEADS, S, QK_WIDTH), BF16),
            pltpu.VMEM((N_HEADS, HEAD_DIM, S), BF16),
        ] + _attention_scratch(S),
        compiler_params=_params(1),
        name="fox_mixer",
    )(x, w, b, wt, bt)


def _rope_table_kernel(pos_ref, freq_ref, sign_ref, cos_ref, sin_ref):
    ang = pos_ref[0].astype(F32) * freq_ref[...]
    cos_ref[0] = jnp.cos(ang)
    sin_ref[0] = jnp.sin(ang) * sign_ref[...]


def _rope_tables(positions, freq_row, sign_row):
    B, S = positions.shape
    return pl.pallas_call(
        _rope_table_kernel,
        grid=(B,),
        in_specs=[
            pl.BlockSpec((1, S, 1), lambda i: (i, 0, 0)),
            pl.BlockSpec((1, LANES), lambda i: (0, 0)),
            pl.BlockSpec((1, LANES), lambda i: (0, 0)),
        ],
        out_specs=[pl.BlockSpec((1, S, LANES), lambda i: (i, 0, 0))] * 2,
        out_shape=[jax.ShapeDtypeStruct((B, S, LANES), F32)] * 2,
        compiler_params=_params(1),
        name="rope_tables",
    )(positions.reshape(B, S, 1), freq_row, sign_row)


def _rms_norm(x, g):
    return x * lax.rsqrt(jnp.mean(x * x, -1, keepdims=True) + RMS_EPS) * g


def _mla_kernel(x_ref, cos_ref, sin_ref, w_ref, b_ref, qn_ref, kvn_ref, wq_ref, wqs_ref, wk_ref, wvt_ref,
                o_ref, rm_ref, cm_ref, qp_ref, kp_ref, vt_ref, ot_ref, sa_ref, sb_ref):
    S = x_ref.shape[1]
    for i in range(S // ROW_CHUNK):
        rows = slice(i * ROW_CHUNK, (i + 1) * ROW_CHUNK)
        xb = x_ref[0, rows, :].astype(BF16)
        rm_ref[rows, :] = _nn(xb, w_ref[...]) + b_ref[...]
    cos = cos_ref[0]
    sin = sin_ref[0]
    scale = (MLA_NOPE + MLA_ROPE) ** -0.5
    q_lat = _rms_norm(rm_ref[:, 0:MLA_Q_RANK], qn_ref[...]).astype(BF16)
    kv_lat = _rms_norm(rm_ref[:, MLA_Q_RANK:MLA_Q_RANK + MLA_KV_RANK], kvn_ref[...]).astype(BF16)
    c0 = MLA_Q_RANK + MLA_KV_RANK
    k_rope = rm_ref[:, c0:c0 + LANES] * cos + rm_ref[:, c0 + LANES:c0 + 2 * LANES] * sin
    cm_ref[...] = _nt(wvt_ref[...], kv_lat)
    vt_ref[...] = cm_ref[...].astype(BF16).reshape(N_HEADS, HEAD_DIM, S)
    for h in range(N_HEADS):
        qh =_nn(q_lat, wq_ref[h]) * cos + _nn(q_lat, wqs_ref[h]) * sin
        qp_ref[h] = (qh * scale).astype(BF16)
        kp_ref[h] = (_nn(kv_lat, wk_ref[h]) + k_rope).astype(BF16)

    _causal_attention(o_ref, qp_ref, kp_ref, vt_ref, ot_ref, (sa_ref, sb_ref))


def _mla(x, cos, sin, w, b, qn, kvn, wq, wqs, wk, wvt, layer):
    B, S, D = x.shape
    n_rm = w.shape[-1]
    lay3 = lambda i: (layer, 0, 0)
    lay4 = lambda i: (layer, 0, 0, 0)
    return pl.pallas_call(
        _mla_kernel,
        grid=(B,),
        in_specs=[
            pl.BlockSpec((1, S, D), lambda i: (i, 0, 0)),
            pl.BlockSpec((1, S, LANES), lambda i: (i, 0, 0)),
            pl.BlockSpec((1, S, LANES), lambda i: (i, 0, 0)),
            pl.BlockSpec((None, D, n_rm), lay3),
            pl.BlockSpec((None, 1, n_rm), lay3),
            pl.BlockSpec((None, 1, MLA_Q_RANK), lay3),
            pl.BlockSpec((None, 1, MLA_KV_RANK), lay3),
            pl.BlockSpec((None, N_HEADS, MLA_Q_RANK, QK_WIDTH), lay4),
            pl.BlockSpec((None, N_HEADS, MLA_Q_RANK, QK_WIDTH), lay4),
            pl.BlockSpec((None, N_HEADS, MLA_KV_RANK, QK_WIDTH), lay4),
            pl.BlockSpec((None, BRANCH_WIDTH, MLA_KV_RANK), lay3),
        ],
        out_specs=pl.BlockSpec((1, S, BRANCH_WIDTH), lambda i: (i, 0, 0)),
        out_shape=jax.ShapeDtypeStruct((B, S, BRANCH_WIDTH), F32),
        scratch_shapes=[
            pltpu.VMEM((S, n_rm), F32),
            pltpu.VMEM((BRANCH_WIDTH, S), F32),
            pltpu.VMEM((N_HEADS, S, QK_WIDTH), BF16),
            pltpu.VMEM((N_HEADS, S, QK_WIDTH), BF16),
            pltpu.VMEM((N_HEADS, HEAD_DIM, S), BF16),
        ] + _attention_scratch(S),
        compiler_params=_params(1),
        name="mla_mixer",
    )(x, cos, sin, w, b, qn, kvn, wq, wqs, wk, wvt)


N_CMP_PAD = 128


def _nsa_compress(src_ref, pos_ref, w1_ref, w2_ref):
    accs = [jnp.zeros((N_CMP_PAD, NSA_CMP_HIDDEN), F32) for _ in range(2)]
    n_cmp = N_CMP_PAD - 1
    for p in range(NSA_CMP_LEN):
        rows = src_ref[pl.ds(p, n_cmp, stride=NSA_CMP_STRIDE), :]
        for which in range(2):
            part = rows[:, which * HEAD_DIM:(which + 1) * HEAD_DIM] + pos_ref[which, p:p + 1, :]
            part = jnp.concatenate([part, jnp.zeros((1, HEAD_DIM), F32)], axis=0)
            accs[which] = accs[which] + jnp.dot(part, w1_ref[which, p], precision=HIGHEST,
                                                preferred_element_type=F32)
    return [jnp.dot(jax.nn.gelu(accs[which]), w2_ref[which], precision=HIGHEST, preferred_element_type=F32)
            for which in range(2)]


def _nsa_kernel(x_ref, w_ref, b_ref, wt_ref, bt_ref, pos_ref, w1_ref, w2_ref, o_ref,
                rm_ref, cm_ref, cin_ref, qp_ref, ks_ref, kw_ref, kc_ref, vt_ref, vtc_ref, sel_ref,
                oslc_ref, owin_ref, sa_ref, sb_ref):
    S = x_ref.shape[1]
    n_slc = S // NSA_SLC_BLOCK
    _project(x_ref, w_ref, b_ref, wt_ref, bt_ref, rm_ref, cm_ref)
    vt_ref[...] = cm_ref[...].astype(BF16).reshape(2, HEAD_DIM, S)
    c_ks, c_kw, c_kc, c_vc, c_g = (BRANCH_WIDTH + i * HEAD_DIM for i in range(5))
    kcols = _position_columns(S, 0, 1)
    for h in range(N_HEADS):
        q = rm_ref[:, h * HEAD_DIM:(h + 1) * HEAD_DIM]
        qp_ref[h] = jnp.concatenate(
            [q * HEAD_DIM ** -0.5, _query_columns(S, NSA_SLOPES[h], 2)], axis=1).astype(BF16)
    ks_ref[...] = jnp.concatenate([rm_ref[:, c_ks:c_ks + HEAD_DIM], kcols], axis=1).astype(BF16)
    kw_ref[...] = jnp.concatenate([rm_ref[:, c_kw:c_kw + HEAD_DIM], kcols], axis=1).astype(BF16)
    cin_ref[...] = rm_ref[:, c_kc:c_kc + 2 * HEAD_DIM]
    k_cmp, v_cmp = _nsa_compress(cin_ref, pos_ref, w1_ref, w2_ref)
    kc_ref[...] = jnp.concatenate(
        [k_cmp, _position_columns(N_CMP_PAD, NSA_CMP_LEN - 1, NSA_CMP_STRIDE)], axis=1).astype(BF16)
    vtc_ref[...] = jnp.concatenate([k_cmp, v_cmp], axis=1).T[HEAD_DIM:, :].astype(BF16)

    jj = lax.broadcasted_iota(jnp.int32, (n_slc, N_CMP_PAD), 0) * NSA_SLC_BLOCK
    cc = lax.broadcasted_iota(jnp.int32, (n_slc, N_CMP_PAD), 1) * NSA_CMP_STRIDE
    overlap_t = jnp.where((cc < jj + NSA_SLC_BLOCK) & (cc + NSA_CMP_LEN > jj), 1.0, 0.0).astype(F32)

    r, c = _tile_iotas()
    per_tile = TK // NSA_SLC_BLOCK
    win_tiles = NSA_WINDOW // TK
    c_idx = lax.broadcasted_iota(jnp.int32, (N_CMP_PAD, TQ), 0)
    lane_cmp = lax.broadcasted_iota(jnp.int32, (N_CMP_PAD, TQ), 1)
    for qi in range(S // TQ):
        q0 = qi * TQ
        cmp_ok = (c_idx * NSA_CMP_STRIDE + (NSA_CMP_LEN - 1) <= q0 + lane_cmp) & (c_idx < N_CMP_PAD - 1)
        imp = jnp.zeros((n_slc, TQ), F32)
        o_cmp = []
        for h in range(N_HEADS):
            s = jnp.where(cmp_ok, _nt(kc_ref[...], qp_ref[h, q0:q0 + TQ, :]), NEG_INF)
            e = jnp.where(cmp_ok, jnp.exp(s - jnp.max(s, axis=0, keepdims=True)), 0.0)
            p = e / jnp.maximum(jnp.sum(e, axis=0, keepdims=True), 1e-30)
            o_cmp.append(_nn(vtc_ref[...], p.astype(BF16)))
            imp = imp + jnp.dot(overlap_t, p, precision=HIGHEST, preferred_element_type=F32)
        n_seen = per_tile * (qi + 1)
        select = n_seen > NSA_TOP_N
        if select:
            j_idx = lax.broadcasted_iota(jnp.int32, (n_seen, TQ), 0)
            own = (q0 + lax.broadcasted_iota(jnp.int32, (n_seen, TQ), 1)) // NSA_SLC_BLOCK
            forced = (j_idx == 0) | (j_idx == own) | (j_idx == own - 1)
            score = jnp.where(j_idx <= own, imp[:n_seen] + NSA_FORCE_BONUS * jnp.where(forced, 1.0, 0.0), NEG_INF)
            beaten = jnp.zeros((n_seen, TQ), F32)
            for j in range(n_seen):
                other = score[j:j + 1, :]
                wins = (other > score) | ((other == score) & (j_idx > j))
                beaten = beaten + jnp.where(wins, 1.0, 0.0)
            sel_ref[0:n_seen, :] = jnp.where(beaten < NSA_TOP_N, 1.0, 0.0)

        def slc_tile(h, j, qi=qi, q0=q0, select=select):
            s = _nt(ks_ref[j * TK:(j + 1) * TK, :], qp_ref[h, q0:q0 + TQ, :])
            mask = r <= c if j == qi else None
            if select:
                rows = [jnp.broadcast_to(sel_ref[per_tile * j + u:per_tile * j + u + 1, :], (NSA_SLC_BLOCK, TQ))
                        for u in range(per_tile)]
                chosen = jnp.concatenate(rows, axis=0) > 0.5
                mask = chosen if mask is None else mask & chosen
            return s if mask is None else jnp.where(mask, s, NEG_INF)

        _attend_heads(qi + 1, slc_tile, lambda h, n: vt_ref[0, :, 0:n * TK], oslc_ref, (sa_ref, sb_ref))

        lo = max(qi - win_tiles, 0)

        def win_tile(h, j, qi=qi, q0=q0, lo=lo):
            t = lo + j
            s = _nt(kw_ref[t * TK:(t + 1) * TK, :], qp_ref[h, q0:q0 + TQ, :])
            if t == qi:
                return jnp.where(r <= c, s, NEG_INF)
            return jnp.where(c < r, s, NEG_INF) if t == qi - win_tiles else s

        _attend_heads(qi - lo + 1, win_tile, lambda h, n, lo=lo: vt_ref[1, :, lo * TK:(lo + n) * TK], owin_ref,
                      (sa_ref, sb_ref))

        gates = jax.nn.sigmoid(rm_ref[q0:q0 + TQ, c_g:c_g + LANES]).T
        outs = [gates[3 * h:3 * h + 1, :] * o_cmp[h] + gates[3 * h + 1:3 * h + 2, :] * oslc_ref[h]
                + gates[3 * h + 2:3 * h + 3, :] * owin_ref[h] for h in range(N_HEADS)]
        _store_heads(o_ref, q0, jnp.concatenate(outs, axis=0))


def _nsa(x, w, b, wt, bt, pos, w1, w2, layer):
    B, S, D = x.shape
    n_rm = w.shape[-1]
    n_cm = wt.shape[1]
    lay3 = lambda i: (layer, 0, 0)
    lay4 = lambda i: (layer, 0, 0, 0)
    lay5 = lambda i: (layer, 0, 0, 0, 0)
    return pl.pallas_call(
        _nsa_kernel,
        grid=(B,),
        in_specs=[
            pl.BlockSpec((1, S, D), lambda i: (i, 0, 0)),
            pl.BlockSpec((None, D, n_rm), lay3),
            pl.BlockSpec((None, 1, n_rm), lay3),
            pl.BlockSpec((None, n_cm, D), lay3),
            pl.BlockSpec((None, n_cm, 1), lay3),
            pl.BlockSpec((None, 2, NSA_CMP_LEN, HEAD_DIM), lay4),
            pl.BlockSpec((None, 2, NSA_CMP_LEN, HEAD_DIM, NSA_CMP_HIDDEN), lay5),
            pl.BlockSpec((None, 2, NSA_CMP_HIDDEN, HEAD_DIM), lay4),
        ],
        out_specs=pl.BlockSpec((1, S, BRANCH_WIDTH), lambda i: (i, 0, 0)),
        out_shape=jax.ShapeDtypeStruct((B, S, BRANCH_WIDTH), F32),
        scratch_shapes=[
            pltpu.VMEM((S, n_rm), F32),
            pltpu.VMEM((n_cm, S), F32),
            pltpu.VMEM((S, 2 * HEAD_DIM), F32),
            pltpu.VMEM((N_HEADS, S, QK_WIDTH), BF16),
            pltpu.VMEM((S, QK_WIDTH), BF16),
            pltpu.VMEM((S, QK_WIDTH), BF16),
            pltpu.VMEM((N_CMP_PAD, QK_WIDTH), BF16),
            pltpu.VMEM((2, HEAD_DIM, S), BF16),
            pltpu.VMEM((HEAD_DIM, N_CMP_PAD), BF16),
            pltpu.VMEM((S // NSA_SLC_BLOCK, TQ), F32),
            pltpu.VMEM((N_HEADS, HEAD_DIM, TQ), F32),
        ] + _attention_scratch(S),
        compiler_params=_params(1),
        name="nsa_mixer",
    )(x, w, b, wt, bt, pos, w1, w2)


def _merge_kernel(x_ref, oa_ref, ob_ref, oc_ref, od_ref, wg_ref, bg_ref, wb_ref, wo_ref, g_ref, b_ref, o_ref):
    x = x_ref[...]
    xb = x.astype(BF16)
    D = x.shape[-1]
    acc = jnp.zeros(x.shape, F32)
    for n, branch in enumerate((oa_ref, ob_ref, oc_ref, od_ref)):
        gate = jax.nn.sigmoid(_nn(xb, wg_ref[:, n * D:(n + 1) * D]) + bg_ref[:, n * D:(n + 1) * D])
        acc = acc + gate * _nn(branch[...].astype(BF16), wb_ref[n])
    mix = _nn(acc.astype(BF16), wo_ref[...])
    o_ref[...] = _layer_norm(ALPHA * x + mix, g_ref[...], b_ref[...])


def _merge_sublayer(x2, branches, wg, bg, wb, wo, g, b, layer):
    T, D = x2.shape
    tm = TOKEN_TILE
    lay3 = lambda i: (layer, 0, 0)
    return pl.pallas_call(
        _merge_kernel,
        grid=(T // tm,),
        in_specs=[pl.BlockSpec((tm, D), lambda i: (i, 0))]
        + [pl.BlockSpec((tm, BRANCH_WIDTH), lambda i: (i, 0))] * N_BRANCH
        + [
            pl.BlockSpec((None, D, N_BRANCH * D), lay3),
            pl.BlockSpec((None, 1, N_BRANCH * D), lay3),
            pl.BlockSpec((None, N_BRANCH, BRANCH_WIDTH, D), lambda i: (layer, 0, 0, 0)),
            pl.BlockSpec((None, D, D), lay3),
            pl.BlockSpec((1, D), lambda i: (0, 0)),
            pl.BlockSpec((1, D), lambda i: (0, 0)),
        ],
        out_specs=pl.BlockSpec((tm, D), lambda i: (i, 0)),
        out_shape=jax.ShapeDtypeStruct((T, D), F32),
        compiler_params=_params(1),
        name="merge_sublayer",
    )(x2, *branches, wg, bg, wb, wo, g, b)


def _mem_kv_kernel(mem_ref, w_ref, o_ref):
    o_ref[0] = _nn(mem_ref[0].astype(BF16), w_ref[...]).astype(BF16)


def _mem_kv(mem, w_kv, layer):
    B, M, D = mem.shape
    n = w_kv.shape[-1]
    return pl.pallas_call(
        _mem_kv_kernel,
        grid=(B,),
        in_specs=[pl.BlockSpec((1, M, D), lambda i: (i, 0, 0)),
                  pl.BlockSpec((None, D, n), lambda i: (layer, 0, 0))],
        out_specs=pl.BlockSpec((1, M, n), lambda i: (i, 0, 0)),
        out_shape=jax.ShapeDtypeStruct((B, M, n), BF16),
        compiler_params=_params(1),
        name="mem_kv",
    )(mem, w_kv)


def _xattn_kernel(x_ref, kv_ref, wq_ref, wo_ref, g_ref, b_ref, o_ref):
    x = x_ref[0]
    q = (_nn(x.astype(BF16), wq_ref[...]) * HEAD_DIM ** -0.5).astype(BF16)
    kv = kv_ref[0]
    heads = []
    for h in range(N_HEADS):
        s = _nt(q[:, h * HEAD_DIM:(h + 1) * HEAD_DIM], kv[:, h * HEAD_DIM:(h + 1) * HEAD_DIM])
        e = jnp.exp(s - jnp.max(s, axis=-1, keepdims=True))
        p = (e / jnp.sum(e, axis=-1, keepdims=True)).astype(BF16)
        heads.append(_nn(p, kv[:, BRANCH_WIDTH + h * HEAD_DIM:BRANCH_WIDTH + (h + 1) * HEAD_DIM]))
    att = jnp.concatenate(heads, axis=1).astype(BF16)
    o_ref[0] = _layer_norm(ALPHA * x + _nn(att, wo_ref[...]), g_ref[...], b_ref[...])


def _xattn_sublayer(x, kv, wq, wo, g, b, layer):
    B, S, D = x.shape
    M, n = kv.shape[1:]
    tm = TOKEN_TILE
    lay3 = lambda i, j: (layer, 0, 0)
    return pl.pallas_call(
        _xattn_kernel,
        grid=(B, S // tm),
        in_specs=[
            pl.BlockSpec((1, tm, D), lambda i, j: (i, j, 0)),
            pl.BlockSpec((1, M, n), lambda i, j: (i, 0, 0)),
            pl.BlockSpec((None, D, BRANCH_WIDTH), lay3),
            pl.BlockSpec((None, BRANCH_WIDTH, D), lay3),
            pl.BlockSpec((1, D), lambda i, j: (0, 0)),
            pl.BlockSpec((1, D), lambda i, j: (0, 0)),
        ],
        out_specs=pl.BlockSpec((1, tm, D), lambda i, j: (i, j, 0)),
        out_shape=jax.ShapeDtypeStruct((B, S, D), F32),
        compiler_params=_params(2),
        name="xattn_sublayer",
    )(x, kv, wq, wo, g, b)


def _pad_cols(w, n):
    return jnp.pad(w, [(0, 0)] * (w.ndim - 1) + [(0, n - w.shape[-1])])


def _split_w_in(w_in, b_in):
    hd, bw = HEAD_DIM, BRANCH_WIDTH
    sizes = (bw, bw, bw, bw, hd, hd, hd, hd, hd, hd, 3 * N_HEADS, bw, bw, bw, N_HEADS,
             MLA_Q_RANK, MLA_KV_RANK, MLA_ROPE, N_BRANCH * D_MODEL)
    names = ("a_q", "a_k", "a_v", "b_q", "b_kc", "b_vc", "b_ks", "b_vs", "b_kw", "b_vw", "b_g",
             "c_q", "c_k", "c_v", "c_f", "d_cq", "d_ckv", "d_kr", "g_merge")
    w, b, off = {}, {}, 0
    for name, size in zip(names, sizes):
        w[name] = w_in[:, :, off:off + size]
        b[name] = b_in[:, off:off + size]
        off += size
    return w, b


def _row_major(ws, bs, names, pad_to):
    w = _pad_cols(jnp.concatenate([ws[n] for n in names], axis=-1), pad_to).astype(BF16)
    b = _pad_cols(jnp.concatenate([bs[n] for n in names], axis=-1), pad_to)[:, None, :]
    return w, b


def _channel_major(ws, bs, names):
    w = jnp.concatenate([ws[n] for n in names], axis=-1).transpose(0, 2, 1).astype(BF16)
    b = jnp.concatenate([bs[n] for n in names], axis=-1)[:, :, None]
    return w, b


def _rope_slab(a, swap):
    half = MLA_ROPE // 2
    first, second = a[..., :half], a[..., half:]
    body = jnp.concatenate([second, first] if swap else [first, second], axis=-1)
    return jnp.pad(body, [(0, 0)] * (a.ndim - 1) + [(MLA_NOPE, LANES - MLA_NOPE - MLA_ROPE)])


def _mla_weights(ws, bs, mla_w_uq, mla_w_ukv):
    L = mla_w_uq.shape[0]
    half = MLA_ROPE // 2
    w = jnp.concatenate([ws["d_cq"], ws["d_ckv"], _rope_slab(ws["d_kr"], False), _rope_slab(ws["d_kr"], True)],
                        axis=-1).astype(BF16)
    b = jnp.concatenate([bs["d_cq"], bs["d_ckv"], _rope_slab(bs["d_kr"], False), _rope_slab(bs["d_kr"], True)],
                        axis=-1)[:, None, :]
    uq = mla_w_uq.reshape(L, MLA_Q_RANK, N_HEADS, MLA_NOPE + MLA_ROPE).transpose(0, 2, 1, 3)
    wq = jnp.concatenate([uq[..., :MLA_NOPE], _rope_slab(uq[..., MLA_NOPE:], False)[..., MLA_NOPE:]],
                         axis=-1).astype(BF16)
    wqs = jnp.concatenate([jnp.zeros_like(uq[..., :MLA_NOPE]), _rope_slab(uq[..., MLA_NOPE:], True)[..., MLA_NOPE:]],
                          axis=-1).astype(BF16)
    ukv = mla_w_ukv.reshape(L, MLA_KV_RANK, N_HEADS, MLA_NOPE + HEAD_DIM).transpose(0, 2, 1, 3)
    wk = _pad_cols(ukv[..., :MLA_NOPE], QK_WIDTH).astype(BF16)
    wvt = ukv[..., MLA_NOPE:].transpose(0, 1, 3, 2).reshape(L, BRANCH_WIDTH, MLA_KV_RANK).astype(BF16)
    inv_freq = ROPE_BASE ** (-jnp.arange(0, MLA_ROPE, 2, dtype=F32) / MLA_ROPE)
    freq_row = _rope_slab(jnp.concatenate([inv_freq, inv_freq])[None, :], False)
    sign_row = _rope_slab(jnp.concatenate([-jnp.ones((half,), F32), jnp.ones((half,), F32)])[None, :], False)
    return w, b, wq, wqs, wk, wvt, freq_row, sign_row


def kernel(x, mem, positions, ln_g, ln_b, ffn1_w_in, ffn1_w_out, ffn2_w_in, ffn2_w_out, w_in, b_in, w_branch, w_o,
           mla_q_norm, mla_w_uq, mla_kv_norm, mla_w_ukv, nsa_cmp_pos, nsa_cmp_w1, nsa_cmp_w2, xa_w_q, xa_w_kv, xa_w_o):
    B, S, D = x.shape
    L = ln_g.shape[0]
    nc = D_FF // FFN_TILE_F

    def ffn_weights(w_i, w_out):
        wg = w_i[:, :, :D_FF].reshape(L, D, nc, FFN_TILE_F).transpose(0, 2, 1, 3).astype(BF16)
        wu = w_i[:, :, D_FF:].reshape(L, D, nc, FFN_TILE_F).transpose(0, 2, 1, 3).astype(BF16)
        return wg, wu, w_out.reshape(L, nc, FFN_TILE_F, D).astype(BF16)

    ffn1 = ffn_weights(ffn1_w_in, ffn1_w_out)
    ffn2 = ffn_weights(ffn2_w_in, ffn2_w_out)

    ws, bs = _split_w_in(w_in, b_in)
    moba_w, moba_b = _row_major(ws, bs, ("a_q", "a_k"), 2 * BRANCH_WIDTH)
    moba_wt, moba_bt = _channel_major(ws, bs, ("a_v",))
    nsa_w, nsa_b = _row_major(ws, bs, ("b_q", "b_ks", "b_kw", "b_kc", "b_vc", "b_g"), 5 * LANES)
    nsa_wt, nsa_bt = _channel_major(ws, bs, ("b_vs", "b_vw"))
    fox_w, fox_b = _row_major(ws, bs, ("c_q", "c_k", "c_f"), 5 * LANES)
    fox_wt, fox_bt = _channel_major(ws, bs, ("c_v",))

    mla_w, mla_b, mla_wq, mla_wqs, mla_wk, mla_wvt, freq_row, sign_row = _mla_weights(ws, bs, mla_w_uq, mla_w_ukv)
    cos, sin = _rope_tables(positions, freq_row, sign_row)

    merge_wg = ws["g_merge"].astype(BF16)
    merge_bg = bs["g_merge"][:, None, :]
    merge_wb = w_branch.astype(BF16)
    merge_wo = w_o.astype(BF16)
    nsa_w1 = nsa_cmp_w1.reshape(L, 2, NSA_CMP_LEN, HEAD_DIM, NSA_CMP_HIDDEN)
    xa_wq, xa_wkv, xa_wo = xa_w_q.astype(BF16), xa_w_kv.astype(BF16), xa_w_o.astype(BF16)

    for l in range(L):
        ln = lambda i: (ln_g[l, i][None, :], ln_b[l, i][None, :])
        x2 = _ffn_sublayer(x.reshape(B * S, D), *ffn1, *ln(0), l)
        x = x2.reshape(B, S, D)
        branches = (
            _moba(x, moba_w, moba_b, moba_wt, moba_bt, l),
            _nsa(x, nsa_w, nsa_b, nsa_wt, nsa_bt, nsa_cmp_pos, nsa_w1, nsa_cmp_w2, l),
            _fox(x, fox_w, fox_b, fox_wt, fox_bt, l),
            _mla(x, cos, sin, mla_w, mla_b, mla_q_norm[:, None, :], mla_kv_norm[:, None, :],
                 mla_wq, mla_wqs, mla_wk, mla_wvt, l),
        )
        x2 = _merge_sublayer(x2, [o.reshape(B * S, BRANCH_WIDTH) for o in branches],
                             merge_wg, merge_bg, merge_wb, merge_wo, *ln(1), l)
        x = _xattn_sublayer(x2.reshape(B, S, D), _mem_kv(mem, xa_wkv, l), xa_wq, xa_wo, *ln(2), l)
        x = _ffn_sublayer(x.reshape(B * S, D), *ffn2, *ln(3), l).reshape(B, S, D)
    return x
```

```python
import functools

import jax
import jax.numpy as jnp
import numpy as np
from jax import lax
from jax.experimental import pallas as pl
from jax.experimental.pallas import tpu as pltpu

F32 = jnp.float32
BF16 = jnp.bfloat16
HIGHEST = lax.Precision.HIGHEST

D_MODEL = 1024
DEPTH = 4
HEAD_DIM = 64
N_HEADS = 4
BRANCH_WIDTH = N_HEADS * HEAD_DIM
N_BRANCH = 4
MOBA_BLOCK = 256
MOBA_TOPK = 3
NSA_CMP_LEN = 32
NSA_CMP_STRIDE = 16
NSA_CMP_HIDDEN = 256
NSA_SLC_BLOCK = 64
NSA_TOP_N = 16
NSA_WINDOW = 512
NSA_FORCE_BONUS = 1.0e4
MLA_Q_RANK = 256
MLA_KV_RANK = 128
MLA_NOPE = 64
MLA_ROPE = 32
ROPE_BASE = 10000.0
MEM_LEN = 256
D_FF = 2816
ALPHA = (2 * DEPTH) ** 0.25
LN_EPS = 1e-5
RMS_EPS = 1e-6
NEG_INF = -1e30
MOBA_SLOPES = tuple(2.0 ** -(2 * h + 1) for h in range(N_HEADS))
NSA_SLOPES = tuple(2.0 ** -(2 * h + 2) for h in range(N_HEADS))

LOG2E = 1.4426950408889634
VT_ROWS = HEAD_DIM + 16
LANES = 128
TQ = 256
TK = 256
QK_WIDTH = 128
ROW_CHUNK = 512
FFN_TILE_F = 256
TOKEN_TILE = 512
XATTN_TILE = 512
XATTN_SUB_TILE = 512
VMEM_LIMIT = 56 * 1024 * 1024


def _nn(a, b):
    return jnp.dot(a, b, preferred_element_type=F32)


def _nt(a, b, precision=None):
    return lax.dot_general(a, b, (((1,), (1,)), ((), ())), precision=precision,
                           preferred_element_type=F32)


def _layer_norm(z, g, b):
    mu = jnp.mean(z, -1, keepdims=True)
    d = z - mu
    var = jnp.mean(d * d, -1, keepdims=True)
    return d * lax.rsqrt(var + LN_EPS) * g + b


def _params(n_parallel):
    return pltpu.CompilerParams(dimension_semantics=("parallel",) * n_parallel,
                                vmem_limit_bytes=VMEM_LIMIT)


def _ffn_kernel(x_ref, wi_ref, wo_ref, g_ref, b_ref, o_ref, *o16_ref):
    x = x_ref[...]
    xb = x.astype(BF16)
    acc = None
    for c in range(D_FF // FFN_TILE_F):
        lo = c * FFN_TILE_F
        hg = _nn(xb, wi_ref[:, lo:lo + FFN_TILE_F])
        hu = _nn(xb, wi_ref[:, D_FF + lo:D_FF + lo + FFN_TILE_F])
        a = (hg * jax.nn.sigmoid(hg) * hu).astype(BF16)
        part = _nn(a, wo_ref[lo:lo + FFN_TILE_F, :])
        acc = part if acc is None else acc + part
    out = _layer_norm(ALPHA * x + 0.5 * acc, g_ref[...], b_ref[...])
    o_ref[...] = out
    for ref in o16_ref:
        ref[...] = out.astype(BF16)


def _ffn_sublayer(x2, wi, wo, g, b, layer, with_bf16_copy):
    T, D = x2.shape
    tm = TOKEN_TILE
    out_spec = pl.BlockSpec((tm, D), lambda i: (i, 0))
    dtypes = (F32, BF16) if with_bf16_copy else (F32,)
    return pl.pallas_call(
        _ffn_kernel,
        grid=(T // tm,),
        in_specs=[
            pl.BlockSpec((tm, D), lambda i: (i, 0)),
            pl.BlockSpec((None, D, 2 * D_FF), lambda i: (layer, 0, 0)),
            pl.BlockSpec((None, D_FF, D), lambda i: (layer, 0, 0)),
            pl.BlockSpec((1, D), lambda i: (0, 0)),
            pl.BlockSpec((1, D), lambda i: (0, 0)),
        ],
        out_specs=[out_spec] * len(dtypes),
        out_shape=[jax.ShapeDtypeStruct((T, D), dt) for dt in dtypes],
        compiler_params=_params(1),
        name="ffn_sublayer",
    )(x2, wi, wo, g, b)


def _tile_iotas():
    r = lax.broadcasted_iota(jnp.int32, (TK, TQ), 0)
    c = lax.broadcasted_iota(jnp.int32, (TK, TQ), 1)
    return r, c


def _scores(s_ref, n_tiles, score_tile, h):
    m = None
    for j in range(n_tiles):
        s = score_tile(h, j)
        s_ref[j * TK:(j + 1) * TK, :] = s
        mj = jnp.max(s, axis=0, keepdims=True)
        m = mj if m is None else jnp.maximum(m, mj)
    return m


def _weighted_values(s_ref, n_tiles, m, vt_cols):
    p_all = jnp.concatenate([jnp.exp2(s_ref[j * TK:(j + 1) * TK, :] - m).astype(BF16) for j in range(n_tiles)],
                            axis=0)
    res = _nn(vt_cols, p_all)
    return res[:HEAD_DIM] / jnp.maximum(res[HEAD_DIM:HEAD_DIM + 1], 1e-30)


def _attend_heads(n_tiles, score_tile, vt_cols, out_ref, s_refs):
    s_a, s_b = s_refs
    m_first = _scores(s_a, n_tiles, score_tile, 0)

    def pair(i, m_a):
        h_a, h_b = 2 * i, 2 * i + 1
        h_next = jnp.minimum(2 * i + 2, N_HEADS - 1)
        m_b = _scores(s_b, n_tiles, score_tile, h_b)
        out_ref[h_a] = _weighted_values(s_a, n_tiles, m_a, vt_cols(h_a, n_tiles))
        m_next = _scores(s_a, n_tiles, score_tile, h_next)
        out_ref[h_b] = _weighted_values(s_b, n_tiles, m_b, vt_cols(h_b, n_tiles))
        return m_next

    lax.fori_loop(0, N_HEADS // 2, pair, m_first)


def _position_columns(n_rows, first, step):
    row = lax.broadcasted_iota(jnp.int32, (n_rows, HEAD_DIM), 0)
    lane = lax.broadcasted_iota(jnp.int32, (n_rows, HEAD_DIM), 1)
    pos = first + step * row
    hi = ((pos >> 7) << 7).astype(F32)
    lo = (pos & 127).astype(F32)
    return jnp.where(lane < 6, jnp.where((lane & 1) == 0, hi, lo), 0.0)


def _bf16_pieces(value):
    hi = float(np.asarray(value, np.float32).astype(BF16))
    mid = float(np.asarray(value - hi, np.float32).astype(BF16))
    return hi, mid, value - hi - mid


def _slope_columns(n_rows, slope):
    lane = lax.broadcasted_iota(jnp.int32, (n_rows, HEAD_DIM), 1)
    hi, mid, rest = _bf16_pieces(slope * LOG2E)
    return jnp.where(lane < 2, hi, jnp.where(lane < 4, mid, jnp.where(lane < 6, rest, 0.0))).astype(F32)


def _ones_columns(n_rows, n_cols):
    lane = lax.broadcasted_iota(jnp.int32, (n_rows, HEAD_DIM), 1)
    return jnp.where(lane < n_cols, 1.0, 0.0).astype(F32)


def _split3(a):
    hi = a.astype(BF16)
    rest = a - hi.astype(F32)
    mid = rest.astype(BF16)
    return hi, mid, (rest - mid.astype(F32)).astype(BF16)


def _project(x_ref, w_ref, b_ref, wt_ref, bt_ref, rm_ref, cm_ref):
    S = x_ref.shape[1]
    for i in range(S // ROW_CHUNK):
        rows = slice(i * ROW_CHUNK, (i + 1) * ROW_CHUNK)
        xb = x_ref[0, rows, :]
        rm_ref[rows, :] = _nn(xb, w_ref[...]) + b_ref[...]
        cm_ref[:, rows] = _nt(wt_ref[...], xb) + bt_ref[...]


def _stage_vt(vt_ref, v_t):
    G, _, S = vt_ref.shape
    vt_ref[:, 0:HEAD_DIM, :] = v_t.astype(BF16).reshape(G, HEAD_DIM, S)
    row = lax.broadcasted_iota(jnp.int32, (G, VT_ROWS - HEAD_DIM, S), 1)
    vt_ref[:, HEAD_DIM:VT_ROWS, :] = jnp.where(row == 0, 1.0, 0.0).astype(BF16)


def _causal_attention(o_ref, qp_ref, kp_ref, vt_ref, ot_ref, s_refs):
    S = qp_ref.shape[1]
    r, c = _tile_iotas()
    for qi in range(S // TQ):
        q0 = qi * TQ

        def score_tile(h, j, qi=qi, q0=q0):
            s = _nt(kp_ref[h, j * TK:(j + 1) * TK, :], qp_ref[h, q0:q0 + TQ, :])
            return jnp.where(r <= c, s, NEG_INF) if j == qi else s

        _attend_heads(qi + 1, score_tile, lambda h, n: vt_ref[h, :, 0:n * TK], ot_ref, s_refs)
        _store_heads(o_ref, q0, ot_ref[...].reshape(BRANCH_WIDTH, TQ))


def _store_heads(o_ref, q0, heads_t):
    o_ref[0, q0:q0 + TQ, :] = heads_t.T


def _attention_scratch(S):
    return [pltpu.VMEM((N_HEADS, HEAD_DIM, TQ), F32), pltpu.VMEM((S, TQ), F32), pltpu.VMEM((S, TQ), F32)]


def _moba_kernel(x_ref, w_ref, b_ref, wt_ref, bt_ref, o_ref, rm_ref, cm_ref, qp_ref, kp_ref, vt_ref, sel_ref,
                 ot_ref, sa_ref, sb_ref):
    S = x_ref.shape[1]
    nb = S // MOBA_BLOCK
    _project(x_ref, w_ref, b_ref, wt_ref, bt_ref, rm_ref, cm_ref)
    _stage_vt(vt_ref, cm_ref[...])
    kcols = _position_columns(S, 0, 1)
    blk = lax.broadcasted_iota(jnp.int32, (nb, S), 0)
    own = lax.broadcasted_iota(jnp.int32, (nb, S), 1) // MOBA_BLOCK
    cand = blk < own
    for h in range(N_HEADS):
        q = rm_ref[:, h * HEAD_DIM:(h + 1) * HEAD_DIM]
        k = rm_ref[:, BRANCH_WIDTH + h * HEAD_DIM:BRANCH_WIDTH + (h + 1) * HEAD_DIM]
        qp_ref[h] = jnp.concatenate(
            [q * (HEAD_DIM ** -0.5 * LOG2E), _slope_columns(S, MOBA_SLOPES[h])], axis=1).astype(BF16)
        kp_ref[h] = jnp.concatenate([k, kcols], axis=1).astype(BF16)
        k_mean = jnp.sum(k.reshape(nb, MOBA_BLOCK, HEAD_DIM), axis=1) * (1.0 / MOBA_BLOCK)
        gate = jnp.where(cand, _nt(k_mean, q, precision=HIGHEST), NEG_INF)
        beaten = jnp.zeros((nb, S), F32)
        for j in range(nb):
            other = gate[j:j + 1, :]
            wins = (other > gate) | ((other == gate) & (blk > j))
            beaten = beaten + jnp.where(wins, 1.0, 0.0)
        sel = jnp.where(cand & (beaten < MOBA_TOPK), 1.0, 0.0)
        for qi in range(S // TQ):
            sel_ref[h, qi] = sel[:, qi * TQ:(qi + 1) * TQ]

    r, c = _tile_iotas()
    for qi in range(S // TQ):
        q0 = qi * TQ

        def score_tile(h, j, qi=qi, q0=q0):
            s = _nt(kp_ref[h, j * TK:(j + 1) * TK, :], qp_ref[h, q0:q0 + TQ, :])
            return jnp.where(r <= c if j == qi else sel_ref[h, qi, j:j + 1, :] > 0.5, s, NEG_INF)

        _attend_heads(qi + 1, score_tile, lambda h, n: vt_ref[h, :, 0:n * TK], ot_ref, (sa_ref, sb_ref))
        _store_heads(o_ref, q0, ot_ref[...].reshape(BRANCH_WIDTH, TQ))


def _moba(x, w, b, wt, bt, layer):
    B, S, D = x.shape
    n_rm = w.shape[-1]
    n_cm = wt.shape[1]
    nb = S // MOBA_BLOCK
    return pl.pallas_call(
        _moba_kernel,
        grid=(B,),
        in_specs=[
            pl.BlockSpec((1, S, D), lambda i: (i, 0, 0)),
            pl.BlockSpec((None, D, n_rm), lambda i: (layer, 0, 0)),
            pl.BlockSpec((None, 1, n_rm), lambda i: (layer, 0, 0)),
            pl.BlockSpec((None, n_cm, D), lambda i: (layer, 0, 0)),
            pl.BlockSpec((None, n_cm, 1), lambda i: (layer, 0, 0)),
        ],
        out_specs=pl.BlockSpec((1, S, BRANCH_WIDTH), lambda i: (i, 0, 0)),
        out_shape=jax.ShapeDtypeStruct((B, S, BRANCH_WIDTH), F32),
        scratch_shapes=[
            pltpu.VMEM((S, n_rm), F32),
            pltpu.VMEM((n_cm, S), F32),
            pltpu.VMEM((N_HEADS, S, QK_WIDTH), BF16),
            pltpu.VMEM((N_HEADS, S, QK_WIDTH), BF16),
            pltpu.VMEM((N_HEADS, VT_ROWS, S), BF16),
            pltpu.VMEM((N_HEADS, S // TQ, nb, TQ), F32),
        ] + _attention_scratch(S),
        compiler_params=_params(1),
        name="moba_mixer",
    )(x, w, b, wt, bt)


def _fox_kernel(x_ref, w_ref, b_ref, wt_ref, bt_ref, o_ref, rm_ref, cm_ref, qp_ref, kp_ref, vt_ref,
                ot_ref, sa_ref, sb_ref):
    S = x_ref.shape[1]
    _project(x_ref, w_ref, b_ref, wt_ref, bt_ref, rm_ref, cm_ref)
    _stage_vt(vt_ref, cm_ref[...])
    f = rm_ref[:, 2 * BRANCH_WIDTH:2 * BRANCH_WIDTH + LANES]
    log_sig = jnp.minimum(f, 0.0) - jnp.log1p(jnp.exp(-jnp.abs(f)))
    blk = MOBA_BLOCK
    tri = jnp.where(lax.broadcasted_iota(jnp.int32, (blk, blk), 0) >= lax.broadcasted_iota(jnp.int32, (blk, blk), 1),
                    1.0, 0.0).astype(BF16)
    carry = jnp.zeros((1, LANES), F32)
    pieces = []
    for i in range(S // blk):
        hi, mid, lo = _split3(log_sig[i * blk:(i + 1) * blk, :])
        cs = _nn(tri, hi) + _nn(tri, mid) + _nn(tri, lo) + carry
        pieces.append(cs)
        carry = cs[blk - 1:blk, :]
    key_bias = jnp.concatenate(pieces, axis=0) * (-LOG2E)
    lane = lax.broadcasted_iota(jnp.int32, (S, HEAD_DIM), 1)
    for h in range(N_HEADS):
        q = rm_ref[:, h * HEAD_DIM:(h + 1) * HEAD_DIM]
        k = rm_ref[:, BRANCH_WIDTH + h * HEAD_DIM:BRANCH_WIDTH + (h + 1) * HEAD_DIM]
        d = key_bias[:, h:h + 1]
        d_hi = d.astype(BF16).astype(F32)
        d_mid = (d - d_hi).astype(BF16).astype(F32)
        d_lo = d - d_hi - d_mid
        dcols = jnp.where(lane == 0, d_hi, jnp.where(lane == 1, d_mid, jnp.where(lane == 2, d_lo, 0.0)))
        qp_ref[h] = jnp.concatenate([q * (HEAD_DIM ** -0.5 * LOG2E), _ones_columns(S, 3)], axis=1).astype(BF16)
        kp_ref[h] = jnp.concatenate([k, dcols], axis=1).astype(BF16)

    _causal_attention(o_ref, qp_ref, kp_ref, vt_ref, ot_ref, (sa_ref, sb_ref))


def _fox(x, w, b, wt, bt, layer):
    B, S, D = x.shape
    n_rm = w.shape[-1]
    n_cm = wt.shape[1]
    return pl.pallas_call(
        _fox_kernel,
        grid=(B,),
        in_specs=[
            pl.BlockSpec((1, S, D), lambda i: (i, 0, 0)),
            pl.BlockSpec((None, D, n_rm), lambda i: (layer, 0, 0)),
            pl.BlockSpec((None, 1, n_rm), lambda i: (layer, 0, 0)),
            pl.BlockSpec((None, n_cm, D), lambda i: (layer, 0, 0)),
            pl.BlockSpec((None, n_cm, 1), lambda i: (layer, 0, 0)),
        ],
        out_specs=pl.BlockSpec((1, S, BRANCH_WIDTH), lambda i: (i, 0, 0)),
        out_shape=jax.ShapeDtypeStruct((B, S, BRANCH_WIDTH), F32),
        scratch_shapes=[
            pltpu.VMEM((S, n_rm), F32),
            pltpu.VMEM((n_cm, S), F32),
            pltpu.VMEM((N_HEADS, S, QK_WIDTH), BF16),
            pltpu.VMEM((N_HEADS, S, QK_WIDTH), BF16),
            pltpu.VMEM((N_HEADS, VT_ROWS, S), BF16),
        ] + _attention_scratch(S),
        compiler_params=_params(1),
        name="fox_mixer",
    )(x, w, b, wt, bt)


def _rope_table_kernel(pos_ref, freq_ref, sign_ref, cos_ref, sin_ref):
    ang = pos_ref[0].astype(F32) * freq_ref[...]
    cos_ref[0] = jnp.cos(ang)
    sin_ref[0] = jnp.sin(ang) * sign_ref[...]


def _rope_tables(positions, freq_row, sign_row):
    B, S = positions.shape
    return pl.pallas_call(
        _rope_table_kernel,
        grid=(B,),
        in_specs=[
            pl.BlockSpec((1, S, 1), lambda i: (i, 0, 0)),
            pl.BlockSpec((1, LANES), lambda i: (0, 0)),
            pl.BlockSpec((1, LANES), lambda i: (0, 0)),
        ],
        out_specs=[pl.BlockSpec((1, S, LANES), lambda i: (i, 0, 0))] * 2,
        out_shape=[jax.ShapeDtypeStruct((B, S, LANES), F32)] * 2,
        compiler_params=_params(1),
        name="rope_tables",
    )(positions.reshape(B, S, 1), freq_row, sign_row)


def _rms_norm(x, g):
    return x * lax.rsqrt(jnp.mean(x * x, -1, keepdims=True) + RMS_EPS) * g


def _mla_kernel(x_ref, cos_ref, sin_ref, w_ref, b_ref, qn_ref, kvn_ref, wq_ref, wqs_ref, wk_ref, wvt_ref,
                o_ref, rm_ref, qp_ref, kp_ref, vt_ref, ot_ref, sa_ref, sb_ref):
    S = x_ref.shape[1]
    for i in range(S // ROW_CHUNK):
        rows = slice(i * ROW_CHUNK, (i + 1) * ROW_CHUNK)
        rm_ref[rows, :] = _nn(x_ref[0, rows, :], w_ref[...]) + b_ref[...]
    cos = cos_ref[0]
    sin = sin_ref[0]
    scale = (MLA_NOPE + MLA_ROPE) ** -0.5 * LOG2E
    q_lat = _rms_norm(rm_ref[:, 0:MLA_Q_RANK], qn_ref[...]).astype(BF16)
    kv_lat = _rms_norm(rm_ref[:, MLA_Q_RANK:MLA_Q_RANK + MLA_KV_RANK], kvn_ref[...]).astype(BF16)
    c0 = MLA_Q_RANK + MLA_KV_RANK
    k_rope = rm_ref[:, c0:c0 + LANES] * cos + rm_ref[:, c0 + LANES:c0 + 2 * LANES] * sin
    _stage_vt(vt_ref, _nt(wvt_ref[...], kv_lat))
    for h in range(N_HEADS):
        qh = _nn(q_lat, wq_ref[h]) * cos + _nn(q_lat, wqs_ref[h]) * sin
        qp_ref[h] = (qh * scale).astype(BF16)
        kp_ref[h] = (_nn(kv_lat, wk_ref[h]) + k_rope).astype(BF16)

    _causal_attention(o_ref, qp_ref, kp_ref, vt_ref, ot_ref, (sa_ref, sb_ref))


def _mla(x, cos, sin, w, b, qn, kvn, wq, wqs, wk, wvt, layer):
    B, S, D = x.shape
    n_rm = w.shape[-1]
    lay3 = lambda i: (layer, 0, 0)
    lay4 = lambda i: (layer, 0, 0, 0)
    return pl.pallas_call(
        _mla_kernel,
        grid=(B,),
        in_specs=[
            pl.BlockSpec((1, S, D), lambda i: (i, 0, 0)),
            pl.BlockSpec((1, S, LANES), lambda i: (i, 0, 0)),
            pl.BlockSpec((1, S, LANES), lambda i: (i, 0, 0)),
            pl.BlockSpec((None, D, n_rm), lay3),
            pl.BlockSpec((None, 1, n_rm), lay3),
            pl.BlockSpec((None, 1, MLA_Q_RANK), lay3),
            pl.BlockSpec((None, 1, MLA_KV_RANK), lay3),
            pl.BlockSpec((None, N_HEADS, MLA_Q_RANK, QK_WIDTH), lay4),
            pl.BlockSpec((None, N_HEADS, MLA_Q_RANK, QK_WIDTH), lay4),
            pl.BlockSpec((None, N_HEADS, MLA_KV_RANK, QK_WIDTH), lay4),
            pl.BlockSpec((None, BRANCH_WIDTH, MLA_KV_RANK), lay3),
        ],
        out_specs=pl.BlockSpec((1, S, BRANCH_WIDTH), lambda i: (i, 0, 0)),
        out_shape=jax.ShapeDtypeStruct((B, S, BRANCH_WIDTH), F32),
        scratch_shapes=[
            pltpu.VMEM((S, n_rm), F32),
            pltpu.VMEM((N_HEADS, S, QK_WIDTH), BF16),
            pltpu.VMEM((N_HEADS, S, QK_WIDTH), BF16),
            pltpu.VMEM((N_HEADS, VT_ROWS, S), BF16),
        ] + _attention_scratch(S),
        compiler_params=_params(1),
        name="mla_mixer",
    )(x, cos, sin, w, b, qn, kvn, wq, wqs, wk, wvt)


N_CMP_PAD = 128


def _nsa_compress(src_ref, pos_ref, w1_ref, w2_ref):
    accs = [jnp.zeros((N_CMP_PAD, NSA_CMP_HIDDEN), F32) for _ in range(2)]
    n_cmp = N_CMP_PAD - 1
    for p in range(NSA_CMP_LEN):
        rows = src_ref[pl.ds(p, n_cmp, stride=NSA_CMP_STRIDE), :]
        for which in range(2):
            part = rows[:, which * HEAD_DIM:(which + 1) * HEAD_DIM] + pos_ref[which, p:p + 1, :]
            part = jnp.concatenate([part, jnp.zeros((1, HEAD_DIM), F32)], axis=0)
            accs[which] = accs[which] + _nn(part.astype(BF16), w1_ref[which, p])
    return [_nn(jax.nn.gelu(accs[which]).astype(BF16), w2_ref[which]) for which in range(2)]


def _nsa_kernel(x_ref, w_ref, b_ref, wt_ref, bt_ref, pos_ref, w1_ref, w2_ref, o_ref,
                rm_ref, cm_ref, cin_ref, qp_ref, ks_ref, kw_ref, kc_ref, vt_ref, vtc_ref, sel_ref,
                oslc_ref, owin_ref, sa_ref, sb_ref):
    S = x_ref.shape[1]
    n_slc = S // NSA_SLC_BLOCK
    _project(x_ref, w_ref, b_ref, wt_ref, bt_ref, rm_ref, cm_ref)
    _stage_vt(vt_ref, cm_ref[...])
    c_ks, c_kw, c_kc, c_vc, c_g = (BRANCH_WIDTH + i * HEAD_DIM for i in range(5))
    kcols = _position_columns(S, 0, 1)
    for h in range(N_HEADS):
        q = rm_ref[:, h * HEAD_DIM:(h + 1) * HEAD_DIM]
        qp_ref[h] = jnp.concatenate(
            [q * (HEAD_DIM ** -0.5 * LOG2E), _slope_columns(S, NSA_SLOPES[h])], axis=1).astype(BF16)
    ks_ref[...] = jnp.concatenate([rm_ref[:, c_ks:c_ks + HEAD_DIM], kcols], axis=1).astype(BF16)
    kw_ref[...] = jnp.concatenate([rm_ref[:, c_kw:c_kw + HEAD_DIM], kcols], axis=1).astype(BF16)
    cin_ref[...] = rm_ref[:, c_kc:c_kc + 2 * HEAD_DIM]
    k_cmp, v_cmp = _nsa_compress(cin_ref, pos_ref, w1_ref, w2_ref)
    kc_ref[...] = jnp.concatenate(
        [k_cmp, _position_columns(N_CMP_PAD, NSA_CMP_LEN - 1, NSA_CMP_STRIDE)], axis=1).astype(BF16)
    vtc_ref[...] = jnp.concatenate([k_cmp, v_cmp], axis=1).T[HEAD_DIM:, :].astype(BF16)

    jj = lax.broadcasted_iota(jnp.int32, (n_slc, N_CMP_PAD), 0) * NSA_SLC_BLOCK
    cc = lax.broadcasted_iota(jnp.int32, (n_slc, N_CMP_PAD), 1) * NSA_CMP_STRIDE
    overlap_t = jnp.where((cc < jj + NSA_SLC_BLOCK) & (cc + NSA_CMP_LEN > jj), 1.0, 0.0).astype(BF16)

    r, c = _tile_iotas()
    per_tile = TK // NSA_SLC_BLOCK
    win_tiles = NSA_WINDOW // TK
    c_idx = lax.broadcasted_iota(jnp.int32, (N_CMP_PAD, TQ), 0)
    lane_cmp = lax.broadcasted_iota(jnp.int32, (N_CMP_PAD, TQ), 1)
    for qi in range(S // TQ):
        q0 = qi * TQ
        cmp_ok = (c_idx * NSA_CMP_STRIDE + (NSA_CMP_LEN - 1) <= q0 + lane_cmp) & (c_idx < N_CMP_PAD - 1)
        imp = jnp.zeros((n_slc, TQ), F32)
        o_cmp = []
        for h in range(N_HEADS):
            s = jnp.where(cmp_ok, _nt(kc_ref[...], qp_ref[h, q0:q0 + TQ, :]), NEG_INF)
            e = jnp.where(cmp_ok, jnp.exp2(s - jnp.max(s, axis=0, keepdims=True)), 0.0)
            p = e / jnp.maximum(jnp.sum(e, axis=0, keepdims=True), 1e-30)
            p_hi, p_mid, p_lo = _split3(p)
            o_cmp.append(_nn(vtc_ref[...], p_hi))
            imp = imp + _nn(overlap_t, p_hi) + _nn(overlap_t, p_mid) + _nn(overlap_t, p_lo)
        n_seen = per_tile * (qi + 1)
        select = n_seen > NSA_TOP_N
        if select:
            j_idx = lax.broadcasted_iota(jnp.int32, (n_seen, TQ), 0)
            own = (q0 + lax.broadcasted_iota(jnp.int32, (n_seen, TQ), 1)) // NSA_SLC_BLOCK
            forced = (j_idx == 0) | (j_idx == own) | (j_idx == own - 1)
            score = jnp.where(j_idx <= own, imp[:n_seen] + NSA_FORCE_BONUS * jnp.where(forced, 1.0, 0.0), NEG_INF)
            beaten = jnp.zeros((n_seen, TQ), F32)
            for j in range(n_seen):
                other = score[j:j + 1, :]
                wins = (other > score) | ((other == score) & (j_idx > j))
                beaten = beaten + jnp.where(wins, 1.0, 0.0)
            sel_ref[0:n_seen, :] = jnp.where(beaten < NSA_TOP_N, 1.0, 0.0)

        def slc_tile(h, j, qi=qi, q0=q0, select=select):
            s = _nt(ks_ref[j * TK:(j + 1) * TK, :], qp_ref[h, q0:q0 + TQ, :])
            mask = r <= c if j == qi else None
            if select:
                rows = [jnp.broadcast_to(sel_ref[per_tile * j + u:per_tile * j + u + 1, :], (NSA_SLC_BLOCK, TQ))
                        for u in range(per_tile)]
                chosen = jnp.concatenate(rows, axis=0) > 0.5
                mask = chosen if mask is None else mask & chosen
            return s if mask is None else jnp.where(mask, s, NEG_INF)

        _attend_heads(qi + 1, slc_tile, lambda h, n: vt_ref[0, :, 0:n * TK], oslc_ref, (sa_ref, sb_ref))

        lo = max(qi - win_tiles, 0)

        def win_tile(h, j, qi=qi, q0=q0, lo=lo):
            t = lo + j
            s = _nt(kw_ref[t * TK:(t + 1) * TK, :], qp_ref[h, q0:q0 + TQ, :])
            if t == qi:
                return jnp.where(r <= c, s, NEG_INF)
            return jnp.where(c < r, s, NEG_INF) if t == qi - win_tiles else s

        _attend_heads(qi - lo + 1, win_tile, lambda h, n, lo=lo: vt_ref[1, :, lo * TK:(lo + n) * TK], owin_ref,
                      (sa_ref, sb_ref))

        gates = jax.nn.sigmoid(rm_ref[q0:q0 + TQ, c_g:c_g + LANES]).T
        outs = [gates[3 * h:3 * h + 1, :] * o_cmp[h] + gates[3 * h + 1:3 * h + 2, :] * oslc_ref[h]
                + gates[3 * h + 2:3 * h + 3, :] * owin_ref[h] for h in range(N_HEADS)]
        _store_heads(o_ref, q0, jnp.concatenate(outs, axis=0))


def _nsa(x, w, b, wt, bt, pos, w1, w2, layer):
    B, S, D = x.shape
    n_rm = w.shape[-1]
    n_cm = wt.shape[1]
    lay3 = lambda i: (layer, 0, 0)
    lay4 = lambda i: (layer, 0, 0, 0)
    lay5 = lambda i: (layer, 0, 0, 0, 0)
    return pl.pallas_call(
        _nsa_kernel,
        grid=(B,),
        in_specs=[
            pl.BlockSpec((1, S, D), lambda i: (i, 0, 0)),
            pl.BlockSpec((None, D, n_rm), lay3),
            pl.BlockSpec((None, 1, n_rm), lay3),
            pl.BlockSpec((None, n_cm, D), lay3),
            pl.BlockSpec((None, n_cm, 1), lay3),
            pl.BlockSpec((None, 2, NSA_CMP_LEN, HEAD_DIM), lay4),
            pl.BlockSpec((None, 2, NSA_CMP_LEN, HEAD_DIM, NSA_CMP_HIDDEN), lay5),
            pl.BlockSpec((None, 2, NSA_CMP_HIDDEN, HEAD_DIM), lay4),
        ],
        out_specs=pl.BlockSpec((1, S, BRANCH_WIDTH), lambda i: (i, 0, 0)),
        out_shape=jax.ShapeDtypeStruct((B, S, BRANCH_WIDTH), F32),
        scratch_shapes=[
            pltpu.VMEM((S, n_rm), F32),
            pltpu.VMEM((n_cm, S), F32),
            pltpu.VMEM((S, 2 * HEAD_DIM), F32),
            pltpu.VMEM((N_HEADS, S, QK_WIDTH), BF16),
            pltpu.VMEM((S, QK_WIDTH), BF16),
            pltpu.VMEM((S, QK_WIDTH), BF16),
            pltpu.VMEM((N_CMP_PAD, QK_WIDTH), BF16),
            pltpu.VMEM((2, VT_ROWS, S), BF16),
            pltpu.VMEM((HEAD_DIM, N_CMP_PAD), BF16),
            pltpu.VMEM((S // NSA_SLC_BLOCK, TQ), F32),
            pltpu.VMEM((N_HEADS, HEAD_DIM, TQ), F32),
        ] + _attention_scratch(S),
        compiler_params=_params(1),
        name="nsa_mixer",
    )(x, w, b, wt, bt, pos, w1, w2)


def _merge_kernel(x_ref, oa_ref, ob_ref, oc_ref, od_ref, wg_ref, bg_ref, wb_ref, wo_ref, g_ref, b_ref, o_ref):
    x = x_ref[...]
    xb = x.astype(BF16)
    D = x.shape[-1]
    acc = jnp.zeros(x.shape, F32)
    for n, branch in enumerate((oa_ref, ob_ref, oc_ref, od_ref)):
        gate = jax.nn.sigmoid(_nn(xb, wg_ref[:, n * D:(n + 1) * D]) + bg_ref[:, n * D:(n + 1) * D])
        acc = acc + gate * _nn(branch[...].astype(BF16), wb_ref[n])
    mix = _nn(acc.astype(BF16), wo_ref[...])
    o_ref[...] = _layer_norm(ALPHA * x + mix, g_ref[...], b_ref[...])


def _merge_sublayer(x2, branches, wg, bg, wb, wo, g, b, layer):
    T, D = x2.shape
    tm = TOKEN_TILE
    lay3 = lambda i: (layer, 0, 0)
    return pl.pallas_call(
        _merge_kernel,
        grid=(T // tm,),
        in_specs=[pl.BlockSpec((tm, D), lambda i: (i, 0))]
        + [pl.BlockSpec((tm, BRANCH_WIDTH), lambda i: (i, 0))] * N_BRANCH
        + [
            pl.BlockSpec((None, D, N_BRANCH * D), lay3),
            pl.BlockSpec((None, 1, N_BRANCH * D), lay3),
            pl.BlockSpec((None, N_BRANCH, BRANCH_WIDTH, D), lambda i: (layer, 0, 0, 0)),
            pl.BlockSpec((None, D, D), lay3),
            pl.BlockSpec((1, D), lambda i: (0, 0)),
            pl.BlockSpec((1, D), lambda i: (0, 0)),
        ],
        out_specs=pl.BlockSpec((tm, D), lambda i: (i, 0)),
        out_shape=jax.ShapeDtypeStruct((T, D), F32),
        compiler_params=_params(1),
        name="merge_sublayer",
    )(x2, *branches, wg, bg, wb, wo, g, b)


def _mem_kv_kernel(mem_ref, wk_ref, wvt_ref, kp_ref, vt_ref):
    mem = mem_ref[0].astype(BF16)
    k = _nn(mem, wk_ref[...])
    zeros = jnp.zeros((k.shape[0], QK_WIDTH - HEAD_DIM), F32)
    for h in range(N_HEADS):
        kp_ref[0, h] = jnp.concatenate([k[:, h * HEAD_DIM:(h + 1) * HEAD_DIM], zeros], axis=1).astype(BF16)
    _stage_vt(vt_ref.at[0], _nt(wvt_ref[...], mem))


def _mem_kv(mem, w_k, w_vt, layer):
    B, M, D = mem.shape
    lay3 = lambda i: (layer, 0, 0)
    return pl.pallas_call(
        _mem_kv_kernel,
        grid=(B,),
        in_specs=[pl.BlockSpec((1, M, D), lambda i: (i, 0, 0)),
                  pl.BlockSpec((None, D, BRANCH_WIDTH), lay3),
                  pl.BlockSpec((None, BRANCH_WIDTH, D), lay3)],
        out_specs=[pl.BlockSpec((1, N_HEADS, M, QK_WIDTH), lambda i: (i, 0, 0, 0)),
                   pl.BlockSpec((1, N_HEADS, VT_ROWS, M), lambda i: (i, 0, 0, 0))],
        out_shape=[jax.ShapeDtypeStruct((B, N_HEADS, M, QK_WIDTH), BF16),
                   jax.ShapeDtypeStruct((B, N_HEADS, VT_ROWS, M), BF16)],
        compiler_params=_params(1),
        name="mem_kv",
    )(mem, w_k, w_vt)


def _xattn_kernel(x_ref, kp_ref, vt_ref, wq_ref, wo_ref, g_ref, b_ref, o_ref):
    zeros = jnp.zeros((XATTN_SUB_TILE, QK_WIDTH - HEAD_DIM), F32)
    outs = []
    for i in range(x_ref.shape[1] // XATTN_SUB_TILE):
        x = x_ref[0, i * XATTN_SUB_TILE:(i + 1) * XATTN_SUB_TILE, :]
        q = _nn(x.astype(BF16), wq_ref[...]) * (HEAD_DIM ** -0.5 * LOG2E)
        heads = []
        for h in range(N_HEADS):
            qp = jnp.concatenate([q[:, h * HEAD_DIM:(h + 1) * HEAD_DIM], zeros], axis=1).astype(BF16)
            s = _nt(kp_ref[0, h], qp)
            p = jnp.exp2(s - jnp.max(s, axis=0, keepdims=True)).astype(BF16)
            res = _nn(vt_ref[0, h], p)
            heads.append(res[:HEAD_DIM] / res[HEAD_DIM:HEAD_DIM + 1])
        att = jnp.concatenate(heads, axis=0).T.astype(BF16)
        outs.append(_layer_norm(ALPHA * x + _nn(att, wo_ref[...]), g_ref[...], b_ref[...]))
    o_ref[0] = jnp.concatenate(outs, axis=0)


def _xattn_sublayer(x, kp, vt, wq, wo, g, b, layer):
    B, S, D = x.shape
    M = kp.shape[2]
    tm = XATTN_TILE
    lay3 = lambda i, j: (layer, 0, 0)
    return pl.pallas_call(
        _xattn_kernel,
        grid=(B, S // tm),
        in_specs=[
            pl.BlockSpec((1, tm, D), lambda i, j: (i, j, 0)),
            pl.BlockSpec((1, N_HEADS, M, QK_WIDTH), lambda i, j: (i, 0, 0, 0)),
            pl.BlockSpec((1, N_HEADS, VT_ROWS, M), lambda i, j: (i, 0, 0, 0)),
            pl.BlockSpec((None, D, BRANCH_WIDTH), lay3),
            pl.BlockSpec((None, BRANCH_WIDTH, D), lay3),
            pl.BlockSpec((1, D), lambda i, j: (0, 0)),
            pl.BlockSpec((1, D), lambda i, j: (0, 0)),
        ],
        out_specs=pl.BlockSpec((1, tm, D), lambda i, j: (i, j, 0)),
        out_shape=jax.ShapeDtypeStruct((B, S, D), F32),
        compiler_params=_params(2),
        name="xattn_sublayer",
    )(x, kp, vt, wq, wo, g, b)


def _pad_cols(w, n):
    return jnp.pad(w, [(0, 0)] * (w.ndim - 1) + [(0, n - w.shape[-1])])


def _split_w_in(w_in, b_in):
    hd, bw = HEAD_DIM, BRANCH_WIDTH
    sizes = (bw, bw, bw, bw, hd, hd, hd, hd, hd, hd, 3 * N_HEADS, bw, bw, bw, N_HEADS,
             MLA_Q_RANK, MLA_KV_RANK, MLA_ROPE, N_BRANCH * D_MODEL)
    names = ("a_q", "a_k", "a_v", "b_q", "b_kc", "b_vc", "b_ks", "b_vs", "b_kw", "b_vw", "b_g",
             "c_q", "c_k", "c_v", "c_f", "d_cq", "d_ckv", "d_kr", "g_merge")
    w, b, off = {}, {}, 0
    for name, size in zip(names, sizes):
        w[name] = w_in[:, :, off:off + size]
        b[name] = b_in[:, off:off + size]
        off += size
    return w, b


def _row_major(ws, bs, names, pad_to):
    w = _pad_cols(jnp.concatenate([ws[n] for n in names], axis=-1), pad_to).astype(BF16)
    b = _pad_cols(jnp.concatenate([bs[n] for n in names], axis=-1), pad_to)[:, None, :]
    return w, b


def _channel_major(ws, bs, names):
    w = jnp.concatenate([ws[n] for n in names], axis=-1).transpose(0, 2, 1).astype(BF16)
    b = jnp.concatenate([bs[n] for n in names], axis=-1)[:, :, None]
    return w, b


def _rope_slab(a, swap):
    half = MLA_ROPE // 2
    first, second = a[..., :half], a[..., half:]
    body = jnp.concatenate([second, first] if swap else [first, second], axis=-1)
    return jnp.pad(body, [(0, 0)] * (a.ndim - 1) + [(MLA_NOPE, LANES - MLA_NOPE - MLA_ROPE)])


def _mla_weights(ws, bs, mla_w_uq, mla_w_ukv):
    L = mla_w_uq.shape[0]
    half = MLA_ROPE // 2
    w = jnp.concatenate([ws["d_cq"], ws["d_ckv"], _rope_slab(ws["d_kr"], False), _rope_slab(ws["d_kr"], True)],
                        axis=-1).astype(BF16)
    b = jnp.concatenate([bs["d_cq"], bs["d_ckv"], _rope_slab(bs["d_kr"], False), _rope_slab(bs["d_kr"], True)],
                        axis=-1)[:, None, :]
    uq = mla_w_uq.reshape(L, MLA_Q_RANK, N_HEADS, MLA_NOPE + MLA_ROPE).transpose(0, 2, 1, 3)
    wq = jnp.concatenate([uq[..., :MLA_NOPE], _rope_slab(uq[..., MLA_NOPE:], False)[..., MLA_NOPE:]],
                         axis=-1).astype(BF16)
    wqs = jnp.concatenate([jnp.zeros_like(uq[..., :MLA_NOPE]), _rope_slab(uq[..., MLA_NOPE:], True)[..., MLA_NOPE:]],
                          axis=-1).astype(BF16)
    ukv = mla_w_ukv.reshape(L, MLA_KV_RANK, N_HEADS, MLA_NOPE + HEAD_DIM).transpose(0, 2, 1, 3)
    wk = _pad_cols(ukv[..., :MLA_NOPE], QK_WIDTH).astype(BF16)
    wvt = ukv[..., MLA_NOPE:].transpose(0, 1, 3, 2).reshape(L, BRANCH_WIDTH, MLA_KV_RANK).astype(BF16)
    inv_freq = ROPE_BASE ** (-jnp.arange(0, MLA_ROPE, 2, dtype=F32) / MLA_ROPE)
    freq_row = _rope_slab(jnp.concatenate([inv_freq, inv_freq])[None, :], False)
    sign_row = _rope_slab(jnp.concatenate([-jnp.ones((half,), F32), jnp.ones((half,), F32)])[None, :], False)
    return w, b, wq, wqs, wk, wvt, freq_row, sign_row


def kernel(x, mem, positions, ln_g, ln_b, ffn1_w_in, ffn1_w_out, ffn2_w_in, ffn2_w_out, w_in, b_in, w_branch, w_o,
           mla_q_norm, mla_w_uq, mla_kv_norm, mla_w_ukv, nsa_cmp_pos, nsa_cmp_w1, nsa_cmp_w2, xa_w_q, xa_w_kv, xa_w_o):
    B, S, D = x.shape
    L = ln_g.shape[0]
    ffn1 = (ffn1_w_in.astype(BF16), ffn1_w_out.astype(BF16))
    ffn2 = (ffn2_w_in.astype(BF16), ffn2_w_out.astype(BF16))

    ws, bs = _split_w_in(w_in, b_in)
    moba_w, moba_b = _row_major(ws, bs, ("a_q", "a_k"), 2 * BRANCH_WIDTH)
    moba_wt, moba_bt = _channel_major(ws, bs, ("a_v",))
    nsa_w, nsa_b = _row_major(ws, bs, ("b_q", "b_ks", "b_kw", "b_kc", "b_vc", "b_g"), 5 * LANES)
    nsa_wt, nsa_bt = _channel_major(ws, bs, ("b_vs", "b_vw"))
    fox_w, fox_b = _row_major(ws, bs, ("c_q", "c_k", "c_f"), 5 * LANES)
    fox_wt, fox_bt = _channel_major(ws, bs, ("c_v",))

    mla_w, mla_b, mla_wq, mla_wqs, mla_wk, mla_wvt, freq_row, sign_row = _mla_weights(ws, bs, mla_w_uq, mla_w_ukv)
    cos, sin = _rope_tables(positions, freq_row, sign_row)

    merge_wg = ws["g_merge"].astype(BF16)
    merge_bg = bs["g_merge"][:, None, :]
    merge_wb = w_branch.astype(BF16)
    merge_wo = w_o.astype(BF16)
    nsa_w1 = nsa_cmp_w1.reshape(L, 2, NSA_CMP_LEN, HEAD_DIM, NSA_CMP_HIDDEN).astype(BF16)
    nsa_w2 = nsa_cmp_w2.astype(BF16)
    xa_wq, xa_wo = xa_w_q.astype(BF16), xa_w_o.astype(BF16)
    xa_wk = xa_w_kv[:, :, :BRANCH_WIDTH].astype(BF16)
    xa_wvt = xa_w_kv[:, :, BRANCH_WIDTH:].transpose(0, 2, 1).astype(BF16)

    for l in range(L):
        ln = lambda i: (ln_g[l, i][None, :], ln_b[l, i][None, :])
        x2, x16 = _ffn_sublayer(x.reshape(B * S, D), *ffn1, *ln(0), l, True)
        x16 = x16.reshape(B, S, D)
        branches = (
            _moba(x16, moba_w, moba_b, moba_wt, moba_bt, l),
            _nsa(x16, nsa_w, nsa_b, nsa_wt, nsa_bt, nsa_cmp_pos, nsa_w1, nsa_w2, l),
            _fox(x16, fox_w, fox_b, fox_wt, fox_bt, l),
            _mla(x16, cos, sin, mla_w, mla_b, mla_q_norm[:, None, :], mla_kv_norm[:, None, :],
                 mla_wq, mla_wqs, mla_wk, mla_wvt, l),
        )
        x2 = _merge_sublayer(x2, [o.reshape(B * S, BRANCH_WIDTH) for o in branches],
                             merge_wg, merge_bg, merge_wb, merge_wo, *ln(1), l)
        x = _xattn_sublayer(x2.reshape(B, S, D), *_mem_kv(mem, xa_wk, xa_wvt, l), xa_wq, xa_wo, *ln(2), l)
        x = _ffn_sublayer(x.reshape(B * S, D), *ffn2, *ln(3), l, False)[0].reshape(B, S, D)
    return x
```

```python
import functools

import jax
import jax.numpy as jnp
import numpy as np
from jax import lax
from jax.experimental import pallas as pl
from jax.experimental.pallas import tpu as pltpu

F32 = jnp.float32
BF16 = jnp.bfloat16
HIGHEST = lax.Precision.HIGHEST

D_MODEL = 1024
DEPTH = 4
HEAD_DIM = 64
N_HEADS = 4
BRANCH_WIDTH = N_HEADS * HEAD_DIM
N_BRANCH = 4
MOBA_BLOCK = 256
MOBA_TOPK = 3
NSA_CMP_LEN = 32
NSA_CMP_STRIDE = 16
NSA_CMP_HIDDEN = 256
NSA_SLC_BLOCK = 64
NSA_TOP_N = 16
NSA_WINDOW = 512
NSA_FORCE_BONUS = 1.0e4
MLA_Q_RANK = 256
MLA_KV_RANK = 128
MLA_NOPE = 64
MLA_ROPE = 32
ROPE_BASE = 10000.0
MEM_LEN = 256
D_FF = 2816
ALPHA = (2 * DEPTH) ** 0.25
LN_EPS = 1e-5
RMS_EPS = 1e-6
NEG_INF = -1e30
MOBA_SLOPES = tuple(2.0 ** -(2 * h + 1) for h in range(N_HEADS))
NSA_SLOPES = tuple(2.0 ** -(2 * h + 2) for h in range(N_HEADS))

LOG2E = 1.4426950408889634
VT_ROWS = HEAD_DIM + 16
LANES = 128
TQ = 256
TK = 256
QK_WIDTH = 128
FOX_F_ROWS = 8
NSA_GATE_ROWS = 16
PAIR_WIDTH = 2 * HEAD_DIM
ROW_CHUNK = 512
FFN_TILE_F = 256
TOKEN_TILE = 512
VMEM_LIMIT = 56 * 1024 * 1024


def _nn(a, b):
    return jnp.dot(a, b, preferred_element_type=F32)


def _nt(a, b, precision=None):
    return lax.dot_general(a, b, (((1,), (1,)), ((), ())), precision=precision,
                           preferred_element_type=F32)


def _layer_norm(z, g, b):
    mu = jnp.mean(z, -1, keepdims=True)
    d = z - mu
    var = jnp.mean(d * d, -1, keepdims=True)
    return d * lax.rsqrt(var + LN_EPS) * g + b


def _params(n_parallel):
    return pltpu.CompilerParams(dimension_semantics=("parallel",) * n_parallel,
                                vmem_limit_bytes=VMEM_LIMIT)


def _ffn_kernel(x_ref, wi_ref, wo_ref, g_ref, b_ref, o_ref, *o16_ref):
    x = x_ref[...]
    xb = x.astype(BF16)
    acc = None
    for c in range(D_FF // FFN_TILE_F):
        lo = c * FFN_TILE_F
        hg = _nn(xb, wi_ref[:, lo:lo + FFN_TILE_F])
        hu = _nn(xb, wi_ref[:, D_FF + lo:D_FF + lo + FFN_TILE_F])
        a = (hg * jax.nn.sigmoid(hg) * hu).astype(BF16)
        part = _nn(a, wo_ref[lo:lo + FFN_TILE_F, :])
        acc = part if acc is None else acc + part
    out = _layer_norm(ALPHA * x + 0.5 * acc, g_ref[...], b_ref[...])
    o_ref[...] = out
    for ref in o16_ref:
        ref[...] = out.astype(BF16)


def _ffn_sublayer(x2, wi, wo, g, b, layer, with_bf16_copy):
    T, D = x2.shape
    tm = TOKEN_TILE
    out_spec = pl.BlockSpec((tm, D), lambda i: (i, 0))
    dtypes = (F32, BF16) if with_bf16_copy else (F32,)
    return pl.pallas_call(
        _ffn_kernel,
        grid=(T // tm,),
        in_specs=[
            pl.BlockSpec((tm, D), lambda i: (i, 0)),
            pl.BlockSpec((None, D, 2 * D_FF), lambda i: (layer, 0, 0)),
            pl.BlockSpec((None, D_FF, D), lambda i: (layer, 0, 0)),
            pl.BlockSpec((1, D), lambda i: (0, 0)),
            pl.BlockSpec((1, D), lambda i: (0, 0)),
        ],
        out_specs=[out_spec] * len(dtypes),
        out_shape=[jax.ShapeDtypeStruct((T, D), dt) for dt in dtypes],
        compiler_params=_params(1),
        name="ffn_sublayer",
    )(x2, wi, wo, g, b)


def _tile_iotas():
    r = lax.broadcasted_iota(jnp.int32, (TK, TQ), 0)
    c = lax.broadcasted_iota(jnp.int32, (TK, TQ), 1)
    return r, c


def _scores(s_ref, n_tiles, score_tile, h):
    m = None
    for j in range(n_tiles):
        s = score_tile(h, j)
        s_ref[j * TK:(j + 1) * TK, :] = s
        mj = jnp.max(s, axis=0, keepdims=True)
        m = mj if m is None else jnp.maximum(m, mj)
    return m


def _weighted_values(s_ref, n_tiles, m, vt_cols):
    p_all = jnp.concatenate([jnp.exp2(s_ref[j * TK:(j + 1) * TK, :] - m).astype(BF16) for j in range(n_tiles)],
                            axis=0)
    res = _nn(vt_cols, p_all)
    return res[:HEAD_DIM] / jnp.maximum(res[HEAD_DIM:HEAD_DIM + 1], 1e-30)


def _attend_heads(n_tiles, score_tile, vt_cols, out_ref, s_refs):
    s_a, s_b = s_refs
    m_first = _scores(s_a, n_tiles, score_tile, 0)

    def pair(i, m_a):
        h_a, h_b = 2 * i, 2 * i + 1
        h_next = jnp.minimum(2 * i + 2, N_HEADS - 1)
        m_b = _scores(s_b, n_tiles, score_tile, h_b)
        out_ref[h_a] = _weighted_values(s_a, n_tiles, m_a, vt_cols(h_a, n_tiles))
        m_next = _scores(s_a, n_tiles, score_tile, h_next)
        out_ref[h_b] = _weighted_values(s_b, n_tiles, m_b, vt_cols(h_b, n_tiles))
        return m_next

    lax.fori_loop(0, N_HEADS // 2, pair, m_first)


def _position_columns(n_rows, first, step):
    row = lax.broadcasted_iota(jnp.int32, (n_rows, HEAD_DIM), 0)
    lane = lax.broadcasted_iota(jnp.int32, (n_rows, HEAD_DIM), 1)
    pos = first + step * row
    hi = ((pos >> 7) << 7).astype(F32)
    lo = (pos & 127).astype(F32)
    return jnp.where(lane < 6, jnp.where((lane & 1) == 0, hi, lo), 0.0)


def _bf16_pieces(value):
    hi = float(np.asarray(value, np.float32).astype(BF16))
    mid = float(np.asarray(value - hi, np.float32).astype(BF16))
    return hi, mid, value - hi - mid


def _slope_columns(n_rows, slope):
    lane = lax.broadcasted_iota(jnp.int32, (n_rows, HEAD_DIM), 1)
    hi, mid, rest = _bf16_pieces(slope * LOG2E)
    return jnp.where(lane < 2, hi, jnp.where(lane < 4, mid, jnp.where(lane < 6, rest, 0.0))).astype(F32)


def _ones_columns(n_rows, n_cols):
    lane = lax.broadcasted_iota(jnp.int32, (n_rows, HEAD_DIM), 1)
    return jnp.where(lane < n_cols, 1.0, 0.0).astype(F32)


def _half_mask(n_rows, parity):
    lane = lax.broadcasted_iota(jnp.int32, (n_rows, PAIR_WIDTH), 1)
    return (lane // HEAD_DIM) == parity


def _bias_slab(cols, parity):
    zeros = jnp.zeros_like(cols)
    return jnp.concatenate([cols, zeros] if parity == 0 else [zeros, cols], axis=1)


def _stage_operand(pair_slab, parity, bias_cols, scale=None):
    feats = pair_slab if scale is None else pair_slab * scale
    own = _half_mask(pair_slab.shape[0], parity)
    return jnp.where(own, feats, _bias_slab(bias_cols, 1 - parity)).astype(BF16)


def _split3(a):
    hi = a.astype(BF16)
    rest = a - hi.astype(F32)
    mid = rest.astype(BF16)
    return hi, mid, (rest - mid.astype(F32)).astype(BF16)


def _project(x_ref, w_ref, b_ref, wt_ref, bt_ref, rm_ref, cm_ref):
    S = x_ref.shape[1]
    for i in range(S // ROW_CHUNK):
        rows = slice(i * ROW_CHUNK, (i + 1) * ROW_CHUNK)
        xb = x_ref[0, rows, :]
        rm_ref[rows, :] = _nn(xb, w_ref[...]) + b_ref[...]
        cm_ref[:, rows] = _nt(wt_ref[...], xb) + bt_ref[...]


def _stage_vt(vt_ref, v_t):
    G, _, S = vt_ref.shape
    vt_ref[:, 0:HEAD_DIM, :] = v_t.astype(BF16).reshape(G, HEAD_DIM, S)
    row = lax.broadcasted_iota(jnp.int32, (G, VT_ROWS - HEAD_DIM, S), 1)
    vt_ref[:, HEAD_DIM:VT_ROWS, :] = jnp.where(row == 0, 1.0, 0.0).astype(BF16)


def _causal_attention(o_ref, qp_ref, kp_ref, vt_ref, ot_ref, s_refs):
    S = qp_ref.shape[1]
    r, c = _tile_iotas()
    for qi in range(S // TQ):
        q0 = qi * TQ

        def score_tile(h, j, qi=qi, q0=q0):
            s = _nt(kp_ref[h, j * TK:(j + 1) * TK, :], qp_ref[h, q0:q0 + TQ, :])
            return jnp.where(r <= c, s, NEG_INF) if j == qi else s

        _attend_heads(qi + 1, score_tile, lambda h, n: vt_ref[h, :, 0:n * TK], ot_ref, s_refs)
        _store_heads(o_ref, q0, ot_ref[...].reshape(BRANCH_WIDTH, TQ))


def _store_heads(o_ref, q0, heads_t):
    o_ref[0, q0:q0 + TQ, :] = heads_t.T


def _attention_scratch(S):
    return [pltpu.VMEM((N_HEADS, HEAD_DIM, TQ), F32), pltpu.VMEM((S, TQ), F32), pltpu.VMEM((S, TQ), F32)]


def _moba_kernel(x_ref, w_ref, b_ref, wt_ref, bt_ref, o_ref, rm_ref, cm_ref, qp_ref, kp_ref, vt_ref, sel_ref,
                 ot_ref, sa_ref, sb_ref):
    S = x_ref.shape[1]
    nb = S // MOBA_BLOCK
    _project(x_ref, w_ref, b_ref, wt_ref, bt_ref, rm_ref, cm_ref)
    _stage_vt(vt_ref, cm_ref[...])
    kcols = _position_columns(S, 0, 1)
    blk = lax.broadcasted_iota(jnp.int32, (nb, S), 0)
    own = lax.broadcasted_iota(jnp.int32, (nb, S), 1) // MOBA_BLOCK
    cand = blk < own
    for h in range(N_HEADS):
        pair, parity = divmod(h, 2)
        q_pair = rm_ref[:, pair * PAIR_WIDTH:(pair + 1) * PAIR_WIDTH]
        k_pair = rm_ref[:, BRANCH_WIDTH + pair * PAIR_WIDTH:BRANCH_WIDTH + (pair + 1) * PAIR_WIDTH]
        qp_ref[h] = _stage_operand(q_pair, parity, _slope_columns(S, MOBA_SLOPES[h]), HEAD_DIM ** -0.5 * LOG2E)
        kp_ref[h] = _stage_operand(k_pair, parity, kcols)
        k_mean = jnp.sum(k_pair.reshape(nb, MOBA_BLOCK, PAIR_WIDTH), axis=1) * (1.0 / MOBA_BLOCK)
        k_mean = jnp.where(_half_mask(nb, parity), k_mean, 0.0)
        gate = jnp.where(cand, _nt(k_mean, q_pair, precision=HIGHEST), NEG_INF)
        beaten = jnp.zeros((nb, S), F32)
        for j in range(nb):
            other = gate[j:j + 1, :]
            wins = (other > gate) | ((other == gate) & (blk > j))
            beaten = beaten + jnp.where(wins, 1.0, 0.0)
        sel = jnp.where(cand & (beaten < MOBA_TOPK), 1.0, 0.0)
        for qi in range(S // TQ):
            sel_ref[h, qi] = sel[:, qi * TQ:(qi + 1) * TQ]

    r, c = _tile_iotas()
    for qi in range(S // TQ):
        q0 = qi * TQ

        def score_tile(h, j, qi=qi, q0=q0):
            s = _nt(kp_ref[h, j * TK:(j + 1) * TK, :], qp_ref[h, q0:q0 + TQ, :])
            return jnp.where(r <= c if j == qi else sel_ref[h, qi, j:j + 1, :] > 0.5, s, NEG_INF)

        _attend_heads(qi + 1, score_tile, lambda h, n: vt_ref[h, :, 0:n * TK], ot_ref, (sa_ref, sb_ref))
        _store_heads(o_ref, q0, ot_ref[...].reshape(BRANCH_WIDTH, TQ))


def _moba(x, w, b, wt, bt, layer):
    B, S, D = x.shape
    n_rm = w.shape[-1]
    n_cm = wt.shape[1]
    nb = S // MOBA_BLOCK
    return pl.pallas_call(
        _moba_kernel,
        grid=(B,),
        in_specs=[
            pl.BlockSpec((1, S, D), lambda i: (i, 0, 0)),
            pl.BlockSpec((None, D, n_rm), lambda i: (layer, 0, 0)),
            pl.BlockSpec((None, 1, n_rm), lambda i: (layer, 0, 0)),
            pl.BlockSpec((None, n_cm, D), lambda i: (layer, 0, 0)),
            pl.BlockSpec((None, n_cm, 1), lambda i: (layer, 0, 0)),
        ],
        out_specs=pl.BlockSpec((1, S, BRANCH_WIDTH), lambda i: (i, 0, 0)),
        out_shape=jax.ShapeDtypeStruct((B, S, BRANCH_WIDTH), F32),
        scratch_shapes=[
            pltpu.VMEM((S, n_rm), F32),
            pltpu.VMEM((n_cm, S), F32),
            pltpu.VMEM((N_HEADS, S, QK_WIDTH), BF16),
            pltpu.VMEM((N_HEADS, S, QK_WIDTH), BF16),
            pltpu.VMEM((N_HEADS, VT_ROWS, S), BF16),
            pltpu.VMEM((N_HEADS, S // TQ, nb, TQ), F32),
        ] + _attention_scratch(S),
        compiler_params=_params(1),
        name="moba_mixer",
    )(x, w, b, wt, bt)


def _fox_kernel(x_ref, w_ref, b_ref, wt_ref, bt_ref, o_ref, rm_ref, cm_ref, qp_ref, kp_ref, vt_ref,
                ot_ref, sa_ref, sb_ref):
    S = x_ref.shape[1]
    _project(x_ref, w_ref, b_ref, wt_ref, bt_ref, rm_ref, cm_ref)
    _stage_vt(vt_ref, cm_ref[0:BRANCH_WIDTH, :])
    f_t = cm_ref[BRANCH_WIDTH:BRANCH_WIDTH + FOX_F_ROWS, :]
    log_sig = jnp.minimum(f_t, 0.0) - jnp.log1p(jnp.exp(-jnp.abs(f_t)))
    blk = MOBA_BLOCK
    tri = jnp.where(lax.broadcasted_iota(jnp.int32, (blk, blk), 0) <= lax.broadcasted_iota(jnp.int32, (blk, blk), 1),
                    1.0, 0.0).astype(BF16)
    carry = jnp.zeros((FOX_F_ROWS, 1), F32)
    pieces = []
    for i in range(S // blk):
        hi, mid, lo = _split3(log_sig[:, i * blk:(i + 1) * blk])
        cs = _nn(hi, tri) + _nn(mid, tri) + _nn(lo, tri) + carry
        pieces.append(cs)
        carry = cs[:, blk - 1:blk]
    key_bias_t = jnp.concatenate(pieces, axis=1) * (-LOG2E)
    key_bias = jnp.concatenate([key_bias_t, jnp.zeros((LANES - FOX_F_ROWS, S), F32)], axis=0).T
    lane = lax.broadcasted_iota(jnp.int32, (S, HEAD_DIM), 1)
    ones = _ones_columns(S, 3)
    for h in range(N_HEADS):
        pair, parity = divmod(h, 2)
        q_pair = rm_ref[:, pair * PAIR_WIDTH:(pair + 1) * PAIR_WIDTH]
        k_pair = rm_ref[:, BRANCH_WIDTH + pair * PAIR_WIDTH:BRANCH_WIDTH + (pair + 1) * PAIR_WIDTH]
        d = key_bias[:, h:h + 1]
        d_hi = d.astype(BF16).astype(F32)
        d_mid = (d - d_hi).astype(BF16).astype(F32)
        d_lo = d - d_hi - d_mid
        dcols = jnp.where(lane == 0, d_hi, jnp.where(lane == 1, d_mid, jnp.where(lane == 2, d_lo, 0.0)))
        qp_ref[h] = _stage_operand(q_pair, parity, ones, HEAD_DIM ** -0.5 * LOG2E)
        kp_ref[h] = _stage_operand(k_pair, parity, dcols)

    _causal_attention(o_ref, qp_ref, kp_ref, vt_ref, ot_ref, (sa_ref, sb_ref))


def _fox(x, w, b, wt, bt, layer):
    B, S, D = x.shape
    n_rm = w.shape[-1]
    n_cm = wt.shape[1]
    return pl.pallas_call(
        _fox_kernel,
        grid=(B,),
        in_specs=[
            pl.BlockSpec((1, S, D), lambda i: (i, 0, 0)),
            pl.BlockSpec((None, D, n_rm), lambda i: (layer, 0, 0)),
            pl.BlockSpec((None, 1, n_rm), lambda i: (layer, 0, 0)),
            pl.BlockSpec((None, n_cm, D), lambda i: (layer, 0, 0)),
            pl.BlockSpec((None, n_cm, 1), lambda i: (layer, 0, 0)),
        ],
        out_specs=pl.BlockSpec((1, S, BRANCH_WIDTH), lambda i: (i, 0, 0)),
        out_shape=jax.ShapeDtypeStruct((B, S, BRANCH_WIDTH), F32),
        scratch_shapes=[
            pltpu.VMEM((S, n_rm), F32),
            pltpu.VMEM((n_cm, S), F32),
            pltpu.VMEM((N_HEADS, S, QK_WIDTH), BF16),
            pltpu.VMEM((N_HEADS, S, QK_WIDTH), BF16),
            pltpu.VMEM((N_HEADS, VT_ROWS, S), BF16),
        ] + _attention_scratch(S),
        compiler_params=_params(1),
        name="fox_mixer",
    )(x, w, b, wt, bt)


def _rope_table_kernel(pos_ref, freq_ref, sign_ref, cos_ref, sin_ref):
    ang = pos_ref[0].astype(F32) * freq_ref[...]
    cos_ref[0] = jnp.cos(ang)
    sin_ref[0] = jnp.sin(ang) * sign_ref[...]


def _rope_tables(positions, freq_row, sign_row):
    B, S = positions.shape
    return pl.pallas_call(
        _rope_table_kernel,
        grid=(B,),
        in_specs=[
            pl.BlockSpec((1, S, 1), lambda i: (i, 0, 0)),
            pl.BlockSpec((1, LANES), lambda i: (0, 0)),
            pl.BlockSpec((1, LANES), lambda i: (0, 0)),
        ],
        out_specs=[pl.BlockSpec((1, S, LANES), lambda i: (i, 0, 0))] * 2,
        out_shape=[jax.ShapeDtypeStruct((B, S, LANES), F32)] * 2,
        compiler_params=_params(1),
        name="rope_tables",
    )(positions.reshape(B, S, 1), freq_row, sign_row)


def _rms_norm(x, g):
    return x * lax.rsqrt(jnp.mean(x * x, -1, keepdims=True) + RMS_EPS) * g


def _mla_kernel(x_ref, cos_ref, sin_ref, w_ref, b_ref, qn_ref, kvn_ref, wq_ref, wqs_ref, wk_ref, wvt_ref,
                o_ref, rm_ref, qp_ref, kp_ref, vt_ref, ot_ref, sa_ref, sb_ref):
    S = x_ref.shape[1]
    for i in range(S // ROW_CHUNK):
        rows = slice(i * ROW_CHUNK, (i + 1) * ROW_CHUNK)
        rm_ref[rows, :] = _nn(x_ref[0, rows, :], w_ref[...]) + b_ref[...]
    cos = cos_ref[0]
    sin = sin_ref[0]
    scale = (MLA_NOPE + MLA_ROPE) ** -0.5 * LOG2E
    q_lat = _rms_norm(rm_ref[:, 0:MLA_Q_RANK], qn_ref[...]).astype(BF16)
    kv_lat = _rms_norm(rm_ref[:, MLA_Q_RANK:MLA_Q_RANK + MLA_KV_RANK], kvn_ref[...]).astype(BF16)
    c0 = MLA_Q_RANK + MLA_KV_RANK
    kr = rm_ref[:, c0:c0 + LANES]
    k_rope = kr * cos + pltpu.roll(kr, LANES - MLA_ROPE, axis=1) * sin
    _stage_vt(vt_ref, _nt(wvt_ref[...], kv_lat))
    for h in range(N_HEADS):
        qh = _nn(q_lat, wq_ref[h]) * cos + _nn(q_lat, wqs_ref[h]) * sin
        qp_ref[h] = (qh * scale).astype(BF16)
        kp_ref[h] = (_nn(kv_lat, wk_ref[h]) + k_rope).astype(BF16)

    _causal_attention(o_ref, qp_ref, kp_ref, vt_ref, ot_ref, (sa_ref, sb_ref))


def _mla(x, cos, sin, w, b, qn, kvn, wq, wqs, wk, wvt, layer):
    B, S, D = x.shape
    n_rm = w.shape[-1]
    lay3 = lambda i: (layer, 0, 0)
    lay4 = lambda i: (layer, 0, 0, 0)
    return pl.pallas_call(
        _mla_kernel,
        grid=(B,),
        in_specs=[
            pl.BlockSpec((1, S, D), lambda i: (i, 0, 0)),
            pl.BlockSpec((1, S, LANES), lambda i: (i, 0, 0)),
            pl.BlockSpec((1, S, LANES), lambda i: (i, 0, 0)),
            pl.BlockSpec((None, D, n_rm), lay3),
            pl.BlockSpec((None, 1, n_rm), lay3),
            pl.BlockSpec((None, 1, MLA_Q_RANK), lay3),
            pl.BlockSpec((None, 1, MLA_KV_RANK), lay3),
            pl.BlockSpec((None, N_HEADS, MLA_Q_RANK, QK_WIDTH), lay4),
            pl.BlockSpec((None, N_HEADS, MLA_Q_RANK, QK_WIDTH), lay4),
            pl.BlockSpec((None, N_HEADS, MLA_KV_RANK, QK_WIDTH), lay4),
            pl.BlockSpec((None, BRANCH_WIDTH, MLA_KV_RANK), lay3),
        ],
        out_specs=pl.BlockSpec((1, S, BRANCH_WIDTH), lambda i: (i, 0, 0)),
        out_shape=jax.ShapeDtypeStruct((B, S, BRANCH_WIDTH), F32),
        scratch_shapes=[
            pltpu.VMEM((S, n_rm), F32),
            pltpu.VMEM((N_HEADS, S, QK_WIDTH), BF16),
            pltpu.VMEM((N_HEADS, S, QK_WIDTH), BF16),
            pltpu.VMEM((N_HEADS, VT_ROWS, S), BF16),
        ] + _attention_scratch(S),
        compiler_params=_params(1),
        name="mla_mixer",
    )(x, cos, sin, w, b, qn, kvn, wq, wqs, wk, wvt)


N_CMP_PAD = 128


def _nsa_compress(src_ref, pos_ref, w1_ref, w2_ref):
    accs = [jnp.zeros((N_CMP_PAD, NSA_CMP_HIDDEN), F32) for _ in range(2)]
    n_cmp = N_CMP_PAD - 1
    for p in range(NSA_CMP_LEN):
        rows = src_ref[pl.ds(p, n_cmp, stride=NSA_CMP_STRIDE), :]
        for which in range(2):
            part = rows[:, which * HEAD_DIM:(which + 1) * HEAD_DIM] + pos_ref[which, p:p + 1, :]
            part = jnp.concatenate([part, jnp.zeros((1, HEAD_DIM), F32)], axis=0)
            accs[which] = accs[which] + _nn(part.astype(BF16), w1_ref[which, p])
    return [_nn(jax.nn.gelu(accs[which]).astype(BF16), w2_ref[which]) for which in range(2)]


def _nsa_kernel(x_ref, w_ref, b_ref, wt_ref, bt_ref, pos_ref, w1_ref, w2_ref, o_ref,
                rm_ref, cm_ref, cin_ref, qp_ref, ks_ref, kw_ref, kc_ref, vt_ref, vtc_ref, sel_ref, gate_ref,
                oslc_ref, owin_ref, sa_ref, sb_ref):
    S = x_ref.shape[1]
    n_slc = S // NSA_SLC_BLOCK
    _project(x_ref, w_ref, b_ref, wt_ref, bt_ref, rm_ref, cm_ref)
    _stage_vt(vt_ref, cm_ref[0:2 * HEAD_DIM, :])
    gate_ref[...] = jax.nn.sigmoid(cm_ref[2 * HEAD_DIM:2 * HEAD_DIM + NSA_GATE_ROWS, :])
    c_kk, c_kc = BRANCH_WIDTH, BRANCH_WIDTH + PAIR_WIDTH
    kcols = _position_columns(S, 0, 1)
    for h in range(N_HEADS):
        pair, parity = divmod(h, 2)
        qp_ref[h] = _stage_operand(rm_ref[:, pair * PAIR_WIDTH:(pair + 1) * PAIR_WIDTH], parity,
                                   _slope_columns(S, NSA_SLOPES[h]), HEAD_DIM ** -0.5 * LOG2E)
    slab = rm_ref[:, c_kk:c_kk + PAIR_WIDTH]
    swapped = pltpu.roll(slab, HEAD_DIM, axis=1)
    ks_ref[0] = _stage_operand(slab, 0, kcols)
    ks_ref[1] = _stage_operand(swapped, 1, kcols)
    kw_ref[0] = _stage_operand(swapped, 0, kcols)
    kw_ref[1] = _stage_operand(slab, 1, kcols)
    cin_ref[...] = rm_ref[:, c_kc:c_kc + 2 * HEAD_DIM]
    k_cmp, v_cmp = _nsa_compress(cin_ref, pos_ref, w1_ref, w2_ref)
    kv_cmp = jnp.concatenate([k_cmp, v_cmp], axis=1)
    ccols = _position_columns(N_CMP_PAD, NSA_CMP_LEN - 1, NSA_CMP_STRIDE)
    kc_ref[0] = _stage_operand(kv_cmp, 0, ccols)
    kc_ref[1] = _stage_operand(pltpu.roll(kv_cmp, HEAD_DIM, axis=1), 1, ccols)
    vtc_ref[...] = kv_cmp.T[HEAD_DIM:, :].astype(BF16)

    jj = lax.broadcasted_iota(jnp.int32, (n_slc, N_CMP_PAD), 0) * NSA_SLC_BLOCK
    cc = lax.broadcasted_iota(jnp.int32, (n_slc, N_CMP_PAD), 1) * NSA_CMP_STRIDE
    overlap_t = jnp.where((cc < jj + NSA_SLC_BLOCK) & (cc + NSA_CMP_LEN > jj), 1.0, 0.0).astype(BF16)

    r, c = _tile_iotas()
    per_tile = TK // NSA_SLC_BLOCK
    win_tiles = NSA_WINDOW // TK
    c_idx = lax.broadcasted_iota(jnp.int32, (N_CMP_PAD, TQ), 0)
    lane_cmp = lax.broadcasted_iota(jnp.int32, (N_CMP_PAD, TQ), 1)
    for qi in range(S // TQ):
        q0 = qi * TQ
        cmp_ok = (c_idx * NSA_CMP_STRIDE + (NSA_CMP_LEN - 1) <= q0 + lane_cmp) & (c_idx < N_CMP_PAD - 1)
        imp = jnp.zeros((n_slc, TQ), F32)
        o_cmp = []
        for h in range(N_HEADS):
            s = jnp.where(cmp_ok, _nt(kc_ref[h % 2], qp_ref[h, q0:q0 + TQ, :]), NEG_INF)
            e = jnp.where(cmp_ok, jnp.exp2(s - jnp.max(s, axis=0, keepdims=True)), 0.0)
            p = e / jnp.maximum(jnp.sum(e, axis=0, keepdims=True), 1e-30)
            p_hi, p_mid, p_lo = _split3(p)
            o_cmp.append(_nn(vtc_ref[...], p_hi))
            imp = imp + _nn(overlap_t, p_hi) + _nn(overlap_t, p_mid) + _nn(overlap_t, p_lo)
        n_seen = per_tile * (qi + 1)
        select = n_seen > NSA_TOP_N
        if select:
            j_idx = lax.broadcasted_iota(jnp.int32, (n_seen, TQ), 0)
            own = (q0 + lax.broadcasted_iota(jnp.int32, (n_seen, TQ), 1)) // NSA_SLC_BLOCK
            forced = (j_idx == 0) | (j_idx == own) | (j_idx == own - 1)
            score = jnp.where(j_idx <= own, imp[:n_seen] + NSA_FORCE_BONUS * jnp.where(forced, 1.0, 0.0), NEG_INF)
            beaten = jnp.zeros((n_seen, TQ), F32)
            for j in range(n_seen):
                other = score[j:j + 1, :]
                wins = (other > score) | ((other == score) & (j_idx > j))
                beaten = beaten + jnp.where(wins, 1.0, 0.0)
            sel_ref[0:n_seen, :] = jnp.where(beaten < NSA_TOP_N, 1.0, 0.0)

        def slc_tile(h, j, qi=qi, q0=q0, select=select):
            s = _nt(ks_ref[h % 2, j * TK:(j + 1) * TK, :], qp_ref[h, q0:q0 + TQ, :])
            mask = r <= c if j == qi else None
            if select:
                rows = [jnp.broadcast_to(sel_ref[per_tile * j + u:per_tile * j + u + 1, :], (NSA_SLC_BLOCK, TQ))
                        for u in range(per_tile)]
                chosen = jnp.concatenate(rows, axis=0) > 0.5
                mask = chosen if mask is None else mask & chosen
            return s if mask is None else jnp.where(mask, s, NEG_INF)

        _attend_heads(qi + 1, slc_tile, lambda h, n: vt_ref[0, :, 0:n * TK], oslc_ref, (sa_ref, sb_ref))

        lo = max(qi - win_tiles, 0)

        def win_tile(h, j, qi=qi, q0=q0, lo=lo):
            t = lo + j
            s = _nt(kw_ref[h % 2, t * TK:(t + 1) * TK, :], qp_ref[h, q0:q0 + TQ, :])
            if t == qi:
                return jnp.where(r <= c, s, NEG_INF)
            return jnp.where(c < r, s, NEG_INF) if t == qi - win_tiles else s

        _attend_heads(qi - lo + 1, win_tile, lambda h, n, lo=lo: vt_ref[1, :, lo * TK:(lo + n) * TK], owin_ref,
                      (sa_ref, sb_ref))

        gates = gate_ref[:, q0:q0 + TQ]
        outs = [gates[3 * h:3 * h + 1, :] * o_cmp[h] + gates[3 * h + 1:3 * h + 2, :] * oslc_ref[h]
                + gates[3 * h + 2:3 * h + 3, :] * owin_ref[h] for h in range(N_HEADS)]
        _store_heads(o_ref, q0, jnp.concatenate(outs, axis=0))


def _nsa(x, w, b, wt, bt, pos, w1, w2, layer):
    B, S, D = x.shape
    n_rm = w.shape[-1]
    n_cm = wt.shape[1]
    lay3 = lambda i: (layer, 0, 0)
    lay4 = lambda i: (layer, 0, 0, 0)
    lay5 = lambda i: (layer, 0, 0, 0, 0)
    return pl.pallas_call(
        _nsa_kernel,
        grid=(B,),
        in_specs=[
            pl.BlockSpec((1, S, D), lambda i: (i, 0, 0)),
            pl.BlockSpec((None, D, n_rm), lay3),
            pl.BlockSpec((None, 1, n_rm), lay3),
            pl.BlockSpec((None, n_cm, D), lay3),
            pl.BlockSpec((None, n_cm, 1), lay3),
            pl.BlockSpec((None, 2, NSA_CMP_LEN, HEAD_DIM), lay4),
            pl.BlockSpec((None, 2, NSA_CMP_LEN, HEAD_DIM, NSA_CMP_HIDDEN), lay5),
            pl.BlockSpec((None, 2, NSA_CMP_HIDDEN, HEAD_DIM), lay4),
        ],
        out_specs=pl.BlockSpec((1, S, BRANCH_WIDTH), lambda i: (i, 0, 0)),
        out_shape=jax.ShapeDtypeStruct((B, S, BRANCH_WIDTH), F32),
        scratch_shapes=[
            pltpu.VMEM((S, n_rm), F32),
            pltpu.VMEM((n_cm, S), F32),
            pltpu.VMEM((S, 2 * HEAD_DIM), F32),
            pltpu.VMEM((N_HEADS, S, QK_WIDTH), BF16),
            pltpu.VMEM((2, S, QK_WIDTH), BF16),
            pltpu.VMEM((2, S, QK_WIDTH), BF16),
            pltpu.VMEM((2, N_CMP_PAD, QK_WIDTH), BF16),
            pltpu.VMEM((2, VT_ROWS, S), BF16),
            pltpu.VMEM((HEAD_DIM, N_CMP_PAD), BF16),
            pltpu.VMEM((S // NSA_SLC_BLOCK, TQ), F32),
            pltpu.VMEM((NSA_GATE_ROWS, S), F32),
            pltpu.VMEM((N_HEADS, HEAD_DIM, TQ), F32),
        ] + _attention_scratch(S),
        compiler_params=_params(1),
        name="nsa_mixer",
    )(x, w, b, wt, bt, pos, w1, w2)


def _merge_kernel(x_ref, oa_ref, ob_ref, oc_ref, od_ref, wg_ref, bg_ref, wb_ref, wo_ref, g_ref, b_ref, o_ref):
    x = x_ref[...]
    xb = x.astype(BF16)
    D = x.shape[-1]
    acc = jnp.zeros(x.shape, F32)
    for n, branch in enumerate((oa_ref, ob_ref, oc_ref, od_ref)):
        gate = jax.nn.sigmoid(_nn(xb, wg_ref[:, n * D:(n + 1) * D]) + bg_ref[:, n * D:(n + 1) * D])
        acc = acc + gate * _nn(branch[...].astype(BF16), wb_ref[n])
    mix = _nn(acc.astype(BF16), wo_ref[...])
    o_ref[...] = _layer_norm(ALPHA * x + mix, g_ref[...], b_ref[...])


def _merge_sublayer(x2, branches, wg, bg, wb, wo, g, b, layer):
    T, D = x2.shape
    tm = TOKEN_TILE
    lay3 = lambda i: (layer, 0, 0)
    return pl.pallas_call(
        _merge_kernel,
        grid=(T // tm,),
        in_specs=[pl.BlockSpec((tm, D), lambda i: (i, 0))]
        + [pl.BlockSpec((tm, BRANCH_WIDTH), lambda i: (i, 0))] * N_BRANCH
        + [
            pl.BlockSpec((None, D, N_BRANCH * D), lay3),
            pl.BlockSpec((None, 1, N_BRANCH * D), lay3),
            pl.BlockSpec((None, N_BRANCH, BRANCH_WIDTH, D), lambda i: (layer, 0, 0, 0)),
            pl.BlockSpec((None, D, D), lay3),
            pl.BlockSpec((1, D), lambda i: (0, 0)),
            pl.BlockSpec((1, D), lambda i: (0, 0)),
        ],
        out_specs=pl.BlockSpec((tm, D), lambda i: (i, 0)),
        out_shape=jax.ShapeDtypeStruct((T, D), F32),
        compiler_params=_params(1),
        name="merge_sublayer",
    )(x2, *branches, wg, bg, wb, wo, g, b)


def _mem_kv_kernel(mem_ref, wk_ref, wvt_ref, kp_ref, vt_ref):
    mem = mem_ref[0].astype(BF16)
    k = _nn(mem, wk_ref[...])
    no_bias = jnp.zeros((k.shape[0], HEAD_DIM), F32)
    for h in range(N_HEADS):
        pair, parity = divmod(h, 2)
        kp_ref[0, h] = _stage_operand(k[:, pair * PAIR_WIDTH:(pair + 1) * PAIR_WIDTH], parity, no_bias)
    _stage_vt(vt_ref.at[0], _nt(wvt_ref[...], mem))


def _mem_kv(mem, w_k, w_vt, layer):
    B, M, D = mem.shape
    lay3 = lambda i: (layer, 0, 0)
    return pl.pallas_call(
        _mem_kv_kernel,
        grid=(B,),
        in_specs=[pl.BlockSpec((1, M, D), lambda i: (i, 0, 0)),
                  pl.BlockSpec((None, D, BRANCH_WIDTH), lay3),
                  pl.BlockSpec((None, BRANCH_WIDTH, D), lay3)],
        out_specs=[pl.BlockSpec((1, N_HEADS, M, QK_WIDTH), lambda i: (i, 0, 0, 0)),
                   pl.BlockSpec((1, N_HEADS, VT_ROWS, M), lambda i: (i, 0, 0, 0))],
        out_shape=[jax.ShapeDtypeStruct((B, N_HEADS, M, QK_WIDTH), BF16),
                   jax.ShapeDtypeStruct((B, N_HEADS, VT_ROWS, M), BF16)],
        compiler_params=_params(1),
        name="mem_kv",
    )(mem, w_k, w_vt)


def _xattn_kernel(x_ref, kp_ref, vt_ref, wq_ref, wo_ref, g_ref, b_ref, o_ref):
    x = x_ref[0]
    q = _nn(x.astype(BF16), wq_ref[...]) * (HEAD_DIM ** -0.5 * LOG2E)
    no_bias = jnp.zeros((x.shape[0], HEAD_DIM), F32)
    heads = []
    for h in range(N_HEADS):
        pair, parity = divmod(h, 2)
        qp = _stage_operand(q[:, pair * PAIR_WIDTH:(pair + 1) * PAIR_WIDTH], parity, no_bias)
        s = _nt(kp_ref[0, h], qp)
        p = jnp.exp2(s - jnp.max(s, axis=0, keepdims=True)).astype(BF16)
        res = _nn(vt_ref[0, h], p)
        heads.append(res[:HEAD_DIM] / res[HEAD_DIM:HEAD_DIM + 1])
    att = jnp.concatenate(heads, axis=0).T.astype(BF16)
    o_ref[0] = _layer_norm(ALPHA * x + _nn(att, wo_ref[...]), g_ref[...], b_ref[...])


def _xattn_sublayer(x, kp, vt, wq, wo, g, b, layer):
    B, S, D = x.shape
    M = kp.shape[2]
    tm = TOKEN_TILE
    lay3 = lambda i, j: (layer, 0, 0)
    return pl.pallas_call(
        _xattn_kernel,
        grid=(B, S // tm),
        in_specs=[
            pl.BlockSpec((1, tm, D), lambda i, j: (i, j, 0)),
            pl.BlockSpec((1, N_HEADS, M, QK_WIDTH), lambda i, j: (i, 0, 0, 0)),
            pl.BlockSpec((1, N_HEADS, VT_ROWS, M), lambda i, j: (i, 0, 0, 0)),
            pl.BlockSpec((None, D, BRANCH_WIDTH), lay3),
            pl.BlockSpec((None, BRANCH_WIDTH, D), lay3),
            pl.BlockSpec((1, D), lambda i, j: (0, 0)),
            pl.BlockSpec((1, D), lambda i, j: (0, 0)),
        ],
        out_specs=pl.BlockSpec((1, tm, D), lambda i, j: (i, j, 0)),
        out_shape=jax.ShapeDtypeStruct((B, S, D), F32),
        compiler_params=_params(2),
        name="xattn_sublayer",
    )(x, kp, vt, wq, wo, g, b)


def _pad_cols(w, n):
    return jnp.pad(w, [(0, 0)] * (w.ndim - 1) + [(0, n - w.shape[-1])])


def _split_w_in(w_in, b_in):
    hd, bw = HEAD_DIM, BRANCH_WIDTH
    sizes = (bw, bw, bw, bw, hd, hd, hd, hd, hd, hd, 3 * N_HEADS, bw, bw, bw, N_HEADS,
             MLA_Q_RANK, MLA_KV_RANK, MLA_ROPE, N_BRANCH * D_MODEL)
    names = ("a_q", "a_k", "a_v", "b_q", "b_kc", "b_vc", "b_ks", "b_vs", "b_kw", "b_vw", "b_g",
             "c_q", "c_k", "c_v", "c_f", "d_cq", "d_ckv", "d_kr", "g_merge")
    w, b, off = {}, {}, 0
    for name, size in zip(names, sizes):
        w[name] = w_in[:, :, off:off + size]
        b[name] = b_in[:, off:off + size]
        off += size
    return w, b


def _row_major(ws, bs, names, pad_to):
    w = _pad_cols(jnp.concatenate([ws[n] for n in names], axis=-1), pad_to).astype(BF16)
    b = _pad_cols(jnp.concatenate([bs[n] for n in names], axis=-1), pad_to)[:, None, :]
    return w, b


def _channel_major(ws, bs, names, pad_to):
    w = _pad_cols(jnp.concatenate([ws[n] for n in names], axis=-1), pad_to).transpose(0, 2, 1).astype(BF16)
    b = _pad_cols(jnp.concatenate([bs[n] for n in names], axis=-1), pad_to)[:, :, None]
    return w, b


def _rope_slab(a, swap):
    half = MLA_ROPE // 2
    first, second = a[..., :half], a[..., half:]
    body = jnp.concatenate([second, first] if swap else [first, second], axis=-1)
    return jnp.pad(body, [(0, 0)] * (a.ndim - 1) + [(MLA_NOPE, LANES - MLA_NOPE - MLA_ROPE)])


def _mla_weights(ws, bs, mla_w_uq, mla_w_ukv):
    L = mla_w_uq.shape[0]
    half = MLA_ROPE // 2
    def key_slab(a):
        return jnp.concatenate([_rope_slab(a, False)[..., :MLA_NOPE + MLA_ROPE], _rope_slab(a, True)[..., MLA_NOPE:MLA_NOPE + MLA_ROPE]],
                               axis=-1)

    w = jnp.concatenate([ws["d_cq"], ws["d_ckv"], key_slab(ws["d_kr"])], axis=-1).astype(BF16)
    b = jnp.concatenate([bs["d_cq"], bs["d_ckv"], key_slab(bs["d_kr"])], axis=-1)[:, None, :]
    uq = mla_w_uq.reshape(L, MLA_Q_RANK, N_HEADS, MLA_NOPE + MLA_ROPE).transpose(0, 2, 1, 3)
    wq = jnp.concatenate([uq[..., :MLA_NOPE], _rope_slab(uq[..., MLA_NOPE:], False)[..., MLA_NOPE:]],
                         axis=-1).astype(BF16)
    wqs = jnp.concatenate([jnp.zeros_like(uq[..., :MLA_NOPE]), _rope_slab(uq[..., MLA_NOPE:], True)[..., MLA_NOPE:]],
                          axis=-1).astype(BF16)
    ukv = mla_w_ukv.reshape(L, MLA_KV_RANK, N_HEADS, MLA_NOPE + HEAD_DIM).transpose(0, 2, 1, 3)
    wk = _pad_cols(ukv[..., :MLA_NOPE], QK_WIDTH).astype(BF16)
    wvt = ukv[..., MLA_NOPE:].transpose(0, 1, 3, 2).reshape(L, BRANCH_WIDTH, MLA_KV_RANK).astype(BF16)
    inv_freq = ROPE_BASE ** (-jnp.arange(0, MLA_ROPE, 2, dtype=F32) / MLA_ROPE)
    freq_row = _rope_slab(jnp.concatenate([inv_freq, inv_freq])[None, :], False)
    sign_row = _rope_slab(jnp.concatenate([-jnp.ones((half,), F32), jnp.ones((half,), F32)])[None, :], False)
    return w, b, wq, wqs, wk, wvt, freq_row, sign_row


def kernel(x, mem, positions, ln_g, ln_b, ffn1_w_in, ffn1_w_out, ffn2_w_in, ffn2_w_out, w_in, b_in, w_branch, w_o,
           mla_q_norm, mla_w_uq, mla_kv_norm, mla_w_ukv, nsa_cmp_pos, nsa_cmp_w1, nsa_cmp_w2, xa_w_q, xa_w_kv, xa_w_o):
    B, S, D = x.shape
    L = ln_g.shape[0]
    ffn1 = (ffn1_w_in.astype(BF16), ffn1_w_out.astype(BF16))
    ffn2 = (ffn2_w_in.astype(BF16), ffn2_w_out.astype(BF16))

    ws, bs = _split_w_in(w_in.astype(BF16), b_in)
    moba_w, moba_b = _row_major(ws, bs, ("a_q", "a_k"), 2 * BRANCH_WIDTH)
    moba_wt, moba_bt = _channel_major(ws, bs, ("a_v",), BRANCH_WIDTH)
    nsa_w, nsa_b = _row_major(ws, bs, ("b_q", "b_ks", "b_kw", "b_kc", "b_vc"), 2 * BRANCH_WIDTH)
    nsa_wt, nsa_bt = _channel_major(ws, bs, ("b_vs", "b_vw", "b_g"), 2 * HEAD_DIM + NSA_GATE_ROWS)
    fox_w, fox_b = _row_major(ws, bs, ("c_q", "c_k"), 2 * BRANCH_WIDTH)
    fox_wt, fox_bt = _channel_major(ws, bs, ("c_v", "c_f"), BRANCH_WIDTH + 2 * FOX_F_ROWS)

    mla_w, mla_b, mla_wq, mla_wqs, mla_wk, mla_wvt, freq_row, sign_row = _mla_weights(ws, bs, mla_w_uq, mla_w_ukv)
    cos, sin = _rope_tables(positions, freq_row, sign_row)

    merge_wg = ws["g_merge"].astype(BF16)
    merge_bg = bs["g_merge"][:, None, :]
    merge_wb = w_branch.astype(BF16)
    merge_wo = w_o.astype(BF16)
    nsa_w1 = nsa_cmp_w1.reshape(L, 2, NSA_CMP_LEN, HEAD_DIM, NSA_CMP_HIDDEN).astype(BF16)
    nsa_w2 = nsa_cmp_w2.astype(BF16)
    xa_wq, xa_wo = xa_w_q.astype(BF16), xa_w_o.astype(BF16)
    xa_wk = xa_w_kv[:, :, :BRANCH_WIDTH].astype(BF16)
    xa_wvt = xa_w_kv[:, :, BRANCH_WIDTH:].transpose(0, 2, 1).astype(BF16)

    for l in range(L):
        ln = lambda i: (ln_g[l, i][None, :], ln_b[l, i][None, :])
        x2, x16 = _ffn_sublayer(x.reshape(B * S, D), *ffn1, *ln(0), l, True)
        x16 = x16.reshape(B, S, D)
        branches = (
            _moba(x16, moba_w, moba_b, moba_wt, moba_bt, l),
            _nsa(x16, nsa_w, nsa_b, nsa_wt, nsa_bt, nsa_cmp_pos, nsa_w1, nsa_w2, l),
            _fox(x16, fox_w, fox_b, fox_wt, fox_bt, l),
            _mla(x16, cos, sin, mla_w, mla_b, mla_q_norm[:, None, :], mla_kv_norm[:, None, :],
                 mla_wq, mla_wqs, mla_wk, mla_wvt, l),
        )
        x2 = _merge_sublayer(x2, [o.reshape(B * S, BRANCH_WIDTH) for o in branches],
                             merge_wg, merge_bg, merge_wb, merge_wo, *ln(1), l)
        x = _xattn_sublayer(x2.reshape(B, S, D), *_mem_kv(mem, xa_wk, xa_wvt, l), xa_wq, xa_wo, *ln(2), l)
        x = _ffn_sublayer(x.reshape(B * S, D), *ffn2, *ln(3), l, False)[0].reshape(B, S, D)
    return x
```

```python
import functools

import jax
import jax.numpy as jnp
import numpy as np
from jax import lax
from jax.experimental import pallas as pl
from jax.experimental.pallas import tpu as pltpu

F32 = jnp.float32
BF16 = jnp.bfloat16

D_MODEL = 1024
DEPTH = 4
HEAD_DIM = 64
N_HEADS = 4
BRANCH_WIDTH = N_HEADS * HEAD_DIM
N_BRANCH = 4
MOBA_BLOCK = 256
MOBA_TOPK = 3
NSA_CMP_LEN = 32
NSA_CMP_STRIDE = 16
NSA_CMP_HIDDEN = 256
NSA_SLC_BLOCK = 64
NSA_TOP_N = 16
NSA_WINDOW = 512
NSA_FORCE_BONUS = 1.0e4
MLA_Q_RANK = 256
MLA_KV_RANK = 128
MLA_NOPE = 64
MLA_ROPE = 32
ROPE_BASE = 10000.0
MEM_LEN = 256
D_FF = 2816
ALPHA = (2 * DEPTH) ** 0.25
LN_EPS = 1e-5
RMS_EPS = 1e-6
NEG_INF = -1e30
MOBA_SLOPES = tuple(2.0 ** -(2 * h + 1) for h in range(N_HEADS))
NSA_SLOPES = tuple(2.0 ** -(2 * h + 2) for h in range(N_HEADS))

LOG2E = 1.4426950408889634
VT_ROWS = HEAD_DIM + 16
LANES = 128
TQ = 256
TK = 256
QK_WIDTH = 128
FOX_F_ROWS = 8
NSA_GATE_ROWS = 16
PAIR_WIDTH = 2 * HEAD_DIM
ROW_CHUNK = 512
FFN_TILE_F = 256
TOKEN_TILE = 512
XATTN_SPLIT = 2
VMEM_LIMIT = 56 * 1024 * 1024


def _nn(a, b):
    return jnp.dot(a, b, preferred_element_type=F32)


def _nt(a, b):
    return lax.dot_general(a, b, (((1,), (1,)), ((), ())), preferred_element_type=F32)


def _layer_norm(z, g, b):
    mu = jnp.mean(z, -1, keepdims=True)
    d = z - mu
    var = jnp.mean(d * d, -1, keepdims=True)
    return d * lax.rsqrt(var + LN_EPS) * g + b


def _params(n_parallel):
    return pltpu.CompilerParams(dimension_semantics=("parallel",) * n_parallel,
                                vmem_limit_bytes=VMEM_LIMIT)


def _ffn_kernel(x_ref, wi_ref, wo_ref, g_ref, b_ref, o_ref, *o16_ref):
    x = x_ref[...]
    xb = x.astype(BF16)
    acc = None
    for c in range(D_FF // FFN_TILE_F):
        lo = c * FFN_TILE_F
        hg = _nn(xb, wi_ref[:, lo:lo + FFN_TILE_F])
        hu = _nn(xb, wi_ref[:, D_FF + lo:D_FF + lo + FFN_TILE_F])
        a = (hg * jax.nn.sigmoid(hg) * hu).astype(BF16)
        part = _nn(a, wo_ref[lo:lo + FFN_TILE_F, :])
        acc = part if acc is None else acc + part
    out = _layer_norm(ALPHA * x + 0.5 * acc, g_ref[...], b_ref[...])
    o_ref[...] = out
    for ref in o16_ref:
        ref[...] = out.astype(BF16)


def _ffn_sublayer(x2, wi, wo, g, b, layer, with_bf16_copy):
    T, D = x2.shape
    tm = TOKEN_TILE
    out_spec = pl.BlockSpec((tm, D), lambda i: (i, 0))
    dtypes = (F32, BF16) if with_bf16_copy else (F32,)
    return pl.pallas_call(
        _ffn_kernel,
        grid=(T // tm,),
        in_specs=[
            pl.BlockSpec((tm, D), lambda i: (i, 0)),
            pl.BlockSpec((None, D, 2 * D_FF), lambda i: (layer, 0, 0)),
            pl.BlockSpec((None, D_FF, D), lambda i: (layer, 0, 0)),
            pl.BlockSpec((1, D), lambda i: (0, 0)),
            pl.BlockSpec((1, D), lambda i: (0, 0)),
        ],
        out_specs=[out_spec] * len(dtypes),
        out_shape=[jax.ShapeDtypeStruct((T, D), dt) for dt in dtypes],
        compiler_params=_params(1),
        name="ffn_sublayer",
    )(x2, wi, wo, g, b)


def _tile_iotas():
    r = lax.broadcasted_iota(jnp.int32, (TK, TQ), 0)
    c = lax.broadcasted_iota(jnp.int32, (TK, TQ), 1)
    return r, c


def _scores(s_ref, n_tiles, score_tile, h):
    m = None
    for j in range(n_tiles):
        s = score_tile(h, j)
        s_ref[j * TK:(j + 1) * TK, :] = s
        mj = jnp.max(s, axis=0, keepdims=True)
        m = mj if m is None else jnp.maximum(m, mj)
    return m


def _weighted_values(s_ref, n_tiles, m, vt_cols):
    p_all = jnp.concatenate([jnp.exp2(s_ref[j * TK:(j + 1) * TK, :] - m).astype(BF16) for j in range(n_tiles)],
                            axis=0)
    res = _nn(vt_cols, p_all)
    return res[:HEAD_DIM] / jnp.maximum(res[HEAD_DIM:HEAD_DIM + 1], 1e-30)


def _attend_heads(n_tiles, score_tile, vt_cols, out_ref, s_refs):
    s_a, s_b = s_refs
    m_first = _scores(s_a, n_tiles, score_tile, 0)

    def pair(i, m_a):
        h_a, h_b = 2 * i, 2 * i + 1
        h_next = jnp.minimum(2 * i + 2, N_HEADS - 1)
        m_b = _scores(s_b, n_tiles, score_tile, h_b)
        out_ref[h_a] = _weighted_values(s_a, n_tiles, m_a, vt_cols(h_a, n_tiles))
        m_next = _scores(s_a, n_tiles, score_tile, h_next)
        out_ref[h_b] = _weighted_values(s_b, n_tiles, m_b, vt_cols(h_b, n_tiles))
        return m_next

    lax.fori_loop(0, N_HEADS // 2, pair, m_first)


def _position_columns(n_rows, first, step):
    row = lax.broadcasted_iota(jnp.int32, (n_rows, HEAD_DIM), 0)
    lane = lax.broadcasted_iota(jnp.int32, (n_rows, HEAD_DIM), 1)
    pos = first + step * row
    hi = ((pos >> 7) << 7).astype(F32)
    lo = (pos & 127).astype(F32)
    return jnp.where(lane < 6, jnp.where((lane & 1) == 0, hi, lo), 0.0)


def _bf16_pieces(value):
    hi = float(np.asarray(value, np.float32).astype(BF16))
    mid = float(np.asarray(value - hi, np.float32).astype(BF16))
    return hi, mid, value - hi - mid


def _slope_columns(n_rows, slope):
    lane = lax.broadcasted_iota(jnp.int32, (n_rows, HEAD_DIM), 1)
    hi, mid, rest = _bf16_pieces(slope * LOG2E)
    return jnp.where(lane < 2, hi, jnp.where(lane < 4, mid, jnp.where(lane < 6, rest, 0.0))).astype(F32)


def _ones_columns(n_rows, n_cols):
    lane = lax.broadcasted_iota(jnp.int32, (n_rows, HEAD_DIM), 1)
    return jnp.where(lane < n_cols, 1.0, 0.0).astype(F32)


def _half_mask(n_rows, parity):
    lane = lax.broadcasted_iota(jnp.int32, (n_rows, PAIR_WIDTH), 1)
    return (lane // HEAD_DIM) == parity


def _bias_slab(cols, parity):
    zeros = jnp.zeros_like(cols)
    return jnp.concatenate([cols, zeros] if parity == 0 else [zeros, cols], axis=1)


def _stage_operand(pair_slab, parity, bias_cols, scale=None):
    feats = pair_slab if scale is None else pair_slab * scale
    own = _half_mask(pair_slab.shape[0], parity)
    return jnp.where(own, feats, _bias_slab(bias_cols, 1 - parity)).astype(BF16)


def _split3(a):
    hi = a.astype(BF16)
    rest = a - hi.astype(F32)
    mid = rest.astype(BF16)
    return hi, mid, (rest - mid.astype(F32)).astype(BF16)


def _project(x_ref, w_ref, b_ref, wt_ref, bt_ref, rm_ref, cm_ref):
    S = x_ref.shape[1]
    for i in range(S // ROW_CHUNK):
        rows = slice(i * ROW_CHUNK, (i + 1) * ROW_CHUNK)
        xb = x_ref[0, rows, :]
        rm_ref[rows, :] = _nn(xb, w_ref[...]) + b_ref[...]
        cm_ref[:, rows] = _nt(wt_ref[...], xb) + bt_ref[...]


def _stage_vt(vt_ref, v_t):
    G, _, S = vt_ref.shape
    vt_ref[:, 0:HEAD_DIM, :] = v_t.astype(BF16).reshape(G, HEAD_DIM, S)
    row = lax.broadcasted_iota(jnp.int32, (G, VT_ROWS - HEAD_DIM, S), 1)
    vt_ref[:, HEAD_DIM:VT_ROWS, :] = jnp.where(row == 0, 1.0, 0.0).astype(BF16)


def _causal_attention(o_ref, qp_ref, kp_ref, vt_ref, ot_ref, s_refs):
    S = qp_ref.shape[1]
    r, c = _tile_iotas()
    for qi in range(S // TQ):
        q0 = qi * TQ

        def score_tile(h, j, qi=qi, q0=q0):
            s = _nt(kp_ref[h, j * TK:(j + 1) * TK, :], qp_ref[h, q0:q0 + TQ, :])
            return jnp.where(r <= c, s, NEG_INF) if j == qi else s

        _attend_heads(qi + 1, score_tile, lambda h, n: vt_ref[h, :, 0:n * TK], ot_ref, s_refs)
        _store_heads(o_ref, q0, ot_ref[...].reshape(BRANCH_WIDTH, TQ))


def _store_heads(o_ref, q0, heads_t):
    o_ref[0, q0:q0 + TQ, :] = heads_t.T


def _attention_scratch(S):
    return [pltpu.VMEM((N_HEADS, HEAD_DIM, TQ), F32), pltpu.VMEM((S, TQ), F32), pltpu.VMEM((S, TQ), F32)]


def _moba_kernel(x_ref, w_ref, b_ref, wt_ref, bt_ref, o_ref, rm_ref, cm_ref, qp_ref, kp_ref, vt_ref, sel_ref,
                 ot_ref, sa_ref, sb_ref):
    S = x_ref.shape[1]
    nb = S // MOBA_BLOCK
    _project(x_ref, w_ref, b_ref, wt_ref, bt_ref, rm_ref, cm_ref)
    _stage_vt(vt_ref, cm_ref[...])
    kcols = _position_columns(S, 0, 1)
    blk = lax.broadcasted_iota(jnp.int32, (nb, S), 0)
    own = lax.broadcasted_iota(jnp.int32, (nb, S), 1) // MOBA_BLOCK
    cand = blk < own
    pair_gates = []
    for pair in range(N_HEADS // 2):
        q_pair = rm_ref[:, pair * PAIR_WIDTH:(pair + 1) * PAIR_WIDTH]
        k_pair = rm_ref[:, BRANCH_WIDTH + pair * PAIR_WIDTH:BRANCH_WIDTH + (pair + 1) * PAIR_WIDTH]
        for parity in range(2):
            h = 2 * pair + parity
            qp_ref[h] = _stage_operand(q_pair, parity, _slope_columns(S, MOBA_SLOPES[h]), HEAD_DIM ** -0.5 * LOG2E)
            kp_ref[h] = _stage_operand(k_pair, parity, kcols)
        k_mean = jnp.sum(k_pair.reshape(nb, MOBA_BLOCK, PAIR_WIDTH), axis=1) * (1.0 / MOBA_BLOCK)
        k_means = jnp.concatenate([jnp.where(_half_mask(nb, parity), k_mean, 0.0) for parity in range(2)], axis=0)
        km_hi, km_mid, _ = _split3(k_means)
        q_hi, q_mid, _ = _split3(q_pair)
        pair_gates.append(_nt(km_hi, q_hi) + _nt(km_hi, q_mid) + _nt(km_mid, q_hi))
    for h in range(N_HEADS):
        pair, parity = divmod(h, 2)
        gate = jnp.where(cand, pair_gates[pair][parity * nb:(parity + 1) * nb, :], NEG_INF)
        beaten = jnp.zeros((nb, S), F32)
        for j in range(nb):
            other = gate[j:j + 1, :]
            wins = (other > gate) | ((other == gate) & (blk > j))
            beaten = beaten + jnp.where(wins, 1.0, 0.0)
        sel = jnp.where(cand & (beaten < MOBA_TOPK), 1.0, 0.0)
        for qi in range(S // TQ):
            sel_ref[h, qi] = sel[:, qi * TQ:(qi + 1) * TQ]

    r, c = _tile_iotas()
    for qi in range(S // TQ):
        q0 = qi * TQ

        def score_tile(h, j, qi=qi, q0=q0):
            s = _nt(kp_ref[h, j * TK:(j + 1) * TK, :], qp_ref[h, q0:q0 + TQ, :])
            return jnp.where(r <= c if j == qi else sel_ref[h, qi, j:j + 1, :] > 0.5, s, NEG_INF)

        _attend_heads(qi + 1, score_tile, lambda h, n: vt_ref[h, :, 0:n * TK], ot_ref, (sa_ref, sb_ref))
        _store_heads(o_ref, q0, ot_ref[...].reshape(BRANCH_WIDTH, TQ))


def _moba(x, w, b, wt, bt, layer):
    B, S, D = x.shape
    n_rm = w.shape[-1]
    n_cm = wt.shape[1]
    nb = S // MOBA_BLOCK
    return pl.pallas_call(
        _moba_kernel,
        grid=(B,),
        in_specs=[
            pl.BlockSpec((1, S, D), lambda i: (i, 0, 0)),
            pl.BlockSpec((None, D, n_rm), lambda i: (layer, 0, 0)),
            pl.BlockSpec((None, 1, n_rm), lambda i: (layer, 0, 0)),
            pl.BlockSpec((None, n_cm, D), lambda i: (layer, 0, 0)),
            pl.BlockSpec((None, n_cm, 1), lambda i: (layer, 0, 0)),
        ],
        out_specs=pl.BlockSpec((1, S, BRANCH_WIDTH), lambda i: (i, 0, 0)),
        out_shape=jax.ShapeDtypeStruct((B, S, BRANCH_WIDTH), F32),
        scratch_shapes=[
            pltpu.VMEM((S, n_rm), F32),
            pltpu.VMEM((n_cm, S), F32),
            pltpu.VMEM((N_HEADS, S, QK_WIDTH), BF16),
            pltpu.VMEM((N_HEADS, S, QK_WIDTH), BF16),
            pltpu.VMEM((N_HEADS, VT_ROWS, S), BF16),
            pltpu.VMEM((N_HEADS, S // TQ, nb, TQ), F32),
        ] + _attention_scratch(S),
        compiler_params=_params(1),
        name="moba_mixer",
    )(x, w, b, wt, bt)


def _fox_kernel(x_ref, w_ref, b_ref, wt_ref, bt_ref, o_ref, rm_ref, cm_ref, qp_ref, kp_ref, vt_ref,
                ot_ref, sa_ref, sb_ref):
    S = x_ref.shape[1]
    _project(x_ref, w_ref, b_ref, wt_ref, bt_ref, rm_ref, cm_ref)
    _stage_vt(vt_ref, cm_ref[0:BRANCH_WIDTH, :])
    f_t = cm_ref[BRANCH_WIDTH:BRANCH_WIDTH + FOX_F_ROWS, :]
    log_sig = jnp.minimum(f_t, 0.0) - jnp.log1p(jnp.exp(-jnp.abs(f_t)))
    blk = MOBA_BLOCK
    tri = jnp.where(lax.broadcasted_iota(jnp.int32, (blk, blk), 0) <= lax.broadcasted_iota(jnp.int32, (blk, blk), 1),
                    1.0, 0.0).astype(BF16)
    stacked = jnp.concatenate([log_sig[:, i * blk:(i + 1) * blk] for i in range(S // blk)], axis=0)
    within = functools.reduce(jnp.add, [_nn(piece, tri) for piece in _split3(stacked)])
    carry = jnp.zeros((FOX_F_ROWS, 1), F32)
    pieces = []
    for i in range(S // blk):
        cs = within[i * FOX_F_ROWS:(i + 1) * FOX_F_ROWS, :] + carry
        pieces.append(cs)
        carry = cs[:, blk - 1:blk]
    key_bias_t = jnp.concatenate(pieces, axis=1) * (-LOG2E)
    key_bias = jnp.concatenate([key_bias_t, jnp.zeros((LANES - FOX_F_ROWS, S), F32)], axis=0).T
    lane = lax.broadcasted_iota(jnp.int32, (S, HEAD_DIM), 1)
    ones = _ones_columns(S, 3)
    for h in range(N_HEADS):
        pair, parity = divmod(h, 2)
        q_pair = rm_ref[:, pair * PAIR_WIDTH:(pair + 1) * PAIR_WIDTH]
        k_pair = rm_ref[:, BRANCH_WIDTH + pair * PAIR_WIDTH:BRANCH_WIDTH + (pair + 1) * PAIR_WIDTH]
        d = key_bias[:, h:h + 1]
        d_hi = d.astype(BF16).astype(F32)
        d_mid = (d - d_hi).astype(BF16).astype(F32)
        d_lo = d - d_hi - d_mid
        dcols = jnp.where(lane == 0, d_hi, jnp.where(lane == 1, d_mid, jnp.where(lane == 2, d_lo, 0.0)))
        qp_ref[h] = _stage_operand(q_pair, parity, ones, HEAD_DIM ** -0.5 * LOG2E)
        kp_ref[h] = _stage_operand(k_pair, parity, dcols)

    _causal_attention(o_ref, qp_ref, kp_ref, vt_ref, ot_ref, (sa_ref, sb_ref))


def _fox(x, w, b, wt, bt, layer):
    B, S, D = x.shape
    n_rm = w.shape[-1]
    n_cm = wt.shape[1]
    return pl.pallas_call(
        _fox_kernel,
        grid=(B,),
        in_specs=[
            pl.BlockSpec((1, S, D), lambda i: (i, 0, 0)),
            pl.BlockSpec((None, D, n_rm), lambda i: (layer, 0, 0)),
            pl.BlockSpec((None, 1, n_rm), lambda i: (layer, 0, 0)),
            pl.BlockSpec((None, n_cm, D), lambda i: (layer, 0, 0)),
            pl.BlockSpec((None, n_cm, 1), lambda i: (layer, 0, 0)),
        ],
        out_specs=pl.BlockSpec((1, S, BRANCH_WIDTH), lambda i: (i, 0, 0)),
        out_shape=jax.ShapeDtypeStruct((B, S, BRANCH_WIDTH), F32),
        scratch_shapes=[
            pltpu.VMEM((S, n_rm), F32),
            pltpu.VMEM((n_cm, S), F32),
            pltpu.VMEM((N_HEADS, S, QK_WIDTH), BF16),
            pltpu.VMEM((N_HEADS, S, QK_WIDTH), BF16),
            pltpu.VMEM((N_HEADS, VT_ROWS, S), BF16),
        ] + _attention_scratch(S),
        compiler_params=_params(1),
        name="fox_mixer",
    )(x, w, b, wt, bt)


def _rope_table_kernel(pos_ref, freq_ref, sign_ref, cos_ref, sin_ref):
    ang = pos_ref[0].astype(F32) * freq_ref[...]
    cos_ref[0] = jnp.cos(ang)
    sin_ref[0] = jnp.sin(ang) * sign_ref[...]


def _rope_tables(positions, freq_row, sign_row):
    B, S = positions.shape
    return pl.pallas_call(
        _rope_table_kernel,
        grid=(B,),
        in_specs=[
            pl.BlockSpec((1, S, 1), lambda i: (i, 0, 0)),
            pl.BlockSpec((1, LANES), lambda i: (0, 0)),
            pl.BlockSpec((1, LANES), lambda i: (0, 0)),
        ],
        out_specs=[pl.BlockSpec((1, S, LANES), lambda i: (i, 0, 0))] * 2,
        out_shape=[jax.ShapeDtypeStruct((B, S, LANES), F32)] * 2,
        compiler_params=_params(1),
        name="rope_tables",
    )(positions.reshape(B, S, 1), freq_row, sign_row)


def _rms_norm(x, g):
    return x * lax.rsqrt(jnp.mean(x * x, -1, keepdims=True) + RMS_EPS) * g


def _mla_kernel(x_ref, cos_ref, sin_ref, w_ref, b_ref, qn_ref, kvn_ref, wq_ref, wqs_ref, wk_ref, wvt_ref,
                o_ref, rm_ref, qp_ref, kp_ref, vt_ref, ot_ref, sa_ref, sb_ref):
    S = x_ref.shape[1]
    for i in range(S // ROW_CHUNK):
        rows = slice(i * ROW_CHUNK, (i + 1) * ROW_CHUNK)
        rm_ref[rows, :] = _nn(x_ref[0, rows, :], w_ref[...]) + b_ref[...]
    cos = cos_ref[0]
    sin = sin_ref[0]
    scale = (MLA_NOPE + MLA_ROPE) ** -0.5 * LOG2E
    q_lat = _rms_norm(rm_ref[:, 0:MLA_Q_RANK], qn_ref[...]).astype(BF16)
    kv_lat = _rms_norm(rm_ref[:, MLA_Q_RANK:MLA_Q_RANK + MLA_KV_RANK], kvn_ref[...]).astype(BF16)
    c0 = MLA_Q_RANK + MLA_KV_RANK
    kr = rm_ref[:, c0:c0 + LANES]
    k_rope = kr * cos + pltpu.roll(kr, LANES - MLA_ROPE, axis=1) * sin
    _stage_vt(vt_ref, _nt(wvt_ref[...], kv_lat))
    pairs = range(N_HEADS // 2)
    q_plain = [_nn(q_lat, wq_ref[p]) for p in pairs]
    q_twin = [_nn(q_lat, wqs_ref[p]) for p in pairs]
    k_nope = [_nn(kv_lat, wk_ref[p]) for p in pairs]
    for h in range(N_HEADS):
        p, lanes = h // 2, slice((h % 2) * QK_WIDTH, (h % 2 + 1) * QK_WIDTH)
        qp_ref[h] = ((q_plain[p][:, lanes] * cos + q_twin[p][:, lanes] * sin) * scale).astype(BF16)
        kp_ref[h] = (k_nope[p][:, lanes] + k_rope).astype(BF16)

    _causal_attention(o_ref, qp_ref, kp_ref, vt_ref, ot_ref, (sa_ref, sb_ref))


def _mla(x, cos, sin, w, b, qn, kvn, wq, wqs, wk, wvt, layer):
    B, S, D = x.shape
    n_rm = w.shape[-1]
    lay3 = lambda i: (layer, 0, 0)
    lay4 = lambda i: (layer, 0, 0, 0)
    return pl.pallas_call(
        _mla_kernel,
        grid=(B,),
        in_specs=[
            pl.BlockSpec((1, S, D), lambda i: (i, 0, 0)),
            pl.BlockSpec((1, S, LANES), lambda i: (i, 0, 0)),
            pl.BlockSpec((1, S, LANES), lambda i: (i, 0, 0)),
            pl.BlockSpec((None, D, n_rm), lay3),
            pl.BlockSpec((None, 1, n_rm), lay3),
            pl.BlockSpec((None, 1, MLA_Q_RANK), lay3),
            pl.BlockSpec((None, 1, MLA_KV_RANK), lay3),
            pl.BlockSpec((None, N_HEADS // 2, MLA_Q_RANK, 2 * QK_WIDTH), lay4),
            pl.BlockSpec((None, N_HEADS // 2, MLA_Q_RANK, 2 * QK_WIDTH), lay4),
            pl.BlockSpec((None, N_HEADS // 2, MLA_KV_RANK, 2 * QK_WIDTH), lay4),
            pl.BlockSpec((None, BRANCH_WIDTH, MLA_KV_RANK), lay3),
        ],
        out_specs=pl.BlockSpec((1, S, BRANCH_WIDTH), lambda i: (i, 0, 0)),
        out_shape=jax.ShapeDtypeStruct((B, S, BRANCH_WIDTH), F32),
        scratch_shapes=[
            pltpu.VMEM((S, n_rm), F32),
            pltpu.VMEM((N_HEADS, S, QK_WIDTH), BF16),
            pltpu.VMEM((N_HEADS, S, QK_WIDTH), BF16),
            pltpu.VMEM((N_HEADS, VT_ROWS, S), BF16),
        ] + _attention_scratch(S),
        compiler_params=_params(1),
        name="mla_mixer",
    )(x, cos, sin, w, b, qn, kvn, wq, wqs, wk, wvt)


N_CMP_PAD = 128


def _nsa_compress(src_ref, pos_ref, w1_ref, w2_ref):
    accs = [jnp.zeros((N_CMP_PAD, NSA_CMP_HIDDEN), F32) for _ in range(2)]
    n_cmp = N_CMP_PAD - 1
    for p in range(NSA_CMP_LEN):
        rows = src_ref[pl.ds(p, n_cmp, stride=NSA_CMP_STRIDE), :]
        for which in range(2):
            part = rows[:, which * HEAD_DIM:(which + 1) * HEAD_DIM] + pos_ref[which, p:p + 1, :]
            part = jnp.concatenate([part, jnp.zeros((1, HEAD_DIM), F32)], axis=0)
            accs[which] = accs[which] + _nn(part.astype(BF16), w1_ref[which, p])
    return [_nn(jax.nn.gelu(accs[which]).astype(BF16), w2_ref[which]) for which in range(2)]


def _nsa_kernel(x_ref, w_ref, b_ref, wt_ref, bt_ref, pos_ref, w1_ref, w2_ref, o_ref,
                rm_ref, cm_ref, cin_ref, qp_ref, ks_ref, kw_ref, kc_ref, vt_ref, vtc_ref, sel_ref, gate_ref,
                oslc_ref, owin_ref, sa_ref, sb_ref):
    S = x_ref.shape[1]
    n_slc = S // NSA_SLC_BLOCK
    _project(x_ref, w_ref, b_ref, wt_ref, bt_ref, rm_ref, cm_ref)
    _stage_vt(vt_ref, cm_ref[0:2 * HEAD_DIM, :])
    gate_ref[...] = jax.nn.sigmoid(cm_ref[2 * HEAD_DIM:2 * HEAD_DIM + NSA_GATE_ROWS, :])
    c_kk, c_kc = BRANCH_WIDTH, BRANCH_WIDTH + PAIR_WIDTH
    kcols = _position_columns(S, 0, 1)
    for h in range(N_HEADS):
        pair, parity = divmod(h, 2)
        qp_ref[h] = _stage_operand(rm_ref[:, pair * PAIR_WIDTH:(pair + 1) * PAIR_WIDTH], parity,
                                   _slope_columns(S, NSA_SLOPES[h]), HEAD_DIM ** -0.5 * LOG2E)
    slab = rm_ref[:, c_kk:c_kk + PAIR_WIDTH]
    swapped = pltpu.roll(slab, HEAD_DIM, axis=1)
    ks_ref[0] = _stage_operand(slab, 0, kcols)
    ks_ref[1] = _stage_operand(swapped, 1, kcols)
    kw_ref[0] = _stage_operand(swapped, 0, kcols)
    kw_ref[1] = _stage_operand(slab, 1, kcols)
    cin_ref[...] = rm_ref[:, c_kc:c_kc + 2 * HEAD_DIM]
    k_cmp, v_cmp = _nsa_compress(cin_ref, pos_ref, w1_ref, w2_ref)
    kv_cmp = jnp.concatenate([k_cmp, v_cmp], axis=1)
    ccols = _position_columns(N_CMP_PAD, NSA_CMP_LEN - 1, NSA_CMP_STRIDE)
    kc_ref[0] = _stage_operand(kv_cmp, 0, ccols)
    kc_ref[1] = _stage_operand(pltpu.roll(kv_cmp, HEAD_DIM, axis=1), 1, ccols)
    vtc_ref[...] = kv_cmp.T[HEAD_DIM:, :].astype(BF16)

    jj = lax.broadcasted_iota(jnp.int32, (n_slc, N_CMP_PAD), 0) * NSA_SLC_BLOCK
    cc = lax.broadcasted_iota(jnp.int32, (n_slc, N_CMP_PAD), 1) * NSA_CMP_STRIDE
    overlap_t = jnp.where((cc < jj + NSA_SLC_BLOCK) & (cc + NSA_CMP_LEN > jj), 1.0, 0.0).astype(BF16)

    r, c = _tile_iotas()
    per_tile = TK // NSA_SLC_BLOCK
    win_tiles = NSA_WINDOW // TK
    c_idx = lax.broadcasted_iota(jnp.int32, (N_CMP_PAD, TQ), 0)
    lane_cmp = lax.broadcasted_iota(jnp.int32, (N_CMP_PAD, TQ), 1)
    for qi in range(S // TQ):
        q0 = qi * TQ
        cmp_ok = (c_idx * NSA_CMP_STRIDE + (NSA_CMP_LEN - 1) <= q0 + lane_cmp) & (c_idx < N_CMP_PAD - 1)
        heads = range(N_HEADS)
        ss = [jnp.where(cmp_ok, _nt(kc_ref[h % 2], qp_ref[h, q0:q0 + TQ, :]), NEG_INF) for h in heads]
        es = [jnp.where(cmp_ok, jnp.exp2(s - jnp.max(s, axis=0, keepdims=True)), 0.0) for s in ss]
        ps = [e / jnp.maximum(jnp.sum(e, axis=0, keepdims=True), 1e-30) for e in es]
        o_cmp = [_nn(vtc_ref[...], p.astype(BF16)) for p in ps]
        imp = functools.reduce(jnp.add, [_nn(overlap_t, piece) for piece in _split3(functools.reduce(jnp.add, ps))])
        n_seen = per_tile * (qi + 1)
        select = n_seen > NSA_TOP_N
        if select:
            j_idx = lax.broadcasted_iota(jnp.int32, (n_seen, TQ), 0)
            own = (q0 + lax.broadcasted_iota(jnp.int32, (n_seen, TQ), 1)) // NSA_SLC_BLOCK
            forced = (j_idx == 0) | (j_idx == own) | (j_idx == own - 1)
            score = jnp.where(j_idx <= own, imp[:n_seen] + NSA_FORCE_BONUS * jnp.where(forced, 1.0, 0.0), NEG_INF)
            beaten = jnp.zeros((n_seen, TQ), F32)
            for j in range(n_seen):
                other = score[j:j + 1, :]
                wins = (other > score) | ((other == score) & (j_idx > j))
                beaten = beaten + jnp.where(wins, 1.0, 0.0)
            sel_ref[0:n_seen, :] = jnp.where(beaten < NSA_TOP_N, 1.0, 0.0)

        def slc_tile(h, j, qi=qi, q0=q0, select=select):
            s = _nt(ks_ref[h % 2, j * TK:(j + 1) * TK, :], qp_ref[h, q0:q0 + TQ, :])
            mask = r <= c if j == qi else None
            if select:
                rows = [jnp.broadcast_to(sel_ref[per_tile * j + u:per_tile * j + u + 1, :], (NSA_SLC_BLOCK, TQ))
                        for u in range(per_tile)]
                chosen = jnp.concatenate(rows, axis=0) > 0.5
                mask = chosen if mask is None else mask & chosen
            return s if mask is None else jnp.where(mask, s, NEG_INF)

        _attend_heads(qi + 1, slc_tile, lambda h, n: vt_ref[0, :, 0:n * TK], oslc_ref, (sa_ref, sb_ref))

        lo = max(qi - win_tiles, 0)

        def win_tile(h, j, qi=qi, q0=q0, lo=lo):
            t = lo + j
            s = _nt(kw_ref[h % 2, t * TK:(t + 1) * TK, :], qp_ref[h, q0:q0 + TQ, :])
            if t == qi:
                return jnp.where(r <= c, s, NEG_INF)
            return jnp.where(c < r, s, NEG_INF) if t == qi - win_tiles else s

        _attend_heads(qi - lo + 1, win_tile, lambda h, n, lo=lo: vt_ref[1, :, lo * TK:(lo + n) * TK], owin_ref,
                      (sa_ref, sb_ref))

        gates = gate_ref[:, q0:q0 + TQ]
        outs = [gates[3 * h:3 * h + 1, :] * o_cmp[h] + gates[3 * h + 1:3 * h + 2, :] * oslc_ref[h]
                + gates[3 * h + 2:3 * h + 3, :] * owin_ref[h] for h in range(N_HEADS)]
        _store_heads(o_ref, q0, jnp.concatenate(outs, axis=0))


def _nsa(x, w, b, wt, bt, pos, w1, w2, layer):
    B, S, D = x.shape
    n_rm = w.shape[-1]
    n_cm = wt.shape[1]
    lay3 = lambda i: (layer, 0, 0)
    lay4 = lambda i: (layer, 0, 0, 0)
    lay5 = lambda i: (layer, 0, 0, 0, 0)
    return pl.pallas_call(
        _nsa_kernel,
        grid=(B,),
        in_specs=[
            pl.BlockSpec((1, S, D), lambda i: (i, 0, 0)),
            pl.BlockSpec((None, D, n_rm), lay3),
            pl.BlockSpec((None, 1, n_rm), lay3),
            pl.BlockSpec((None, n_cm, D), lay3),
            pl.BlockSpec((None, n_cm, 1), lay3),
            pl.BlockSpec((None, 2, NSA_CMP_LEN, HEAD_DIM), lay4),
            pl.BlockSpec((None, 2, NSA_CMP_LEN, HEAD_DIM, NSA_CMP_HIDDEN), lay5),
            pl.BlockSpec((None, 2, NSA_CMP_HIDDEN, HEAD_DIM), lay4),
        ],
        out_specs=pl.BlockSpec((1, S, BRANCH_WIDTH), lambda i: (i, 0, 0)),
        out_shape=jax.ShapeDtypeStruct((B, S, BRANCH_WIDTH), F32),
        scratch_shapes=[
            pltpu.VMEM((S, n_rm), F32),
            pltpu.VMEM((n_cm, S), F32),
            pltpu.VMEM((S, 2 * HEAD_DIM), F32),
            pltpu.VMEM((N_HEADS, S, QK_WIDTH), BF16),
            pltpu.VMEM((2, S, QK_WIDTH), BF16),
            pltpu.VMEM((2, S, QK_WIDTH), BF16),
            pltpu.VMEM((2, N_CMP_PAD, QK_WIDTH), BF16),
            pltpu.VMEM((2, VT_ROWS, S), BF16),
            pltpu.VMEM((HEAD_DIM, N_CMP_PAD), BF16),
            pltpu.VMEM((S // NSA_SLC_BLOCK, TQ), F32),
            pltpu.VMEM((NSA_GATE_ROWS, S), F32),
            pltpu.VMEM((N_HEADS, HEAD_DIM, TQ), F32),
        ] + _attention_scratch(S),
        compiler_params=_params(1),
        name="nsa_mixer",
    )(x, w, b, wt, bt, pos, w1, w2)


def _merge_kernel(x_ref, oa_ref, ob_ref, oc_ref, od_ref, wg_ref, bg_ref, wb_ref, wo_ref, g_ref, b_ref, o_ref):
    x = x_ref[...]
    xb = x.astype(BF16)
    D = x.shape[-1]
    acc = jnp.zeros(x.shape, F32)
    for n, branch in enumerate((oa_ref, ob_ref, oc_ref, od_ref)):
        gate = jax.nn.sigmoid(_nn(xb, wg_ref[:, n * D:(n + 1) * D]) + bg_ref[:, n * D:(n + 1) * D])
        acc = acc + gate * _nn(branch[...].astype(BF16), wb_ref[n])
    mix = _nn(acc.astype(BF16), wo_ref[...])
    o_ref[...] = _layer_norm(ALPHA * x + mix, g_ref[...], b_ref[...])


def _merge_sublayer(x2, branches, wg, bg, wb, wo, g, b, layer):
    T, D = x2.shape
    tm = TOKEN_TILE
    lay3 = lambda i: (layer, 0, 0)
    return pl.pallas_call(
        _merge_kernel,
        grid=(T // tm,),
        in_specs=[pl.BlockSpec((tm, D), lambda i: (i, 0))]
        + [pl.BlockSpec((tm, BRANCH_WIDTH), lambda i: (i, 0))] * N_BRANCH
        + [
            pl.BlockSpec((None, D, N_BRANCH * D), lay3),
            pl.BlockSpec((None, 1, N_BRANCH * D), lay3),
            pl.BlockSpec((None, N_BRANCH, BRANCH_WIDTH, D), lambda i: (layer, 0, 0, 0)),
            pl.BlockSpec((None, D, D), lay3),
            pl.BlockSpec((1, D), lambda i: (0, 0)),
            pl.BlockSpec((1, D), lambda i: (0, 0)),
        ],
        out_specs=pl.BlockSpec((tm, D), lambda i: (i, 0)),
        out_shape=jax.ShapeDtypeStruct((T, D), F32),
        compiler_params=_params(1),
        name="merge_sublayer",
    )(x2, *branches, wg, bg, wb, wo, g, b)


def _mem_kv_kernel(mem_ref, wk_ref, wvt_ref, kp_ref, vt_ref):
    mem = mem_ref[0].astype(BF16)
    k = _nn(mem, wk_ref[...])
    no_bias = jnp.zeros((k.shape[0], HEAD_DIM), F32)
    for h in range(N_HEADS):
        pair, parity = divmod(h, 2)
        kp_ref[0, h] = _stage_operand(k[:, pair * PAIR_WIDTH:(pair + 1) * PAIR_WIDTH], parity, no_bias)
    _stage_vt(vt_ref.at[0], _nt(wvt_ref[...], mem))


def _mem_kv(mem, w_k, w_vt, layer):
    B, M, D = mem.shape
    lay3 = lambda i: (layer, 0, 0)
    return pl.pallas_call(
        _mem_kv_kernel,
        grid=(B,),
        in_specs=[pl.BlockSpec((1, M, D), lambda i: (i, 0, 0)),
                  pl.BlockSpec((None, D, BRANCH_WIDTH), lay3),
                  pl.BlockSpec((None, BRANCH_WIDTH, D), lay3)],
        out_specs=[pl.BlockSpec((1, N_HEADS, M, QK_WIDTH), lambda i: (i, 0, 0, 0)),
                   pl.BlockSpec((1, N_HEADS, VT_ROWS, M), lambda i: (i, 0, 0, 0))],
        out_shape=[jax.ShapeDtypeStruct((B, N_HEADS, M, QK_WIDTH), BF16),
                   jax.ShapeDtypeStruct((B, N_HEADS, VT_ROWS, M), BF16)],
        compiler_params=_params(1),
        name="mem_kv",
    )(mem, w_k, w_vt)


def _xattn_kernel(x_ref, kp_ref, vt_ref, wq_ref, wo_ref, g_ref, b_ref, o_ref):
    rows = x_ref.shape[1] // XATTN_SPLIT
    xs = [x_ref[0, i * rows:(i + 1) * rows, :] for i in range(XATTN_SPLIT)]
    qs = [_nn(x.astype(BF16), wq_ref[...]) * (HEAD_DIM ** -0.5 * LOG2E) for x in xs]
    no_bias = jnp.zeros((rows, HEAD_DIM), F32)

    def logits_t(q, h):
        pair, parity = divmod(h, 2)
        return _nt(kp_ref[0, h], _stage_operand(q[:, pair * PAIR_WIDTH:(pair + 1) * PAIR_WIDTH], parity, no_bias))

    atts = []
    for q in qs:
        heads, s = [], logits_t(q, 0)
        for h in range(N_HEADS):
            s_next = logits_t(q, h + 1) if h + 1 < N_HEADS else None
            p = jnp.exp2(s - jnp.max(s, axis=0, keepdims=True)).astype(BF16)
            res = _nn(vt_ref[0, h], p)
            heads.append(res[:HEAD_DIM] / res[HEAD_DIM:HEAD_DIM + 1])
            s = s_next
        atts.append(jnp.concatenate(heads, axis=0).T.astype(BF16))
    outs = [_layer_norm(ALPHA * x + _nn(att, wo_ref[...]), g_ref[...], b_ref[...]) for x, att in zip(xs, atts)]
    o_ref[0] = jnp.concatenate(outs, axis=0)


def _xattn_sublayer(x, kp, vt, wq, wo, g, b, layer):
    B, S, D = x.shape
    M = kp.shape[2]
    tm = TOKEN_TILE
    lay3 = lambda i, j: (layer, 0, 0)
    return pl.pallas_call(
        _xattn_kernel,
        grid=(B, S // tm),
        in_specs=[
            pl.BlockSpec((1, tm, D), lambda i, j: (i, j, 0)),
            pl.BlockSpec((1, N_HEADS, M, QK_WIDTH), lambda i, j: (i, 0, 0, 0)),
            pl.BlockSpec((1, N_HEADS, VT_ROWS, M), lambda i, j: (i, 0, 0, 0)),
            pl.BlockSpec((None, D, BRANCH_WIDTH), lay3),
            pl.BlockSpec((None, BRANCH_WIDTH, D), lay3),
            pl.BlockSpec((1, D), lambda i, j: (0, 0)),
            pl.BlockSpec((1, D), lambda i, j: (0, 0)),
        ],
        out_specs=pl.BlockSpec((1, tm, D), lambda i, j: (i, j, 0)),
        out_shape=jax.ShapeDtypeStruct((B, S, D), F32),
        compiler_params=_params(2),
        name="xattn_sublayer",
    )(x, kp, vt, wq, wo, g, b)


def _pad_cols(w, n):
    return jnp.pad(w, [(0, 0)] * (w.ndim - 1) + [(0, n - w.shape[-1])])


def _split_w_in(w_in, b_in):
    hd, bw = HEAD_DIM, BRANCH_WIDTH
    sizes = (bw, bw, bw, bw, hd, hd, hd, hd, hd, hd, 3 * N_HEADS, bw, bw, bw, N_HEADS,
             MLA_Q_RANK, MLA_KV_RANK, MLA_ROPE, N_BRANCH * D_MODEL)
    names = ("a_q", "a_k", "a_v", "b_q", "b_kc", "b_vc", "b_ks", "b_vs", "b_kw", "b_vw", "b_g",
             "c_q", "c_k", "c_v", "c_f", "d_cq", "d_ckv", "d_kr", "g_merge")
    w, b, off = {}, {}, 0
    for name, size in zip(names, sizes):
        w[name] = w_in[:, :, off:off + size]
        b[name] = b_in[:, off:off + size]
        off += size
    return w, b


def _row_major(ws, bs, names, pad_to):
    w = _pad_cols(jnp.concatenate([ws[n] for n in names], axis=-1), pad_to).astype(BF16)
    b = _pad_cols(jnp.concatenate([bs[n] for n in names], axis=-1), pad_to)[:, None, :]
    return w, b


def _channel_major(ws, bs, names, pad_to):
    w = _pad_cols(jnp.concatenate([ws[n] for n in names], axis=-1), pad_to).transpose(0, 2, 1).astype(BF16)
    b = _pad_cols(jnp.concatenate([bs[n] for n in names], axis=-1), pad_to)[:, :, None]
    return w, b


def _rope_slab(a, swap):
    half = MLA_ROPE // 2
    first, second = a[..., :half], a[..., half:]
    body = jnp.concatenate([second, first] if swap else [first, second], axis=-1)
    return jnp.pad(body, [(0, 0)] * (a.ndim - 1) + [(MLA_NOPE, LANES - MLA_NOPE - MLA_ROPE)])


def _mla_weights(ws, bs, mla_w_uq, mla_w_ukv):
    L = mla_w_uq.shape[0]
    half = MLA_ROPE // 2
    def key_slab(a):
        return jnp.concatenate([_rope_slab(a, False)[..., :MLA_NOPE + MLA_ROPE], _rope_slab(a, True)[..., MLA_NOPE:MLA_NOPE + MLA_ROPE]],
                               axis=-1)

    w = jnp.concatenate([ws["d_cq"], ws["d_ckv"], key_slab(ws["d_kr"])], axis=-1).astype(BF16)
    b = jnp.concatenate([bs["d_cq"], bs["d_ckv"], key_slab(bs["d_kr"])], axis=-1)[:, None, :]
    uq = mla_w_uq.reshape(L, MLA_Q_RANK, N_HEADS, MLA_NOPE + MLA_ROPE).transpose(0, 2, 1, 3)
    wq = jnp.concatenate([uq[..., :MLA_NOPE], _rope_slab(uq[..., MLA_NOPE:], False)[..., MLA_NOPE:]],
                         axis=-1).astype(BF16)
    wqs = jnp.concatenate([jnp.zeros_like(uq[..., :MLA_NOPE]), _rope_slab(uq[..., MLA_NOPE:], True)[..., MLA_NOPE:]],
                          axis=-1).astype(BF16)
    ukv = mla_w_ukv.reshape(L, MLA_KV_RANK, N_HEADS, MLA_NOPE + HEAD_DIM).transpose(0, 2, 1, 3)
    wk = _pad_cols(ukv[..., :MLA_NOPE], QK_WIDTH).astype(BF16)
    wvt = ukv[..., MLA_NOPE:].transpose(0, 1, 3, 2).reshape(L, BRANCH_WIDTH, MLA_KV_RANK).astype(BF16)
    inv_freq = ROPE_BASE ** (-jnp.arange(0, MLA_ROPE, 2, dtype=F32) / MLA_ROPE)
    freq_row = _rope_slab(jnp.concatenate([inv_freq, inv_freq])[None, :], False)
    sign_row = _rope_slab(jnp.concatenate([-jnp.ones((half,), F32), jnp.ones((half,), F32)])[None, :], False)
    def by_pair(a):
        return jnp.concatenate([a[:, 0::2], a[:, 1::2]], axis=-1)

    return w, b, by_pair(wq), by_pair(wqs), by_pair(wk), wvt, freq_row, sign_row


def kernel(x, mem, positions, ln_g, ln_b, ffn1_w_in, ffn1_w_out, ffn2_w_in, ffn2_w_out, w_in, b_in, w_branch, w_o,
           mla_q_norm, mla_w_uq, mla_kv_norm, mla_w_ukv, nsa_cmp_pos, nsa_cmp_w1, nsa_cmp_w2, xa_w_q, xa_w_kv, xa_w_o):
    B, S, D = x.shape
    L = ln_g.shape[0]
    ffn1 = (ffn1_w_in.astype(BF16), ffn1_w_out.astype(BF16))
    ffn2 = (ffn2_w_in.astype(BF16), ffn2_w_out.astype(BF16))

    ws, bs = _split_w_in(w_in.astype(BF16), b_in)
    moba_w, moba_b = _row_major(ws, bs, ("a_q", "a_k"), 2 * BRANCH_WIDTH)
    moba_wt, moba_bt = _channel_major(ws, bs, ("a_v",), BRANCH_WIDTH)
    nsa_w, nsa_b = _row_major(ws, bs, ("b_q", "b_ks", "b_kw", "b_kc", "b_vc"), 2 * BRANCH_WIDTH)
    nsa_wt, nsa_bt = _channel_major(ws, bs, ("b_vs", "b_vw", "b_g"), 2 * HEAD_DIM + NSA_GATE_ROWS)
    fox_w, fox_b = _row_major(ws, bs, ("c_q", "c_k"), 2 * BRANCH_WIDTH)
    fox_wt, fox_bt = _channel_major(ws, bs, ("c_v", "c_f"), BRANCH_WIDTH + 2 * FOX_F_ROWS)

    mla_w, mla_b, mla_wq, mla_wqs, mla_wk, mla_wvt, freq_row, sign_row = _mla_weights(ws, bs, mla_w_uq, mla_w_ukv)
    cos, sin = _rope_tables(positions, freq_row, sign_row)

    merge_wg = ws["g_merge"].astype(BF16)
    merge_bg = bs["g_merge"][:, None, :]
    merge_wb = w_branch.astype(BF16)
    merge_wo = w_o.astype(BF16)
    nsa_w1 = nsa_cmp_w1.reshape(L, 2, NSA_CMP_LEN, HEAD_DIM, NSA_CMP_HIDDEN).astype(BF16)
    nsa_w2 = nsa_cmp_w2.astype(BF16)
    xa_wq, xa_wo = xa_w_q.astype(BF16), xa_w_o.astype(BF16)
    xa_wk = xa_w_kv[:, :, :BRANCH_WIDTH].astype(BF16)
    xa_wvt = xa_w_kv[:, :, BRANCH_WIDTH:].transpose(0, 2, 1).astype(BF16)

    for l in range(L):
        ln = lambda i: (ln_g[l, i][None, :], ln_b[l, i][None, :])
        x2, x16 = _ffn_sublayer(x.reshape(B * S, D), *ffn1, *ln(0), l, True)
        x16 = x16.reshape(B, S, D)
        branches = (
            _moba(x16, moba_w, moba_b, moba_wt, moba_bt, l),
            _nsa(x16, nsa_w, nsa_b, nsa_wt, nsa_bt, nsa_cmp_pos, nsa_w1, nsa_w2, l),
            _fox(x16, fox_w, fox_b, fox_wt, fox_bt, l),
            _mla(x16, cos, sin, mla_w, mla_b, mla_q_norm[:, None, :], mla_kv_norm[:, None, :],
                 mla_wq, mla_wqs, mla_wk, mla_wvt, l),
        )
        x2 = _merge_sublayer(x2, [o.reshape(B * S, BRANCH_WIDTH) for o in branches],
                             merge_wg, merge_bg, merge_wb, merge_wo, *ln(1), l)
        x = _xattn_sublayer(x2.reshape(B, S, D), *_mem_kv(mem, xa_wk, xa_wvt, l), xa_wq, xa_wo, *ln(2), l)
        x = _ffn_sublayer(x.reshape(B * S, D), *ffn2, *ln(3), l, False)[0].reshape(B, S, D)
    return x
```

```python
import functools

import jax
import jax.numpy as jnp
import numpy as np
from jax import lax
from jax.experimental import pallas as pl
from jax.experimental.pallas import tpu as pltpu

F32 = jnp.float32
BF16 = jnp.bfloat16

D_MODEL = 1024
DEPTH = 4
HEAD_DIM = 64
N_HEADS = 4
BRANCH_WIDTH = N_HEADS * HEAD_DIM
N_BRANCH = 4
MOBA_BLOCK = 256
MOBA_TOPK = 3
NSA_CMP_LEN = 32
NSA_CMP_STRIDE = 16
NSA_CMP_HIDDEN = 256
NSA_SLC_BLOCK = 64
NSA_TOP_N = 16
NSA_WINDOW = 512
NSA_FORCE_BONUS = 1.0e4
MLA_Q_RANK = 256
MLA_KV_RANK = 128
MLA_NOPE = 64
MLA_ROPE = 32
ROPE_BASE = 10000.0
MEM_LEN = 256
D_FF = 2816
ALPHA = (2 * DEPTH) ** 0.25
LN_EPS = 1e-5
RMS_EPS = 1e-6
NEG_INF = -1e30
MOBA_SLOPES = tuple(2.0 ** -(2 * h + 1) for h in range(N_HEADS))
NSA_SLOPES = tuple(2.0 ** -(2 * h + 2) for h in range(N_HEADS))

LOG2E = 1.4426950408889634
VT_ROWS = HEAD_DIM + 16
LANES = 128
TQ = 256
TK = 256
QK_WIDTH = 128
FOX_F_ROWS = 8
NSA_GATE_ROWS = 16
PAIR_WIDTH = 2 * HEAD_DIM
ROW_CHUNK = 512
FFN_TILE_F = 256
STATIC_HEADS_MAX_TILES = 6
TOKEN_TILE = 512
XATTN_SPLIT = 2
VMEM_LIMIT = 56 * 1024 * 1024


def _nn(a, b):
    return jnp.dot(a, b, preferred_element_type=F32)


def _nt(a, b):
    return lax.dot_general(a, b, (((1,), (1,)), ((), ())), preferred_element_type=F32)


def _layer_norm(z, g, b):
    mu = jnp.mean(z, -1, keepdims=True)
    d = z - mu
    var = jnp.mean(d * d, -1, keepdims=True)
    return d * lax.rsqrt(var + LN_EPS) * g + b


def _params(n_parallel):
    return pltpu.CompilerParams(dimension_semantics=("parallel",) * n_parallel,
                                vmem_limit_bytes=VMEM_LIMIT)


def _ffn_kernel(x_ref, wi_ref, wo_ref, g_ref, b_ref, o_ref, *o16_ref):
    x = x_ref[...]
    xb = x.astype(BF16)
    acc = None
    for c in range(D_FF // FFN_TILE_F):
        lo = c * FFN_TILE_F
        hg = _nn(xb, wi_ref[:, lo:lo + FFN_TILE_F])
        hu = _nn(xb, wi_ref[:, D_FF + lo:D_FF + lo + FFN_TILE_F])
        a = (hg * jax.nn.sigmoid(hg) * hu).astype(BF16)
        part = _nn(a, wo_ref[lo:lo + FFN_TILE_F, :])
        acc = part if acc is None else acc + part
    out = _layer_norm(ALPHA * x + 0.5 * acc, g_ref[...], b_ref[...])
    o_ref[...] = out
    for ref in o16_ref:
        ref[...] = out.astype(BF16)


def _ffn_sublayer(x2, wi, wo, g, b, layer, with_bf16_copy):
    T, D = x2.shape
    tm = TOKEN_TILE
    out_spec = pl.BlockSpec((tm, D), lambda i: (i, 0))
    dtypes = (F32, BF16) if with_bf16_copy else (F32,)
    return pl.pallas_call(
        _ffn_kernel,
        grid=(T // tm,),
        in_specs=[
            pl.BlockSpec((tm, D), lambda i: (i, 0)),
            pl.BlockSpec((None, D, 2 * D_FF), lambda i: (layer, 0, 0)),
            pl.BlockSpec((None, D_FF, D), lambda i: (layer, 0, 0)),
            pl.BlockSpec((1, D), lambda i: (0, 0)),
            pl.BlockSpec((1, D), lambda i: (0, 0)),
        ],
        out_specs=[out_spec] * len(dtypes),
        out_shape=[jax.ShapeDtypeStruct((T, D), dt) for dt in dtypes],
        compiler_params=_params(1),
        name="ffn_sublayer",
    )(x2, wi, wo, g, b)


def _tile_iotas():
    r = lax.broadcasted_iota(jnp.int32, (TK, TQ), 0)
    c = lax.broadcasted_iota(jnp.int32, (TK, TQ), 1)
    return r, c


def _scores(s_ref, n_tiles, score_tile, h):
    m = None
    for j in range(n_tiles):
        s = score_tile(h, j)
        s_ref[j * TK:(j + 1) * TK, :] = s
        mj = jnp.max(s, axis=0, keepdims=True)
        m = mj if m is None else jnp.maximum(m, mj)
    return m


def _weighted_values(s_ref, n_tiles, m, vt_cols):
    p_all = jnp.concatenate([jnp.exp2(s_ref[j * TK:(j + 1) * TK, :] - m).astype(BF16) for j in range(n_tiles)],
                            axis=0)
    res = _nn(vt_cols, p_all)
    return res[:HEAD_DIM] / jnp.maximum(res[HEAD_DIM:HEAD_DIM + 1], 1e-30)


def _attend_heads(n_tiles, score_tile, vt_cols, out_ref, s_refs):
    if n_tiles <= STATIC_HEADS_MAX_TILES:
        tiles = [score_tile(0, j) for j in range(n_tiles)]
        for h in range(N_HEADS):
            ahead = [score_tile(h + 1, j) for j in range(n_tiles)] if h + 1 < N_HEADS else None
            m = functools.reduce(jnp.maximum, [jnp.max(s, axis=0, keepdims=True) for s in tiles])
            p_all = jnp.concatenate([jnp.exp2(s - m).astype(BF16) for s in tiles], axis=0)
            res = _nn(vt_cols(h, n_tiles), p_all)
            out_ref[h] = res[:HEAD_DIM] / jnp.maximum(res[HEAD_DIM:HEAD_DIM + 1], 1e-30)
            tiles = ahead
        return
    s_a, s_b = s_refs
    m_first = _scores(s_a, n_tiles, score_tile, 0)

    def pair(i, m_a):
        h_a, h_b = 2 * i, 2 * i + 1
        h_next = jnp.minimum(2 * i + 2, N_HEADS - 1)
        m_b = _scores(s_b, n_tiles, score_tile, h_b)
        out_ref[h_a] = _weighted_values(s_a, n_tiles, m_a, vt_cols(h_a, n_tiles))
        m_next = _scores(s_a, n_tiles, score_tile, h_next)
        out_ref[h_b] = _weighted_values(s_b, n_tiles, m_b, vt_cols(h_b, n_tiles))
        return m_next

    lax.fori_loop(0, N_HEADS // 2, pair, m_first)


def _position_columns(n_rows, first, step):
    row = lax.broadcasted_iota(jnp.int32, (n_rows, HEAD_DIM), 0)
    lane = lax.broadcasted_iota(jnp.int32, (n_rows, HEAD_DIM), 1)
    pos = first + step * row
    hi = ((pos >> 7) << 7).astype(F32)
    lo = (pos & 127).astype(F32)
    return jnp.where(lane < 6, jnp.where((lane & 1) == 0, hi, lo), 0.0)


def _bf16_pieces(value):
    hi = float(np.asarray(value, np.float32).astype(BF16))
    mid = float(np.asarray(value - hi, np.float32).astype(BF16))
    return hi, mid, value - hi - mid


def _slope_columns(n_rows, slope):
    lane = lax.broadcasted_iota(jnp.int32, (n_rows, HEAD_DIM), 1)
    hi, mid, rest = _bf16_pieces(slope * LOG2E)
    return jnp.where(lane < 2, hi, jnp.where(lane < 4, mid, jnp.where(lane < 6, rest, 0.0))).astype(F32)


def _ones_columns(n_rows, n_cols):
    lane = lax.broadcasted_iota(jnp.int32, (n_rows, HEAD_DIM), 1)
    return jnp.where(lane < n_cols, 1.0, 0.0).astype(F32)


def _half_mask(n_rows, parity):
    lane = lax.broadcasted_iota(jnp.int32, (n_rows, PAIR_WIDTH), 1)
    return (lane // HEAD_DIM) == parity


def _bias_slab(cols, parity):
    zeros = jnp.zeros_like(cols)
    return jnp.concatenate([cols, zeros] if parity == 0 else [zeros, cols], axis=1)


def _stage_operand(pair_slab, parity, bias_cols, scale=None):
    feats = pair_slab if scale is None else pair_slab * scale
    own = _half_mask(pair_slab.shape[0], parity)
    return jnp.where(own, feats, _bias_slab(bias_cols, 1 - parity)).astype(BF16)


def _split3(a):
    hi = a.astype(BF16)
    rest = a - hi.astype(F32)
    mid = rest.astype(BF16)
    return hi, mid, (rest - mid.astype(F32)).astype(BF16)


def _project(x_ref, w_ref, b_ref, wt_ref, bt_ref, rm_ref, cm_ref):
    S = x_ref.shape[1]
    for i in range(S // ROW_CHUNK):
        rows = slice(i * ROW_CHUNK, (i + 1) * ROW_CHUNK)
        xb = x_ref[0, rows, :]
        rm_ref[rows, :] = _nn(xb, w_ref[...]) + b_ref[...]
        cm_ref[:, rows] = _nt(wt_ref[...], xb) + bt_ref[...]


def _stage_vt(vt_ref, v_t):
    G, _, S = vt_ref.shape
    vt_ref[:, 0:HEAD_DIM, :] = v_t.astype(BF16).reshape(G, HEAD_DIM, S)
    row = lax.broadcasted_iota(jnp.int32, (G, VT_ROWS - HEAD_DIM, S), 1)
    vt_ref[:, HEAD_DIM:VT_ROWS, :] = jnp.where(row == 0, 1.0, 0.0).astype(BF16)


def _causal_attention(o_ref, qp_ref, kp_ref, vt_ref, ot_ref, s_refs):
    S = qp_ref.shape[1]
    r, c = _tile_iotas()
    for qi in range(S // TQ):
        q0 = qi * TQ

        def score_tile(h, j, qi=qi, q0=q0):
            s = _nt(kp_ref[h, j * TK:(j + 1) * TK, :], qp_ref[h, q0:q0 + TQ, :])
            return jnp.where(r <= c, s, NEG_INF) if j == qi else s

        _attend_heads(qi + 1, score_tile, lambda h, n: vt_ref[h, :, 0:n * TK], ot_ref, s_refs)
        _store_heads(o_ref, q0, ot_ref[...].reshape(BRANCH_WIDTH, TQ))


def _store_heads(o_ref, q0, heads_t):
    o_ref[0, q0:q0 + TQ, :] = heads_t.T


def _attention_scratch(S):
    return [pltpu.VMEM((N_HEADS, HEAD_DIM, TQ), F32), pltpu.VMEM((S, TQ), F32), pltpu.VMEM((S, TQ), F32)]


def _moba_kernel(x_ref, w_ref, b_ref, wt_ref, bt_ref, o_ref, rm_ref, cm_ref, qp_ref, kp_ref, vt_ref, sel_ref,
                 ot_ref, sa_ref, sb_ref):
    S = x_ref.shape[1]
    nb = S // MOBA_BLOCK
    _project(x_ref, w_ref, b_ref, wt_ref, bt_ref, rm_ref, cm_ref)
    _stage_vt(vt_ref, cm_ref[...])
    kcols = _position_columns(S, 0, 1)
    blk = lax.broadcasted_iota(jnp.int32, (nb, S), 0)
    own = lax.broadcasted_iota(jnp.int32, (nb, S), 1) // MOBA_BLOCK
    cand = blk < own
    pair_gates = []
    for pair in range(N_HEADS // 2):
        q_pair = rm_ref[:, pair * PAIR_WIDTH:(pair + 1) * PAIR_WIDTH]
        k_pair = rm_ref[:, BRANCH_WIDTH + pair * PAIR_WIDTH:BRANCH_WIDTH + (pair + 1) * PAIR_WIDTH]
        for parity in range(2):
            h = 2 * pair + parity
            qp_ref[h] = _stage_operand(q_pair, parity, _slope_columns(S, MOBA_SLOPES[h]), HEAD_DIM ** -0.5 * LOG2E)
            kp_ref[h] = _stage_operand(k_pair, parity, kcols)
        k_mean = jnp.sum(k_pair.reshape(nb, MOBA_BLOCK, PAIR_WIDTH), axis=1) * (1.0 / MOBA_BLOCK)
        k_means = jnp.concatenate([jnp.where(_half_mask(nb, parity), k_mean, 0.0) for parity in range(2)], axis=0)
        km_hi, km_mid, _ = _split3(k_means)
        q_hi, q_mid, _ = _split3(q_pair)
        pair_gates.append(_nt(km_hi, q_hi) + _nt(km_hi, q_mid) + _nt(km_mid, q_hi))
    for h in range(N_HEADS):
        pair, parity = divmod(h, 2)
        gate = jnp.where(cand, pair_gates[pair][parity * nb:(parity + 1) * nb, :], NEG_INF)
        beaten = jnp.zeros((nb, S), F32)
        for j in range(nb):
            other = gate[j:j + 1, :]
            wins = (other > gate) | ((other == gate) & (blk > j))
            beaten = beaten + jnp.where(wins, 1.0, 0.0)
        sel = jnp.where(cand & (beaten < MOBA_TOPK), 1.0, 0.0)
        for qi in range(S // TQ):
            sel_ref[h, qi] = sel[:, qi * TQ:(qi + 1) * TQ]

    r, c = _tile_iotas()
    for qi in range(S // TQ):
        q0 = qi * TQ

        def score_tile(h, j, qi=qi, q0=q0):
            s = _nt(kp_ref[h, j * TK:(j + 1) * TK, :], qp_ref[h, q0:q0 + TQ, :])
            return jnp.where(r <= c if j == qi else sel_ref[h, qi, j:j + 1, :] > 0.5, s, NEG_INF)

        _attend_heads(qi + 1, score_tile, lambda h, n: vt_ref[h, :, 0:n * TK], ot_ref, (sa_ref, sb_ref))
        _store_heads(o_ref, q0, ot_ref[...].reshape(BRANCH_WIDTH, TQ))


def _moba(x, w, b, wt, bt, layer):
    B, S, D = x.shape
    n_rm = w.shape[-1]
    n_cm = wt.shape[1]
    nb = S // MOBA_BLOCK
    return pl.pallas_call(
        _moba_kernel,
        grid=(B,),
        in_specs=[
            pl.BlockSpec((1, S, D), lambda i: (i, 0, 0)),
            pl.BlockSpec((None, D, n_rm), lambda i: (layer, 0, 0)),
            pl.BlockSpec((None, 1, n_rm), lambda i: (layer, 0, 0)),
            pl.BlockSpec((None, n_cm, D), lambda i: (layer, 0, 0)),
            pl.BlockSpec((None, n_cm, 1), lambda i: (layer, 0, 0)),
        ],
        out_specs=pl.BlockSpec((1, S, BRANCH_WIDTH), lambda i: (i, 0, 0)),
        out_shape=jax.ShapeDtypeStruct((B, S, BRANCH_WIDTH), F32),
        scratch_shapes=[
            pltpu.VMEM((S, n_rm), F32),
            pltpu.VMEM((n_cm, S), F32),
            pltpu.VMEM((N_HEADS, S, QK_WIDTH), BF16),
            pltpu.VMEM((N_HEADS, S, QK_WIDTH), BF16),
            pltpu.VMEM((N_HEADS, VT_ROWS, S), BF16),
            pltpu.VMEM((N_HEADS, S // TQ, nb, TQ), F32),
        ] + _attention_scratch(S),
        compiler_params=_params(1),
        name="moba_mixer",
    )(x, w, b, wt, bt)


def _fox_kernel(x_ref, w_ref, b_ref, wt_ref, bt_ref, o_ref, rm_ref, cm_ref, qp_ref, kp_ref, vt_ref,
                ot_ref, sa_ref, sb_ref):
    S = x_ref.shape[1]
    _project(x_ref, w_ref, b_ref, wt_ref, bt_ref, rm_ref, cm_ref)
    _stage_vt(vt_ref, cm_ref[0:BRANCH_WIDTH, :])
    f_t = cm_ref[BRANCH_WIDTH:BRANCH_WIDTH + FOX_F_ROWS, :]
    log_sig = jnp.minimum(f_t, 0.0) - jnp.log1p(jnp.exp(-jnp.abs(f_t)))
    blk = MOBA_BLOCK
    tri = jnp.where(lax.broadcasted_iota(jnp.int32, (blk, blk), 0) <= lax.broadcasted_iota(jnp.int32, (blk, blk), 1),
                    1.0, 0.0).astype(BF16)
    stacked = jnp.concatenate([log_sig[:, i * blk:(i + 1) * blk] for i in range(S // blk)], axis=0)
    within = functools.reduce(jnp.add, [_nn(piece, tri) for piece in _split3(stacked)])
    carry = jnp.zeros((FOX_F_ROWS, 1), F32)
    pieces = []
    for i in range(S // blk):
        cs = within[i * FOX_F_ROWS:(i + 1) * FOX_F_ROWS, :] + carry
        pieces.append(cs)
        carry = cs[:, blk - 1:blk]
    key_bias_t = jnp.concatenate(pieces, axis=1) * (-LOG2E)
    key_bias = jnp.concatenate([key_bias_t, jnp.zeros((LANES - FOX_F_ROWS, S), F32)], axis=0).T
    lane = lax.broadcasted_iota(jnp.int32, (S, HEAD_DIM), 1)
    ones = _ones_columns(S, 3)
    for h in range(N_HEADS):
        pair, parity = divmod(h, 2)
        q_pair = rm_ref[:, pair * PAIR_WIDTH:(pair + 1) * PAIR_WIDTH]
        k_pair = rm_ref[:, BRANCH_WIDTH + pair * PAIR_WIDTH:BRANCH_WIDTH + (pair + 1) * PAIR_WIDTH]
        d = key_bias[:, h:h + 1]
        d_hi = d.astype(BF16).astype(F32)
        d_mid = (d - d_hi).astype(BF16).astype(F32)
        d_lo = d - d_hi - d_mid
        dcols = jnp.where(lane == 0, d_hi, jnp.where(lane == 1, d_mid, jnp.where(lane == 2, d_lo, 0.0)))
        qp_ref[h] = _stage_operand(q_pair, parity, ones, HEAD_DIM ** -0.5 * LOG2E)
        kp_ref[h] = _stage_operand(k_pair, parity, dcols)

    _causal_attention(o_ref, qp_ref, kp_ref, vt_ref, ot_ref, (sa_ref, sb_ref))


def _fox(x, w, b, wt, bt, layer):
    B, S, D = x.shape
    n_rm = w.shape[-1]
    n_cm = wt.shape[1]
    return pl.pallas_call(
        _fox_kernel,
        grid=(B,),
        in_specs=[
            pl.BlockSpec((1, S, D), lambda i: (i, 0, 0)),
            pl.BlockSpec((None, D, n_rm), lambda i: (layer, 0, 0)),
            pl.BlockSpec((None, 1, n_rm), lambda i: (layer, 0, 0)),
            pl.BlockSpec((None, n_cm, D), lambda i: (layer, 0, 0)),
            pl.BlockSpec((None, n_cm, 1), lambda i: (layer, 0, 0)),
        ],
        out_specs=pl.BlockSpec((1, S, BRANCH_WIDTH), lambda i: (i, 0, 0)),
        out_shape=jax.ShapeDtypeStruct((B, S, BRANCH_WIDTH), F32),
        scratch_shapes=[
            pltpu.VMEM((S, n_rm), F32),
            pltpu.VMEM((n_cm, S), F32),
            pltpu.VMEM((N_HEADS, S, QK_WIDTH), BF16),
            pltpu.VMEM((N_HEADS, S, QK_WIDTH), BF16),
            pltpu.VMEM((N_HEADS, VT_ROWS, S), BF16),
        ] + _attention_scratch(S),
        compiler_params=_params(1),
        name="fox_mixer",
    )(x, w, b, wt, bt)


def _rope_table_kernel(pos_ref, freq_ref, sign_ref, cos_ref, sin_ref):
    ang = pos_ref[0].astype(F32) * freq_ref[...]
    cos_ref[0] = jnp.cos(ang)
    sin_ref[0] = jnp.sin(ang) * sign_ref[...]


def _rope_tables(positions, freq_row, sign_row):
    B, S = positions.shape
    return pl.pallas_call(
        _rope_table_kernel,
        grid=(B,),
        in_specs=[
            pl.BlockSpec((1, S, 1), lambda i: (i, 0, 0)),
            pl.BlockSpec((1, LANES), lambda i: (0, 0)),
            pl.BlockSpec((1, LANES), lambda i: (0, 0)),
        ],
        out_specs=[pl.BlockSpec((1, S, LANES), lambda i: (i, 0, 0))] * 2,
        out_shape=[jax.ShapeDtypeStruct((B, S, LANES), F32)] * 2,
        compiler_params=_params(1),
        name="rope_tables",
    )(positions.reshape(B, S, 1), freq_row, sign_row)


def _rms_norm(x, g):
    return x * lax.rsqrt(jnp.mean(x * x, -1, keepdims=True) + RMS_EPS) * g


def _mla_kernel(x_ref, cos_ref, sin_ref, w_ref, b_ref, qn_ref, kvn_ref, wq_ref, wqs_ref, wk_ref, wvt_ref,
                o_ref, rm_ref, qp_ref, kp_ref, vt_ref, ot_ref, sa_ref, sb_ref):
    S = x_ref.shape[1]
    for i in range(S // ROW_CHUNK):
        rows = slice(i * ROW_CHUNK, (i + 1) * ROW_CHUNK)
        rm_ref[rows, :] = _nn(x_ref[0, rows, :], w_ref[...]) + b_ref[...]
    cos = cos_ref[0]
    sin = sin_ref[0]
    scale = (MLA_NOPE + MLA_ROPE) ** -0.5 * LOG2E
    q_lat = _rms_norm(rm_ref[:, 0:MLA_Q_RANK], qn_ref[...]).astype(BF16)
    kv_lat = _rms_norm(rm_ref[:, MLA_Q_RANK:MLA_Q_RANK + MLA_KV_RANK], kvn_ref[...]).astype(BF16)
    c0 = MLA_Q_RANK + MLA_KV_RANK
    kr = rm_ref[:, c0:c0 + LANES]
    k_rope = kr * cos + pltpu.roll(kr, LANES - MLA_ROPE, axis=1) * sin
    _stage_vt(vt_ref, _nt(wvt_ref[...], kv_lat))
    pairs = range(N_HEADS // 2)
    q_plain = [_nn(q_lat, wq_ref[p]) for p in pairs]
    q_twin = [_nn(q_lat, wqs_ref[p]) for p in pairs]
    k_nope = [_nn(kv_lat, wk_ref[p]) for p in pairs]
    for h in range(N_HEADS):
        p, lanes = h // 2, slice((h % 2) * QK_WIDTH, (h % 2 + 1) * QK_WIDTH)
        qp_ref[h] = ((q_plain[p][:, lanes] * cos + q_twin[p][:, lanes] * sin) * scale).astype(BF16)
        kp_ref[h] = (k_nope[p][:, lanes] + k_rope).astype(BF16)

    _causal_attention(o_ref, qp_ref, kp_ref, vt_ref, ot_ref, (sa_ref, sb_ref))


def _mla(x, cos, sin, w, b, qn, kvn, wq, wqs, wk, wvt, layer):
    B, S, D = x.shape
    n_rm = w.shape[-1]
    lay3 = lambda i: (layer, 0, 0)
    lay4 = lambda i: (layer, 0, 0, 0)
    return pl.pallas_call(
        _mla_kernel,
        grid=(B,),
        in_specs=[
            pl.BlockSpec((1, S, D), lambda i: (i, 0, 0)),
            pl.BlockSpec((1, S, LANES), lambda i: (i, 0, 0)),
            pl.BlockSpec((1, S, LANES), lambda i: (i, 0, 0)),
            pl.BlockSpec((None, D, n_rm), lay3),
            pl.BlockSpec((None, 1, n_rm), lay3),
            pl.BlockSpec((None, 1, MLA_Q_RANK), lay3),
            pl.BlockSpec((None, 1, MLA_KV_RANK), lay3),
            pl.BlockSpec((None, N_HEADS // 2, MLA_Q_RANK, 2 * QK_WIDTH), lay4),
            pl.BlockSpec((None, N_HEADS // 2, MLA_Q_RANK, 2 * QK_WIDTH), lay4),
            pl.BlockSpec((None, N_HEADS // 2, MLA_KV_RANK, 2 * QK_WIDTH), lay4),
            pl.BlockSpec((None, BRANCH_WIDTH, MLA_KV_RANK), lay3),
        ],
        out_specs=pl.BlockSpec((1, S, BRANCH_WIDTH), lambda i: (i, 0, 0)),
        out_shape=jax.ShapeDtypeStruct((B, S, BRANCH_WIDTH), F32),
        scratch_shapes=[
            pltpu.VMEM((S, n_rm), F32),
            pltpu.VMEM((N_HEADS, S, QK_WIDTH), BF16),
            pltpu.VMEM((N_HEADS, S, QK_WIDTH), BF16),
            pltpu.VMEM((N_HEADS, VT_ROWS, S), BF16),
        ] + _attention_scratch(S),
        compiler_params=_params(1),
        name="mla_mixer",
    )(x, cos, sin, w, b, qn, kvn, wq, wqs, wk, wvt)


N_CMP_PAD = 128


def _nsa_compress(src_ref, pos_ref, w1_ref, w2_ref):
    accs = [jnp.zeros((N_CMP_PAD, NSA_CMP_HIDDEN), F32) for _ in range(2)]
    n_cmp = N_CMP_PAD - 1
    for p in range(NSA_CMP_LEN):
        rows = src_ref[pl.ds(p, n_cmp, stride=NSA_CMP_STRIDE), :]
        for which in range(2):
            part = rows[:, which * HEAD_DIM:(which + 1) * HEAD_DIM] + pos_ref[which, p:p + 1, :]
            part = jnp.concatenate([part, jnp.zeros((1, HEAD_DIM), F32)], axis=0)
            accs[which] = accs[which] + _nn(part.astype(BF16), w1_ref[which, p])
    return [_nn(jax.nn.gelu(accs[which]).astype(BF16), w2_ref[which]) for which in range(2)]


def _nsa_kernel(x_ref, w_ref, b_ref, wt_ref, bt_ref, pos_ref, w1_ref, w2_ref, o_ref,
                rm_ref, cm_ref, cin_ref, qp_ref, ks_ref, kw_ref, kc_ref, vt_ref, vtc_ref, sel_ref, gate_ref,
                oslc_ref, owin_ref, sa_ref, sb_ref):
    S = x_ref.shape[1]
    n_slc = S // NSA_SLC_BLOCK
    _project(x_ref, w_ref, b_ref, wt_ref, bt_ref, rm_ref, cm_ref)
    _stage_vt(vt_ref, cm_ref[0:2 * HEAD_DIM, :])
    gate_ref[...] = jax.nn.sigmoid(cm_ref[2 * HEAD_DIM:2 * HEAD_DIM + NSA_GATE_ROWS, :])
    c_kk, c_kc = BRANCH_WIDTH, BRANCH_WIDTH + PAIR_WIDTH
    kcols = _position_columns(S, 0, 1)
    for h in range(N_HEADS):
        pair, parity = divmod(h, 2)
        qp_ref[h] = _stage_operand(rm_ref[:, pair * PAIR_WIDTH:(pair + 1) * PAIR_WIDTH], parity,
                                   _slope_columns(S, NSA_SLOPES[h]), HEAD_DIM ** -0.5 * LOG2E)
    slab = rm_ref[:, c_kk:c_kk + PAIR_WIDTH]
    swapped = pltpu.roll(slab, HEAD_DIM, axis=1)
    ks_ref[0] = _stage_operand(slab, 0, kcols)
    ks_ref[1] = _stage_operand(swapped, 1, kcols)
    kw_ref[0] = _stage_operand(swapped, 0, kcols)
    kw_ref[1] = _stage_operand(slab, 1, kcols)
    cin_ref[...] = rm_ref[:, c_kc:c_kc + 2 * HEAD_DIM]
    k_cmp, v_cmp = _nsa_compress(cin_ref, pos_ref, w1_ref, w2_ref)
    kv_cmp = jnp.concatenate([k_cmp, v_cmp], axis=1)
    ccols = _position_columns(N_CMP_PAD, NSA_CMP_LEN - 1, NSA_CMP_STRIDE)
    kc_ref[0] = _stage_operand(kv_cmp, 0, ccols)
    kc_ref[1] = _stage_operand(pltpu.roll(kv_cmp, HEAD_DIM, axis=1), 1, ccols)
    vtc_ref[...] = kv_cmp.T[HEAD_DIM:, :].astype(BF16)

    jj = lax.broadcasted_iota(jnp.int32, (n_slc, N_CMP_PAD), 0) * NSA_SLC_BLOCK
    cc = lax.broadcasted_iota(jnp.int32, (n_slc, N_CMP_PAD), 1) * NSA_CMP_STRIDE
    overlap_t = jnp.where((cc < jj + NSA_SLC_BLOCK) & (cc + NSA_CMP_LEN > jj), 1.0, 0.0).astype(BF16)

    r, c = _tile_iotas()
    per_tile = TK // NSA_SLC_BLOCK
    win_tiles = NSA_WINDOW // TK
    c_idx = lax.broadcasted_iota(jnp.int32, (N_CMP_PAD, TQ), 0)
    lane_cmp = lax.broadcasted_iota(jnp.int32, (N_CMP_PAD, TQ), 1)
    for qi in range(S // TQ):
        q0 = qi * TQ
        cmp_ok = (c_idx * NSA_CMP_STRIDE + (NSA_CMP_LEN - 1) <= q0 + lane_cmp) & (c_idx < N_CMP_PAD - 1)
        heads = range(N_HEADS)
        ss = [jnp.where(cmp_ok, _nt(kc_ref[h % 2], qp_ref[h, q0:q0 + TQ, :]), NEG_INF) for h in heads]
        es = [jnp.where(cmp_ok, jnp.exp2(s - jnp.max(s, axis=0, keepdims=True)), 0.0) for s in ss]
        ps = [e / jnp.maximum(jnp.sum(e, axis=0, keepdims=True), 1e-30) for e in es]
        o_cmp = [_nn(vtc_ref[...], p.astype(BF16)) for p in ps]
        imp = functools.reduce(jnp.add, [_nn(overlap_t, piece) for piece in _split3(functools.reduce(jnp.add, ps))])
        n_seen = per_tile * (qi + 1)
        select = n_seen > NSA_TOP_N
        if select:
            j_idx = lax.broadcasted_iota(jnp.int32, (n_seen, TQ), 0)
            own = (q0 + lax.broadcasted_iota(jnp.int32, (n_seen, TQ), 1)) // NSA_SLC_BLOCK
            forced = (j_idx == 0) | (j_idx == own) | (j_idx == own - 1)
            score = jnp.where(j_idx <= own, imp[:n_seen] + NSA_FORCE_BONUS * jnp.where(forced, 1.0, 0.0), NEG_INF)
            beaten = jnp.zeros((n_seen, TQ), F32)
            for j in range(n_seen):
                other = score[j:j + 1, :]
                wins = (other > score) | ((other == score) & (j_idx > j))
                beaten = beaten + jnp.where(wins, 1.0, 0.0)
            sel_ref[0:n_seen, :] = jnp.where(beaten < NSA_TOP_N, 1.0, 0.0)

        def slc_tile(h, j, qi=qi, q0=q0, select=select):
            s = _nt(ks_ref[h % 2, j * TK:(j + 1) * TK, :], qp_ref[h, q0:q0 + TQ, :])
            mask = r <= c if j == qi else None
            if select:
                rows = [jnp.broadcast_to(sel_ref[per_tile * j + u:per_tile * j + u + 1, :], (NSA_SLC_BLOCK, TQ))
                        for u in range(per_tile)]
                chosen = jnp.concatenate(rows, axis=0) > 0.5
                mask = chosen if mask is None else mask & chosen
            return s if mask is None else jnp.where(mask, s, NEG_INF)

        _attend_heads(qi + 1, slc_tile, lambda h, n: vt_ref[0, :, 0:n * TK], oslc_ref, (sa_ref, sb_ref))

        lo = max(qi - win_tiles, 0)

        def win_tile(h, j, qi=qi, q0=q0, lo=lo):
            t = lo + j
            s = _nt(kw_ref[h % 2, t * TK:(t + 1) * TK, :], qp_ref[h, q0:q0 + TQ, :])
            if t == qi:
                return jnp.where(r <= c, s, NEG_INF)
            return jnp.where(c < r, s, NEG_INF) if t == qi - win_tiles else s

        _attend_heads(qi - lo + 1, win_tile, lambda h, n, lo=lo: vt_ref[1, :, lo * TK:(lo + n) * TK], owin_ref,
                      (sa_ref, sb_ref))

        gates = gate_ref[:, q0:q0 + TQ]
        outs = [gates[3 * h:3 * h + 1, :] * o_cmp[h] + gates[3 * h + 1:3 * h + 2, :] * oslc_ref[h]
                + gates[3 * h + 2:3 * h + 3, :] * owin_ref[h] for h in range(N_HEADS)]
        _store_heads(o_ref, q0, jnp.concatenate(outs, axis=0))


def _nsa(x, w, b, wt, bt, pos, w1, w2, layer):
    B, S, D = x.shape
    n_rm = w.shape[-1]
    n_cm = wt.shape[1]
    lay3 = lambda i: (layer, 0, 0)
    lay4 = lambda i: (layer, 0, 0, 0)
    lay5 = lambda i: (layer, 0, 0, 0, 0)
    return pl.pallas_call(
        _nsa_kernel,
        grid=(B,),
        in_specs=[
            pl.BlockSpec((1, S, D), lambda i: (i, 0, 0)),
            pl.BlockSpec((None, D, n_rm), lay3),
            pl.BlockSpec((None, 1, n_rm), lay3),
            pl.BlockSpec((None, n_cm, D), lay3),
            pl.BlockSpec((None, n_cm, 1), lay3),
            pl.BlockSpec((None, 2, NSA_CMP_LEN, HEAD_DIM), lay4),
            pl.BlockSpec((None, 2, NSA_CMP_LEN, HEAD_DIM, NSA_CMP_HIDDEN), lay5),
            pl.BlockSpec((None, 2, NSA_CMP_HIDDEN, HEAD_DIM), lay4),
        ],
        out_specs=pl.BlockSpec((1, S, BRANCH_WIDTH), lambda i: (i, 0, 0)),
        out_shape=jax.ShapeDtypeStruct((B, S, BRANCH_WIDTH), F32),
        scratch_shapes=[
            pltpu.VMEM((S, n_rm), F32),
            pltpu.VMEM((n_cm, S), F32),
            pltpu.VMEM((S, 2 * HEAD_DIM), F32),
            pltpu.VMEM((N_HEADS, S, QK_WIDTH), BF16),
            pltpu.VMEM((2, S, QK_WIDTH), BF16),
            pltpu.VMEM((2, S, QK_WIDTH), BF16),
            pltpu.VMEM((2, N_CMP_PAD, QK_WIDTH), BF16),
            pltpu.VMEM((2, VT_ROWS, S), BF16),
            pltpu.VMEM((HEAD_DIM, N_CMP_PAD), BF16),
            pltpu.VMEM((S // NSA_SLC_BLOCK, TQ), F32),
            pltpu.VMEM((NSA_GATE_ROWS, S), F32),
            pltpu.VMEM((N_HEADS, HEAD_DIM, TQ), F32),
        ] + _attention_scratch(S),
        compiler_params=_params(1),
        name="nsa_mixer",
    )(x, w, b, wt, bt, pos, w1, w2)


def _merge_kernel(x_ref, oa_ref, ob_ref, oc_ref, od_ref, wg_ref, bg_ref, wb_ref, wo_ref, g_ref, b_ref, o_ref):
    x = x_ref[...]
    xb = x.astype(BF16)
    D = x.shape[-1]
    acc = jnp.zeros(x.shape, F32)
    for n, branch in enumerate((oa_ref, ob_ref, oc_ref, od_ref)):
        gate = jax.nn.sigmoid(_nn(xb, wg_ref[:, n * D:(n + 1) * D]) + bg_ref[:, n * D:(n + 1) * D])
        acc = acc + gate * _nn(branch[...].astype(BF16), wb_ref[n])
    mix = _nn(acc.astype(BF16), wo_ref[...])
    o_ref[...] = _layer_norm(ALPHA * x + mix, g_ref[...], b_ref[...])


def _merge_sublayer(x2, branches, wg, bg, wb, wo, g, b, layer):
    T, D = x2.shape
    tm = TOKEN_TILE
    lay3 = lambda i: (layer, 0, 0)
    return pl.pallas_call(
        _merge_kernel,
        grid=(T // tm,),
        in_specs=[pl.BlockSpec((tm, D), lambda i: (i, 0))]
        + [pl.BlockSpec((tm, BRANCH_WIDTH), lambda i: (i, 0))] * N_BRANCH
        + [
            pl.BlockSpec((None, D, N_BRANCH * D), lay3),
            pl.BlockSpec((None, 1, N_BRANCH * D), lay3),
            pl.BlockSpec((None, N_BRANCH, BRANCH_WIDTH, D), lambda i: (layer, 0, 0, 0)),
            pl.BlockSpec((None, D, D), lay3),
            pl.BlockSpec((1, D), lambda i: (0, 0)),
            pl.BlockSpec((1, D), lambda i: (0, 0)),
        ],
        out_specs=pl.BlockSpec((tm, D), lambda i: (i, 0)),
        out_shape=jax.ShapeDtypeStruct((T, D), F32),
        compiler_params=_params(1),
        name="merge_sublayer",
    )(x2, *branches, wg, bg, wb, wo, g, b)


def _mem_kv_kernel(mem_ref, wk_ref, wvt_ref, kp_ref, vt_ref):
    mem = mem_ref[0].astype(BF16)
    k = _nn(mem, wk_ref[...])
    no_bias = jnp.zeros((k.shape[0], HEAD_DIM), F32)
    for h in range(N_HEADS):
        pair, parity = divmod(h, 2)
        kp_ref[0, h] = _stage_operand(k[:, pair * PAIR_WIDTH:(pair + 1) * PAIR_WIDTH], parity, no_bias)
    _stage_vt(vt_ref.at[0], _nt(wvt_ref[...], mem))


def _mem_kv(mem, w_k, w_vt, layer):
    B, M, D = mem.shape
    lay3 = lambda i: (layer, 0, 0)
    return pl.pallas_call(
        _mem_kv_kernel,
        grid=(B,),
        in_specs=[pl.BlockSpec((1, M, D), lambda i: (i, 0, 0)),
                  pl.BlockSpec((None, D, BRANCH_WIDTH), lay3),
                  pl.BlockSpec((None, BRANCH_WIDTH, D), lay3)],
        out_specs=[pl.BlockSpec((1, N_HEADS, M, QK_WIDTH), lambda i: (i, 0, 0, 0)),
                   pl.BlockSpec((1, N_HEADS, VT_ROWS, M), lambda i: (i, 0, 0, 0))],
        out_shape=[jax.ShapeDtypeStruct((B, N_HEADS, M, QK_WIDTH), BF16),
                   jax.ShapeDtypeStruct((B, N_HEADS, VT_ROWS, M), BF16)],
        compiler_params=_params(1),
        name="mem_kv",
    )(mem, w_k, w_vt)


def _xattn_kernel(x_ref, kp_ref, vt_ref, wq_ref, wo_ref, g_ref, b_ref, o_ref):
    rows = x_ref.shape[1] // XATTN_SPLIT
    xs = [x_ref[0, i * rows:(i + 1) * rows, :] for i in range(XATTN_SPLIT)]
    qs = [_nn(x.astype(BF16), wq_ref[...]) * (HEAD_DIM ** -0.5 * LOG2E) for x in xs]
    no_bias = jnp.zeros((rows, HEAD_DIM), F32)

    def logits_t(q, h):
        pair, parity = divmod(h, 2)
        return _nt(kp_ref[0, h], _stage_operand(q[:, pair * PAIR_WIDTH:(pair + 1) * PAIR_WIDTH], parity, no_bias))

    atts = []
    for q in qs:
        heads, s = [], logits_t(q, 0)
        for h in range(N_HEADS):
            s_next = logits_t(q, h + 1) if h + 1 < N_HEADS else None
            p = jnp.exp2(s - jnp.max(s, axis=0, keepdims=True)).astype(BF16)
            res = _nn(vt_ref[0, h], p)
            heads.append(res[:HEAD_DIM] / res[HEAD_DIM:HEAD_DIM + 1])
            s = s_next
        atts.append(jnp.concatenate(heads, axis=0).T.astype(BF16))
    outs = [_layer_norm(ALPHA * x + _nn(att, wo_ref[...]), g_ref[...], b_ref[...]) for x, att in zip(xs, atts)]
    o_ref[0] = jnp.concatenate(outs, axis=0)


def _xattn_sublayer(x, kp, vt, wq, wo, g, b, layer):
    B, S, D = x.shape
    M = kp.shape[2]
    tm = TOKEN_TILE
    lay3 = lambda i, j: (layer, 0, 0)
    return pl.pallas_call(
        _xattn_kernel,
        grid=(B, S // tm),
        in_specs=[
            pl.BlockSpec((1, tm, D), lambda i, j: (i, j, 0)),
            pl.BlockSpec((1, N_HEADS, M, QK_WIDTH), lambda i, j: (i, 0, 0, 0)),
            pl.BlockSpec((1, N_HEADS, VT_ROWS, M), lambda i, j: (i, 0, 0, 0)),
            pl.BlockSpec((None, D, BRANCH_WIDTH), lay3),
            pl.BlockSpec((None, BRANCH_WIDTH, D), lay3),
            pl.BlockSpec((1, D), lambda i, j: (0, 0)),
            pl.BlockSpec((1, D), lambda i, j: (0, 0)),
        ],
        out_specs=pl.BlockSpec((1, tm, D), lambda i, j: (i, j, 0)),
        out_shape=jax.ShapeDtypeStruct((B, S, D), F32),
        compiler_params=_params(2),
        name="xattn_sublayer",
    )(x, kp, vt, wq, wo, g, b)


def _pad_cols(w, n):
    return jnp.pad(w, [(0, 0)] * (w.ndim - 1) + [(0, n - w.shape[-1])])


def _split_w_in(w_in, b_in):
    hd, bw = HEAD_DIM, BRANCH_WIDTH
    sizes = (bw, bw, bw, bw, hd, hd, hd, hd, hd, hd, 3 * N_HEADS, bw, bw, bw, N_HEADS,
             MLA_Q_RANK, MLA_KV_RANK, MLA_ROPE, N_BRANCH * D_MODEL)
    names = ("a_q", "a_k", "a_v", "b_q", "b_kc", "b_vc", "b_ks", "b_vs", "b_kw", "b_vw", "b_g",
             "c_q", "c_k", "c_v", "c_f", "d_cq", "d_ckv", "d_kr", "g_merge")
    w, b, off = {}, {}, 0
    for name, size in zip(names, sizes):
        w[name] = w_in[:, :, off:off + size]
        b[name] = b_in[:, off:off + size]
        off += size
    return w, b


def _row_major(ws, bs, names, pad_to):
    w = _pad_cols(jnp.concatenate([ws[n] for n in names], axis=-1), pad_to).astype(BF16)
    b = _pad_cols(jnp.concatenate([bs[n] for n in names], axis=-1), pad_to)[:, None, :]
    return w, b


def _channel_major(ws, bs, names, pad_to):
    w = _pad_cols(jnp.concatenate([ws[n] for n in names], axis=-1), pad_to).transpose(0, 2, 1).astype(BF16)
    b = _pad_cols(jnp.concatenate([bs[n] for n in names], axis=-1), pad_to)[:, :, None]
    return w, b


def _rope_slab(a, swap):
    half = MLA_ROPE // 2
    first, second = a[..., :half], a[..., half:]
    body = jnp.concatenate([second, first] if swap else [first, second], axis=-1)
    return jnp.pad(body, [(0, 0)] * (a.ndim - 1) + [(MLA_NOPE, LANES - MLA_NOPE - MLA_ROPE)])


def _mla_weights(ws, bs, mla_w_uq, mla_w_ukv):
    L = mla_w_uq.shape[0]
    half = MLA_ROPE // 2
    def key_slab(a):
        return jnp.concatenate([_rope_slab(a, False)[..., :MLA_NOPE + MLA_ROPE], _rope_slab(a, True)[..., MLA_NOPE:MLA_NOPE + MLA_ROPE]],
                               axis=-1)

    w = jnp.concatenate([ws["d_cq"], ws["d_ckv"], key_slab(ws["d_kr"])], axis=-1).astype(BF16)
    b = jnp.concatenate([bs["d_cq"], bs["d_ckv"], key_slab(bs["d_kr"])], axis=-1)[:, None, :]
    uq = mla_w_uq.reshape(L, MLA_Q_RANK, N_HEADS, MLA_NOPE + MLA_ROPE).transpose(0, 2, 1, 3)
    wq = jnp.concatenate([uq[..., :MLA_NOPE], _rope_slab(uq[..., MLA_NOPE:], False)[..., MLA_NOPE:]],
                         axis=-1).astype(BF16)
    wqs = jnp.concatenate([jnp.zeros_like(uq[..., :MLA_NOPE]), _rope_slab(uq[..., MLA_NOPE:], True)[..., MLA_NOPE:]],
                          axis=-1).astype(BF16)
    ukv = mla_w_ukv.reshape(L, MLA_KV_RANK, N_HEADS, MLA_NOPE + HEAD_DIM).transpose(0, 2, 1, 3)
    wk = _pad_cols(ukv[..., :MLA_NOPE], QK_WIDTH).astype(BF16)
    wvt = ukv[..., MLA_NOPE:].transpose(0, 1, 3, 2).reshape(L, BRANCH_WIDTH, MLA_KV_RANK).astype(BF16)
    inv_freq = ROPE_BASE ** (-jnp.arange(0, MLA_ROPE, 2, dtype=F32) / MLA_ROPE)
    freq_row = _rope_slab(jnp.concatenate([inv_freq, inv_freq])[None, :], False)
    sign_row = _rope_slab(jnp.concatenate([-jnp.ones((half,), F32), jnp.ones((half,), F32)])[None, :], False)
    def by_pair(a):
        return jnp.concatenate([a[:, 0::2], a[:, 1::2]], axis=-1)

    return w, b, by_pair(wq), by_pair(wqs), by_pair(wk), wvt, freq_row, sign_row


def kernel(x, mem, positions, ln_g, ln_b, ffn1_w_in, ffn1_w_out, ffn2_w_in, ffn2_w_out, w_in, b_in, w_branch, w_o,
           mla_q_norm, mla_w_uq, mla_kv_norm, mla_w_ukv, nsa_cmp_pos, nsa_cmp_w1, nsa_cmp_w2, xa_w_q, xa_w_kv, xa_w_o):
    B, S, D = x.shape
    L = ln_g.shape[0]
    ffn1 = (ffn1_w_in.astype(BF16), ffn1_w_out.astype(BF16))
    ffn2 = (ffn2_w_in.astype(BF16), ffn2_w_out.astype(BF16))

    ws, bs = _split_w_in(w_in.astype(BF16), b_in)
    moba_w, moba_b = _row_major(ws, bs, ("a_q", "a_k"), 2 * BRANCH_WIDTH)
    moba_wt, moba_bt = _channel_major(ws, bs, ("a_v",), BRANCH_WIDTH)
    nsa_w, nsa_b = _row_major(ws, bs, ("b_q", "b_ks", "b_kw", "b_kc", "b_vc"), 2 * BRANCH_WIDTH)
    nsa_wt, nsa_bt = _channel_major(ws, bs, ("b_vs", "b_vw", "b_g"), 2 * HEAD_DIM + NSA_GATE_ROWS)
    fox_w, fox_b = _row_major(ws, bs, ("c_q", "c_k"), 2 * BRANCH_WIDTH)
    fox_wt, fox_bt = _channel_major(ws, bs, ("c_v", "c_f"), BRANCH_WIDTH + 2 * FOX_F_ROWS)

    mla_w, mla_b, mla_wq, mla_wqs, mla_wk, mla_wvt, freq_row, sign_row = _mla_weights(ws, bs, mla_w_uq, mla_w_ukv)
    cos, sin = _rope_tables(positions, freq_row, sign_row)

    merge_wg = ws["g_merge"].astype(BF16)
    merge_bg = bs["g_merge"][:, None, :]
    merge_wb = w_branch.astype(BF16)
    merge_wo = w_o.astype(BF16)
    nsa_w1 = nsa_cmp_w1.reshape(L, 2, NSA_CMP_LEN, HEAD_DIM, NSA_CMP_HIDDEN).astype(BF16)
    nsa_w2 = nsa_cmp_w2.astype(BF16)
    xa_wq, xa_wo = xa_w_q.astype(BF16), xa_w_o.astype(BF16)
    xa_wk = xa_w_kv[:, :, :BRANCH_WIDTH].astype(BF16)
    xa_wvt = xa_w_kv[:, :, BRANCH_WIDTH:].transpose(0, 2, 1).astype(BF16)

    for l in range(L):
        ln = lambda i: (ln_g[l, i][None, :], ln_b[l, i][None, :])
        x2, x16 = _ffn_sublayer(x.reshape(B * S, D), *ffn1, *ln(0), l, True)
        x16 = x16.reshape(B, S, D)
        branches = (
            _moba(x16, moba_w, moba_b, moba_wt, moba_bt, l),
            _nsa(x16, nsa_w, nsa_b, nsa_wt, nsa_bt, nsa_cmp_pos, nsa_w1, nsa_w2, l),
            _fox(x16, fox_w, fox_b, fox_wt, fox_bt, l),
            _mla(x16, cos, sin, mla_w, mla_b, mla_q_norm[:, None, :], mla_kv_norm[:, None, :],
                 mla_wq, mla_wqs, mla_wk, mla_wvt, l),
        )
        x2 = _merge_sublayer(x2, [o.reshape(B * S, BRANCH_WIDTH) for o in branches],
                             merge_wg, merge_bg, merge_wb, merge_wo, *ln(1), l)
        x = _xattn_sublayer(x2.reshape(B, S, D), *_mem_kv(mem, xa_wk, xa_wvt, l), xa_wq, xa_wo, *ln(2), l)
        x = _ffn_sublayer(x.reshape(B * S, D), *ffn2, *ln(3), l, False)[0].reshape(B, S, D)
    return x
```

```python
import functools

import jax
import jax.numpy as jnp
import numpy as np
from jax import lax
from jax.experimental import pallas as pl
from jax.experimental.pallas import tpu as pltpu

F32 = jnp.float32
BF16 = jnp.bfloat16

D_MODEL = 1024
DEPTH = 4
HEAD_DIM = 64
N_HEADS = 4
BRANCH_WIDTH = N_HEADS * HEAD_DIM
N_BRANCH = 4
MOBA_BLOCK = 256
MOBA_TOPK = 3
NSA_CMP_LEN = 32
NSA_CMP_STRIDE = 16
NSA_CMP_HIDDEN = 256
NSA_SLC_BLOCK = 64
NSA_TOP_N = 16
NSA_WINDOW = 512
NSA_FORCE_BONUS = 1.0e4
MLA_Q_RANK = 256
MLA_KV_RANK = 128
MLA_NOPE = 64
MLA_ROPE = 32
ROPE_BASE = 10000.0
MEM_LEN = 256
D_FF = 2816
ALPHA = (2 * DEPTH) ** 0.25
LN_EPS = 1e-5
RMS_EPS = 1e-6
NEG_INF = -1e30
MOBA_SLOPES = tuple(2.0 ** -(2 * h + 1) for h in range(N_HEADS))
NSA_SLOPES = tuple(2.0 ** -(2 * h + 2) for h in range(N_HEADS))

LOG2E = 1.4426950408889634
VT_ROWS = HEAD_DIM + 16
LANES = 128
TQ = 256
TK = 256
QK_WIDTH = 128
FOX_F_ROWS = 8
NSA_GATE_ROWS = 16
PAIR_WIDTH = 2 * HEAD_DIM
ROW_CHUNK = 512
FFN_TILE_F = 256
STATIC_HEADS_MAX_TILES = 6
TOKEN_TILE = 512
XATTN_TILE = 1024
XATTN_SPLIT = 4
VMEM_LIMIT = 56 * 1024 * 1024


def _nn(a, b):
    return jnp.dot(a, b, preferred_element_type=F32)


def _nt(a, b):
    return lax.dot_general(a, b, (((1,), (1,)), ((), ())), preferred_element_type=F32)


def _layer_norm(z, g, b):
    mu = jnp.mean(z, -1, keepdims=True)
    d = z - mu
    var = jnp.mean(d * d, -1, keepdims=True)
    return d * lax.rsqrt(var + LN_EPS) * g + b


def _params(n_parallel):
    return pltpu.CompilerParams(dimension_semantics=("parallel",) * n_parallel,
                                vmem_limit_bytes=VMEM_LIMIT)


def _ffn_kernel(x_ref, wi_ref, wo_ref, g_ref, b_ref, o_ref, *o16_ref):
    x = x_ref[...]
    xb = x.astype(BF16)
    acc = None
    for c in range(D_FF // FFN_TILE_F):
        lo = c * FFN_TILE_F
        hg = _nn(xb, wi_ref[:, lo:lo + FFN_TILE_F])
        hu = _nn(xb, wi_ref[:, D_FF + lo:D_FF + lo + FFN_TILE_F])
        a = (hg * jax.nn.sigmoid(hg) * hu).astype(BF16)
        part = _nn(a, wo_ref[lo:lo + FFN_TILE_F, :])
        acc = part if acc is None else acc + part
    out = _layer_norm(ALPHA * x + 0.5 * acc, g_ref[...], b_ref[...])
    o_ref[...] = out
    for ref in o16_ref:
        ref[...] = out.astype(BF16)


def _ffn_sublayer(x2, wi, wo, g, b, layer, with_bf16_copy):
    T, D = x2.shape
    tm = TOKEN_TILE
    out_spec = pl.BlockSpec((tm, D), lambda i: (i, 0))
    dtypes = (F32, BF16) if with_bf16_copy else (F32,)
    return pl.pallas_call(
        _ffn_kernel,
        grid=(T // tm,),
        in_specs=[
            pl.BlockSpec((tm, D), lambda i: (i, 0)),
            pl.BlockSpec((None, D, 2 * D_FF), lambda i: (layer, 0, 0)),
            pl.BlockSpec((None, D_FF, D), lambda i: (layer, 0, 0)),
            pl.BlockSpec((1, D), lambda i: (0, 0)),
            pl.BlockSpec((1, D), lambda i: (0, 0)),
        ],
        out_specs=[out_spec] * len(dtypes),
        out_shape=[jax.ShapeDtypeStruct((T, D), dt) for dt in dtypes],
        compiler_params=_params(1),
        name="ffn_sublayer",
    )(x2, wi, wo, g, b)


def _tile_iotas():
    r = lax.broadcasted_iota(jnp.int32, (TK, TQ), 0)
    c = lax.broadcasted_iota(jnp.int32, (TK, TQ), 1)
    return r, c


def _scores(s_ref, n_tiles, score_tile, h):
    m = None
    for j in range(n_tiles):
        s = score_tile(h, j)
        s_ref[j * TK:(j + 1) * TK, :] = s
        mj = jnp.max(s, axis=0, keepdims=True)
        m = mj if m is None else jnp.maximum(m, mj)
    return m


def _weighted_values(s_ref, n_tiles, m, vt_cols):
    p_all = jnp.concatenate([jnp.exp2(s_ref[j * TK:(j + 1) * TK, :] - m).astype(BF16) for j in range(n_tiles)],
                            axis=0)
    res = _nn(vt_cols, p_all)
    return res[:HEAD_DIM] / jnp.maximum(res[HEAD_DIM:HEAD_DIM + 1], 1e-30)


def _attend_heads(n_tiles, score_tile, vt_cols, out_ref, s_refs, first=None, ahead_of_last=None):
    if n_tiles <= STATIC_HEADS_MAX_TILES:
        tiles = first if first is not None else [score_tile(0, j) for j in range(n_tiles)]
        for h in range(N_HEADS):
            if h + 1 < N_HEADS:
                ahead = [score_tile(h + 1, j) for j in range(n_tiles)]
            else:
                ahead = ahead_of_last() if ahead_of_last is not None else None
            m =functools.reduce(jnp.maximum, [jnp.max(s, axis=0, keepdims=True) for s in tiles])
            p_all = jnp.concatenate([jnp.exp2(s - m).astype(BF16) for s in tiles], axis=0)
            res = _nn(vt_cols(h, n_tiles), p_all)
            out_ref[h] = res[:HEAD_DIM] / jnp.maximum(res[HEAD_DIM:HEAD_DIM + 1], 1e-30)
            tiles = ahead
        return tiles
    assert first is None, "the loop path computes its own first logits"
    s_a, s_b = s_refs
    m_first = _scores(s_a, n_tiles, score_tile, 0)

    def pair(i, m_a):
        h_a, h_b = 2 * i, 2 * i + 1
        h_next = jnp.minimum(2 * i + 2, N_HEADS - 1)
        m_b = _scores(s_b, n_tiles, score_tile, h_b)
        out_ref[h_a] = _weighted_values(s_a, n_tiles, m_a, vt_cols(h_a, n_tiles))
        m_next = _scores(s_a, n_tiles, score_tile, h_next)
        out_ref[h_b] = _weighted_values(s_b, n_tiles, m_b, vt_cols(h_b, n_tiles))
        return m_next

    lax.fori_loop(0, N_HEADS // 2, pair, m_first)
    return ahead_of_last() if ahead_of_last is not None else None


def _position_columns(n_rows, first, step):
    row = lax.broadcasted_iota(jnp.int32, (n_rows, HEAD_DIM), 0)
    lane = lax.broadcasted_iota(jnp.int32, (n_rows, HEAD_DIM), 1)
    pos = first + step * row
    hi = ((pos >> 7) << 7).astype(F32)
    lo = (pos & 127).astype(F32)
    return jnp.where(lane < 6, jnp.where((lane & 1) == 0, hi, lo), 0.0)


def _bf16_pieces(value):
    hi = float(np.asarray(value, np.float32).astype(BF16))
    mid = float(np.asarray(value - hi, np.float32).astype(BF16))
    return hi, mid, value - hi - mid


def _slope_columns(n_rows, slope):
    lane = lax.broadcasted_iota(jnp.int32, (n_rows, HEAD_DIM), 1)
    hi, mid, rest = _bf16_pieces(slope * LOG2E)
    return jnp.where(lane < 2, hi, jnp.where(lane < 4, mid, jnp.where(lane < 6, rest, 0.0))).astype(F32)


def _ones_columns(n_rows, n_cols):
    lane = lax.broadcasted_iota(jnp.int32, (n_rows, HEAD_DIM), 1)
    return jnp.where(lane < n_cols, 1.0, 0.0).astype(F32)


def _half_mask(n_rows, parity):
    lane = lax.broadcasted_iota(jnp.int32, (n_rows, PAIR_WIDTH), 1)
    return (lane // HEAD_DIM) == parity


def _bias_slab(cols, parity):
    zeros = jnp.zeros_like(cols)
    return jnp.concatenate([cols, zeros] if parity == 0 else [zeros, cols], axis=1)


def _stage_operand(pair_slab, parity, bias_cols, scale=None):
    feats = pair_slab if scale is None else pair_slab * scale
    own = _half_mask(pair_slab.shape[0], parity)
    return jnp.where(own, feats, _bias_slab(bias_cols, 1 - parity)).astype(BF16)


def _split3(a):
    hi = a.astype(BF16)
    rest = a - hi.astype(F32)
    mid = rest.astype(BF16)
    return hi, mid, (rest - mid.astype(F32)).astype(BF16)


def _project(x_ref, w_ref, b_ref, wt_ref, bt_ref, rm_ref, cm_ref):
    S = x_ref.shape[1]
    for i in range(S // ROW_CHUNK):
        rows = slice(i * ROW_CHUNK, (i + 1) * ROW_CHUNK)
        xb = x_ref[0, rows, :]
        rm_ref[rows, :] = _nn(xb, w_ref[...]) + b_ref[...]
        cm_ref[:, rows] = _nt(wt_ref[...], xb) + bt_ref[...]


def _stage_vt(vt_ref, v_t):
    G, _, S = vt_ref.shape
    vt_ref[:, 0:HEAD_DIM, :] = v_t.astype(BF16).reshape(G, HEAD_DIM, S)
    row = lax.broadcasted_iota(jnp.int32, (G, VT_ROWS - HEAD_DIM, S), 1)
    vt_ref[:, HEAD_DIM:VT_ROWS, :] = jnp.where(row == 0, 1.0, 0.0).astype(BF16)


def _causal_attention(o_ref, qp_ref, kp_ref, vt_ref, ot_ref, s_refs):
    S = qp_ref.shape[1]
    r, c = _tile_iotas()
    for qi in range(S // TQ):
        q0 = qi * TQ

        def score_tile(h, j, qi=qi, q0=q0):
            s = _nt(kp_ref[h, j * TK:(j + 1) * TK, :], qp_ref[h, q0:q0 + TQ, :])
            return jnp.where(r <= c, s, NEG_INF) if j == qi else s

        _attend_heads(qi + 1, score_tile, lambda h, n: vt_ref[h, :, 0:n * TK], ot_ref, s_refs)
        _store_heads(o_ref, q0, ot_ref[...].reshape(BRANCH_WIDTH, TQ))


def _store_heads(o_ref, q0, heads_t):
    o_ref[0, q0:q0 + TQ, :] = heads_t.T


def _attention_scratch(S):
    return [pltpu.VMEM((N_HEADS, HEAD_DIM, TQ), F32), pltpu.VMEM((S, TQ), F32), pltpu.VMEM((S, TQ), F32)]


def _moba_kernel(x_ref, w_ref, b_ref, wt_ref, bt_ref, o_ref, rm_ref, cm_ref, qp_ref, kp_ref, vt_ref, sel_ref,
                 ot_ref, sa_ref, sb_ref):
    S = x_ref.shape[1]
    nb = S // MOBA_BLOCK
    _project(x_ref, w_ref, b_ref, wt_ref, bt_ref, rm_ref, cm_ref)
    _stage_vt(vt_ref, cm_ref[...])
    kcols = _position_columns(S, 0, 1)
    blk = lax.broadcasted_iota(jnp.int32, (nb, S), 0)
    own = lax.broadcasted_iota(jnp.int32, (nb, S), 1) // MOBA_BLOCK
    cand = blk < own
    pair_gates = []
    for pair in range(N_HEADS // 2):
        q_pair = rm_ref[:, pair * PAIR_WIDTH:(pair + 1) * PAIR_WIDTH]
        k_pair = rm_ref[:, BRANCH_WIDTH + pair * PAIR_WIDTH:BRANCH_WIDTH + (pair + 1) * PAIR_WIDTH]
        for parity in range(2):
            h = 2 * pair + parity
            qp_ref[h] = _stage_operand(q_pair, parity, _slope_columns(S, MOBA_SLOPES[h]), HEAD_DIM ** -0.5 * LOG2E)
            kp_ref[h] = _stage_operand(k_pair, parity, kcols)
        k_mean = jnp.sum(k_pair.reshape(nb, MOBA_BLOCK, PAIR_WIDTH), axis=1) * (1.0 / MOBA_BLOCK)
        k_means = jnp.concatenate([jnp.where(_half_mask(nb, parity), k_mean, 0.0) for parity in range(2)], axis=0)
        km_hi, km_mid, _ = _split3(k_means)
        q_hi, q_mid, _ = _split3(q_pair)
        pair_gates.append(_nt(km_hi, q_hi) + _nt(km_hi, q_mid) + _nt(km_mid, q_hi))
    for h in range(N_HEADS):
        pair, parity = divmod(h, 2)
        gate = jnp.where(cand, pair_gates[pair][parity * nb:(parity + 1) * nb, :], NEG_INF)
        beaten = jnp.zeros((nb, S), F32)
        for j in range(nb):
            other = gate[j:j + 1, :]
            wins = (other > gate) | ((other == gate) & (blk > j))
            beaten = beaten + jnp.where(wins, 1.0, 0.0)
        sel = jnp.where(cand & (beaten < MOBA_TOPK), 1.0, 0.0)
        for qi in range(S // TQ):
            sel_ref[h, qi] = sel[:, qi * TQ:(qi + 1) * TQ]

    r, c = _tile_iotas()
    for qi in range(S // TQ):
        q0 = qi * TQ

        def score_tile(h, j, qi=qi, q0=q0):
            s = _nt(kp_ref[h, j * TK:(j + 1) * TK, :], qp_ref[h, q0:q0 + TQ, :])
            return jnp.where(r <= c if j == qi else sel_ref[h, qi, j:j + 1, :] > 0.5, s, NEG_INF)

        _attend_heads(qi + 1, score_tile, lambda h, n: vt_ref[h, :, 0:n * TK], ot_ref, (sa_ref, sb_ref))
        _store_heads(o_ref, q0, ot_ref[...].reshape(BRANCH_WIDTH, TQ))


def _moba(x, w, b, wt, bt, layer):
    B, S, D = x.shape
    n_rm = w.shape[-1]
    n_cm = wt.shape[1]
    nb = S // MOBA_BLOCK
    return pl.pallas_call(
        _moba_kernel,
        grid=(B,),
        in_specs=[
            pl.BlockSpec((1, S, D), lambda i: (i, 0, 0)),
            pl.BlockSpec((None, D, n_rm), lambda i: (layer, 0, 0)),
            pl.BlockSpec((None, 1, n_rm), lambda i: (layer, 0, 0)),
            pl.BlockSpec((None, n_cm, D), lambda i: (layer, 0, 0)),
            pl.BlockSpec((None, n_cm, 1), lambda i: (layer, 0, 0)),
        ],
        out_specs=pl.BlockSpec((1, S, BRANCH_WIDTH), lambda i: (i, 0, 0)),
        out_shape=jax.ShapeDtypeStruct((B, S, BRANCH_WIDTH), F32),
        scratch_shapes=[
            pltpu.VMEM((S, n_rm), F32),
            pltpu.VMEM((n_cm, S), F32),
            pltpu.VMEM((N_HEADS, S, QK_WIDTH), BF16),
            pltpu.VMEM((N_HEADS, S, QK_WIDTH), BF16),
            pltpu.VMEM((N_HEADS, VT_ROWS, S), BF16),
            pltpu.VMEM((N_HEADS, S // TQ, nb, TQ), F32),
        ] + _attention_scratch(S),
        compiler_params=_params(1),
        name="moba_mixer",
    )(x, w, b, wt, bt)


def _fox_kernel(x_ref, w_ref, b_ref, wt_ref, bt_ref, o_ref, rm_ref, cm_ref, qp_ref, kp_ref, vt_ref,
                ot_ref, sa_ref, sb_ref):
    S = x_ref.shape[1]
    _project(x_ref, w_ref, b_ref, wt_ref, bt_ref, rm_ref, cm_ref)
    _stage_vt(vt_ref, cm_ref[0:BRANCH_WIDTH, :])
    f_t = cm_ref[BRANCH_WIDTH:BRANCH_WIDTH + FOX_F_ROWS, :]
    log_sig = jnp.minimum(f_t, 0.0) - jnp.log1p(jnp.exp(-jnp.abs(f_t)))
    blk = MOBA_BLOCK
    tri = jnp.where(lax.broadcasted_iota(jnp.int32, (blk, blk), 0) <= lax.broadcasted_iota(jnp.int32, (blk, blk), 1),
                    1.0, 0.0).astype(BF16)
    stacked = jnp.concatenate([log_sig[:, i * blk:(i + 1) * blk] for i in range(S // blk)], axis=0)
    within = functools.reduce(jnp.add, [_nn(piece, tri) for piece in _split3(stacked)])
    carry = jnp.zeros((FOX_F_ROWS, 1), F32)
    pieces = []
    for i in range(S // blk):
        cs = within[i * FOX_F_ROWS:(i + 1) * FOX_F_ROWS, :] + carry
        pieces.append(cs)
        carry = cs[:, blk - 1:blk]
    key_bias_t = jnp.concatenate(pieces, axis=1) * (-LOG2E)
    key_bias = jnp.concatenate([key_bias_t, jnp.zeros((LANES - FOX_F_ROWS, S), F32)], axis=0).T
    lane = lax.broadcasted_iota(jnp.int32, (S, HEAD_DIM), 1)
    ones = _ones_columns(S, 3)
    for h in range(N_HEADS):
        pair, parity = divmod(h, 2)
        q_pair = rm_ref[:, pair * PAIR_WIDTH:(pair + 1) * PAIR_WIDTH]
        k_pair = rm_ref[:, BRANCH_WIDTH + pair * PAIR_WIDTH:BRANCH_WIDTH + (pair + 1) * PAIR_WIDTH]
        d = key_bias[:, h:h + 1]
        d_hi = d.astype(BF16).astype(F32)
        d_mid = (d - d_hi).astype(BF16).astype(F32)
        d_lo = d - d_hi - d_mid
        dcols = jnp.where(lane == 0, d_hi, jnp.where(lane == 1, d_mid, jnp.where(lane == 2, d_lo, 0.0)))
        qp_ref[h] = _stage_operand(q_pair, parity, ones, HEAD_DIM ** -0.5 * LOG2E)
        kp_ref[h] = _stage_operand(k_pair, parity, dcols)

    _causal_attention(o_ref, qp_ref, kp_ref, vt_ref, ot_ref, (sa_ref, sb_ref))


def _fox(x, w, b, wt, bt, layer):
    B, S, D = x.shape
    n_rm = w.shape[-1]
    n_cm = wt.shape[1]
    return pl.pallas_call(
        _fox_kernel,
        grid=(B,),
        in_specs=[
            pl.BlockSpec((1, S, D), lambda i: (i, 0, 0)),
            pl.BlockSpec((None, D, n_rm), lambda i: (layer, 0, 0)),
            pl.BlockSpec((None, 1, n_rm), lambda i: (layer, 0, 0)),
            pl.BlockSpec((None, n_cm, D), lambda i: (layer, 0, 0)),
            pl.BlockSpec((None, n_cm, 1), lambda i: (layer, 0, 0)),
        ],
        out_specs=pl.BlockSpec((1, S, BRANCH_WIDTH), lambda i: (i, 0, 0)),
        out_shape=jax.ShapeDtypeStruct((B, S, BRANCH_WIDTH), F32),
        scratch_shapes=[
            pltpu.VMEM((S, n_rm), F32),
            pltpu.VMEM((n_cm, S), F32),
            pltpu.VMEM((N_HEADS, S, QK_WIDTH), BF16),
            pltpu.VMEM((N_HEADS, S, QK_WIDTH), BF16),
            pltpu.VMEM((N_HEADS, VT_ROWS, S), BF16),
        ] + _attention_scratch(S),
        compiler_params=_params(1),
        name="fox_mixer",
    )(x, w, b, wt, bt)


def _rope_table_kernel(pos_ref, freq_ref, sign_ref, cos_ref, sin_ref):
    ang = pos_ref[0].astype(F32) * freq_ref[...]
    cos_ref[0] = jnp.cos(ang)
    sin_ref[0] = jnp.sin(ang) * sign_ref[...]


def _rope_tables(positions, freq_row, sign_row):
    B, S = positions.shape
    return pl.pallas_call(
        _rope_table_kernel,
        grid=(B,),
        in_specs=[
            pl.BlockSpec((1, S, 1), lambda i: (i, 0, 0)),
            pl.BlockSpec((1, LANES), lambda i: (0, 0)),
            pl.BlockSpec((1, LANES), lambda i: (0, 0)),
        ],
        out_specs=[pl.BlockSpec((1, S, LANES), lambda i: (i, 0, 0))] * 2,
        out_shape=[jax.ShapeDtypeStruct((B, S, LANES), F32)] * 2,
        compiler_params=_params(1),
        name="rope_tables",
    )(positions.reshape(B, S, 1), freq_row, sign_row)


def _rms_norm(x, g):
    return x * lax.rsqrt(jnp.mean(x * x, -1, keepdims=True) + RMS_EPS) * g


def _mla_kernel(x_ref, cos_ref, sin_ref, w_ref, b_ref, qn_ref, kvn_ref, wq_ref, wqs_ref, wk_ref, wvt_ref,
                o_ref, rm_ref, qp_ref, kp_ref, vt_ref, ot_ref, sa_ref, sb_ref):
    S = x_ref.shape[1]
    for i in range(S // ROW_CHUNK):
        rows = slice(i * ROW_CHUNK, (i + 1) * ROW_CHUNK)
        rm_ref[rows, :] = _nn(x_ref[0, rows, :], w_ref[...]) + b_ref[...]
    cos = cos_ref[0]
    sin = sin_ref[0]
    scale = (MLA_NOPE + MLA_ROPE) ** -0.5 * LOG2E
    q_lat = _rms_norm(rm_ref[:, 0:MLA_Q_RANK], qn_ref[...]).astype(BF16)
    kv_lat = _rms_norm(rm_ref[:, MLA_Q_RANK:MLA_Q_RANK + MLA_KV_RANK], kvn_ref[...]).astype(BF16)
    c0 = MLA_Q_RANK + MLA_KV_RANK
    kr = rm_ref[:, c0:c0 + LANES]
    k_rope = kr * cos + pltpu.roll(kr, LANES - MLA_ROPE, axis=1) * sin
    _stage_vt(vt_ref, _nt(wvt_ref[...], kv_lat))
    pairs = range(N_HEADS // 2)
    q_plain = [_nn(q_lat, wq_ref[p]) for p in pairs]
    q_twin = [_nn(q_lat, wqs_ref[p]) for p in pairs]
    k_nope = [_nn(kv_lat, wk_ref[p]) for p in pairs]
    for h in range(N_HEADS):
        p, lanes = h // 2, slice((h % 2) * QK_WIDTH, (h % 2 + 1) * QK_WIDTH)
        qp_ref[h] = ((q_plain[p][:, lanes] * cos + q_twin[p][:, lanes] * sin) * scale).astype(BF16)
        kp_ref[h] = (k_nope[p][:, lanes] + k_rope).astype(BF16)

    _causal_attention(o_ref, qp_ref, kp_ref, vt_ref, ot_ref, (sa_ref, sb_ref))


def _mla(x, cos, sin, w, b, qn, kvn, wq, wqs, wk, wvt, layer):
    B, S, D = x.shape
    n_rm = w.shape[-1]
    lay3 = lambda i: (layer, 0, 0)
    lay4 = lambda i: (layer, 0, 0, 0)
    return pl.pallas_call(
        _mla_kernel,
        grid=(B,),
        in_specs=[
            pl.BlockSpec((1, S, D), lambda i: (i, 0, 0)),
            pl.BlockSpec((1, S, LANES), lambda i: (i, 0, 0)),
            pl.BlockSpec((1, S, LANES), lambda i: (i, 0, 0)),
            pl.BlockSpec((None, D, n_rm), lay3),
            pl.BlockSpec((None, 1, n_rm), lay3),
            pl.BlockSpec((None, 1, MLA_Q_RANK), lay3),
            pl.BlockSpec((None, 1, MLA_KV_RANK), lay3),
            pl.BlockSpec((None, N_HEADS // 2, MLA_Q_RANK, 2 * QK_WIDTH), lay4),
            pl.BlockSpec((None, N_HEADS // 2, MLA_Q_RANK, 2 * QK_WIDTH), lay4),
            pl.BlockSpec((None, N_HEADS // 2, MLA_KV_RANK, 2 * QK_WIDTH), lay4),
            pl.BlockSpec((None, BRANCH_WIDTH, MLA_KV_RANK), lay3),
        ],
        out_specs=pl.BlockSpec((1, S, BRANCH_WIDTH), lambda i: (i, 0, 0)),
        out_shape=jax.ShapeDtypeStruct((B, S, BRANCH_WIDTH), F32),
        scratch_shapes=[
            pltpu.VMEM((S, n_rm), F32),
            pltpu.VMEM((N_HEADS, S, QK_WIDTH), BF16),
            pltpu.VMEM((N_HEADS, S, QK_WIDTH), BF16),
            pltpu.VMEM((N_HEADS, VT_ROWS, S), BF16),
        ] + _attention_scratch(S),
        compiler_params=_params(1),
        name="mla_mixer",
    )(x, cos, sin, w, b, qn, kvn, wq, wqs, wk, wvt)


N_CMP_PAD = 128


def _nsa_compress(src_ref, pos_ref, w1_ref, w2_ref):
    accs = [jnp.zeros((N_CMP_PAD, NSA_CMP_HIDDEN), F32) for _ in range(2)]
    n_cmp = N_CMP_PAD - 1
    for p in range(NSA_CMP_LEN):
        rows = src_ref[pl.ds(p, n_cmp, stride=NSA_CMP_STRIDE), :]
        for which in range(2):
            part = rows[:, which * HEAD_DIM:(which + 1) * HEAD_DIM] + pos_ref[which, p:p + 1, :]
            part = jnp.concatenate([part, jnp.zeros((1, HEAD_DIM), F32)], axis=0)
            accs[which] = accs[which] + _nn(part.astype(BF16), w1_ref[which, p])
    return [_nn(jax.nn.gelu(accs[which]).astype(BF16), w2_ref[which]) for which in range(2)]


def _nsa_kernel(x_ref, w_ref, b_ref, wt_ref, bt_ref, pos_ref, w1_ref, w2_ref, o_ref,
                rm_ref, cm_ref, cin_ref, qp_ref, ks_ref, kw_ref, kc_ref, vt_ref, vtc_ref, sel_ref, gate_ref,
                oslc_ref, owin_ref, sa_ref, sb_ref):
    S = x_ref.shape[1]
    n_slc = S // NSA_SLC_BLOCK
    _project(x_ref, w_ref, b_ref, wt_ref, bt_ref, rm_ref, cm_ref)
    _stage_vt(vt_ref, cm_ref[0:2 * HEAD_DIM, :])
    gate_ref[...] = jax.nn.sigmoid(cm_ref[2 * HEAD_DIM:2 * HEAD_DIM + NSA_GATE_ROWS, :])
    c_kk, c_kc = BRANCH_WIDTH, BRANCH_WIDTH + PAIR_WIDTH
    kcols = _position_columns(S, 0, 1)
    for h in range(N_HEADS):
        pair, parity = divmod(h, 2)
        qp_ref[h] = _stage_operand(rm_ref[:, pair * PAIR_WIDTH:(pair + 1) * PAIR_WIDTH], parity,
                                   _slope_columns(S, NSA_SLOPES[h]), HEAD_DIM ** -0.5 * LOG2E)
    slab = rm_ref[:, c_kk:c_kk + PAIR_WIDTH]
    swapped = pltpu.roll(slab, HEAD_DIM, axis=1)
    ks_ref[0] = _stage_operand(slab, 0, kcols)
    ks_ref[1] = _stage_operand(swapped, 1, kcols)
    kw_ref[0] = _stage_operand(swapped, 0, kcols)
    kw_ref[1] = _stage_operand(slab, 1, kcols)
    cin_ref[...] = rm_ref[:, c_kc:c_kc + 2 * HEAD_DIM]
    k_cmp, v_cmp = _nsa_compress(cin_ref, pos_ref, w1_ref, w2_ref)
    kv_cmp = jnp.concatenate([k_cmp, v_cmp], axis=1)
    ccols = _position_columns(N_CMP_PAD, NSA_CMP_LEN - 1, NSA_CMP_STRIDE)
    kc_ref[0] = _stage_operand(kv_cmp, 0, ccols)
    kc_ref[1] = _stage_operand(pltpu.roll(kv_cmp, HEAD_DIM, axis=1), 1, ccols)
    vtc_ref[...] = kv_cmp.T[HEAD_DIM:, :].astype(BF16)

    jj = lax.broadcasted_iota(jnp.int32, (n_slc, N_CMP_PAD), 0) * NSA_SLC_BLOCK
    cc = lax.broadcasted_iota(jnp.int32, (n_slc, N_CMP_PAD), 1) * NSA_CMP_STRIDE
    overlap_t = jnp.where((cc < jj + NSA_SLC_BLOCK) & (cc + NSA_CMP_LEN > jj), 1.0, 0.0).astype(BF16)

    r, c = _tile_iotas()
    per_tile = TK // NSA_SLC_BLOCK
    win_tiles = NSA_WINDOW // TK
    c_idx = lax.broadcasted_iota(jnp.int32, (N_CMP_PAD, TQ), 0)
    lane_cmp = lax.broadcasted_iota(jnp.int32, (N_CMP_PAD, TQ), 1)
    for qi in range(S // TQ):
        q0 = qi * TQ
        cmp_ok = (c_idx * NSA_CMP_STRIDE + (NSA_CMP_LEN - 1) <= q0 + lane_cmp) & (c_idx < N_CMP_PAD - 1)
        heads = range(N_HEADS)
        ss = [jnp.where(cmp_ok, _nt(kc_ref[h % 2], qp_ref[h, q0:q0 + TQ, :]), NEG_INF) for h in heads]
        es = [jnp.where(cmp_ok, jnp.exp2(s - jnp.max(s, axis=0, keepdims=True)), 0.0) for s in ss]
        ps = [e / jnp.maximum(jnp.sum(e, axis=0, keepdims=True), 1e-30) for e in es]
        o_cmp = [_nn(vtc_ref[...], p.astype(BF16)) for p in ps]
        imp = functools.reduce(jnp.add, [_nn(overlap_t, piece) for piece in _split3(functools.reduce(jnp.add, ps))])
        n_seen = per_tile * (qi + 1)
        select = n_seen > NSA_TOP_N
        if select:
            j_idx = lax.broadcasted_iota(jnp.int32, (n_seen, TQ), 0)
            own = (q0 + lax.broadcasted_iota(jnp.int32, (n_seen, TQ), 1)) // NSA_SLC_BLOCK
            forced = (j_idx == 0) | (j_idx == own) | (j_idx == own - 1)
            score = jnp.where(j_idx <= own, imp[:n_seen] + NSA_FORCE_BONUS * jnp.where(forced, 1.0, 0.0), NEG_INF)
            beaten = jnp.zeros((n_seen, TQ), F32)
            for j in range(n_seen):
                other = score[j:j + 1, :]
                wins = (other > score) | ((other == score) & (j_idx > j))
                beaten = beaten + jnp.where(wins, 1.0, 0.0)
            sel_ref[0:n_seen, :] = jnp.where(beaten < NSA_TOP_N, 1.0, 0.0)

        def slc_tile(h, j, qi=qi, q0=q0, select=select):
            s = _nt(ks_ref[h % 2, j * TK:(j + 1) * TK, :], qp_ref[h, q0:q0 + TQ, :])
            mask = r <= c if j == qi else None
            if select:
                rows = [jnp.broadcast_to(sel_ref[per_tile * j + u:per_tile * j + u + 1, :], (NSA_SLC_BLOCK, TQ))
                        for u in range(per_tile)]
                chosen = jnp.concatenate(rows, axis=0) > 0.5
                mask = chosen if mask is None else mask & chosen
            return s if mask is None else jnp.where(mask, s, NEG_INF)

        lo = max(qi - win_tiles, 0)
        n_win = qi - lo + 1

        def win_tile(h, j, qi=qi, q0=q0, lo=lo):
            t = lo + j
            s = _nt(kw_ref[h % 2, t * TK:(t + 1) * TK, :], qp_ref[h, q0:q0 + TQ, :])
            if t == qi:
                return jnp.where(r <= c, s, NEG_INF)
            return jnp.where(c < r, s, NEG_INF) if t == qi - win_tiles else s

        win_first = _attend_heads(qi + 1, slc_tile, lambda h, n: vt_ref[0, :, 0:n * TK], oslc_ref, (sa_ref, sb_ref),
                                  None, lambda: [win_tile(0, j) for j in range(n_win)])
        _attend_heads(n_win, win_tile, lambda h, n, lo=lo: vt_ref[1, :, lo * TK:(lo + n) * TK], owin_ref,
                      (sa_ref, sb_ref), win_first)

        gates = gate_ref[:, q0:q0 + TQ]
        outs = [gates[3 * h:3 * h + 1, :] * o_cmp[h] + gates[3 * h + 1:3 * h + 2, :] * oslc_ref[h]
                + gates[3 * h + 2:3 * h + 3, :] * owin_ref[h] for h in range(N_HEADS)]
        _store_heads(o_ref, q0, jnp.concatenate(outs, axis=0))


def _nsa(x, w, b, wt, bt, pos, w1, w2, layer):
    B, S, D = x.shape
    n_rm = w.shape[-1]
    n_cm = wt.shape[1]
    lay3 = lambda i: (layer, 0, 0)
    lay4 = lambda i: (layer, 0, 0, 0)
    lay5 = lambda i: (layer, 0, 0, 0, 0)
    return pl.pallas_call(
        _nsa_kernel,
        grid=(B,),
        in_specs=[
            pl.BlockSpec((1, S, D), lambda i: (i, 0, 0)),
            pl.BlockSpec((None, D, n_rm), lay3),
            pl.BlockSpec((None, 1, n_rm), lay3),
            pl.BlockSpec((None, n_cm, D), lay3),
            pl.BlockSpec((None, n_cm, 1), lay3),
            pl.BlockSpec((None, 2, NSA_CMP_LEN, HEAD_DIM), lay4),
            pl.BlockSpec((None, 2, NSA_CMP_LEN, HEAD_DIM, NSA_CMP_HIDDEN), lay5),
            pl.BlockSpec((None, 2, NSA_CMP_HIDDEN, HEAD_DIM), lay4),
        ],
        out_specs=pl.BlockSpec((1, S, BRANCH_WIDTH), lambda i: (i, 0, 0)),
        out_shape=jax.ShapeDtypeStruct((B, S, BRANCH_WIDTH), F32),
        scratch_shapes=[
            pltpu.VMEM((S, n_rm), F32),
            pltpu.VMEM((n_cm, S), F32),
            pltpu.VMEM((S, 2 * HEAD_DIM), F32),
            pltpu.VMEM((N_HEADS, S, QK_WIDTH), BF16),
            pltpu.VMEM((2, S, QK_WIDTH), BF16),
            pltpu.VMEM((2, S, QK_WIDTH), BF16),
            pltpu.VMEM((2, N_CMP_PAD, QK_WIDTH), BF16),
            pltpu.VMEM((2, VT_ROWS, S), BF16),
            pltpu.VMEM((HEAD_DIM, N_CMP_PAD), BF16),
            pltpu.VMEM((S // NSA_SLC_BLOCK, TQ), F32),
            pltpu.VMEM((NSA_GATE_ROWS, S), F32),
            pltpu.VMEM((N_HEADS, HEAD_DIM, TQ), F32),
        ] + _attention_scratch(S),
        compiler_params=_params(1),
        name="nsa_mixer",
    )(x, w, b, wt, bt, pos, w1, w2)


def _merge_kernel(x_ref, oa_ref, ob_ref, oc_ref, od_ref, wg_ref, bg_ref, wb_ref, wo_ref, g_ref, b_ref, o_ref):
    x = x_ref[...]
    xb = x.astype(BF16)
    D = x.shape[-1]
    acc = jnp.zeros(x.shape, F32)
    for n, branch in enumerate((oa_ref, ob_ref, oc_ref, od_ref)):
        gate = jax.nn.sigmoid(_nn(xb, wg_ref[:, n * D:(n + 1) * D]) + bg_ref[:, n * D:(n + 1) * D])
        acc = acc + gate * _nn(branch[...].astype(BF16), wb_ref[n])
    half = x.shape[0] // 2
    mixed = acc.astype(BF16)
    mixes = [_nn(mixed[i * half:(i + 1) * half], wo_ref[...]) for i in range(2)]
    o_ref[...] = jnp.concatenate([_layer_norm(ALPHA * x[i * half:(i + 1) * half] + mixes[i], g_ref[...], b_ref[...])
                                  for i in range(2)], axis=0)


def _merge_sublayer(x2, branches, wg, bg, wb, wo, g, b, layer):
    T, D = x2.shape
    tm = TOKEN_TILE
    lay3 = lambda i: (layer, 0, 0)
    return pl.pallas_call(
        _merge_kernel,
        grid=(T // tm,),
        in_specs=[pl.BlockSpec((tm, D), lambda i: (i, 0))]
        + [pl.BlockSpec((tm, BRANCH_WIDTH), lambda i: (i, 0))] * N_BRANCH
        + [
            pl.BlockSpec((None, D, N_BRANCH * D), lay3),
            pl.BlockSpec((None, 1, N_BRANCH * D), lay3),
            pl.BlockSpec((None, N_BRANCH, BRANCH_WIDTH, D), lambda i: (layer, 0, 0, 0)),
            pl.BlockSpec((None, D, D), lay3),
            pl.BlockSpec((1, D), lambda i: (0, 0)),
            pl.BlockSpec((1, D), lambda i: (0, 0)),
        ],
        out_specs=pl.BlockSpec((tm, D), lambda i: (i, 0)),
        out_shape=jax.ShapeDtypeStruct((T, D), F32),
        compiler_params=_params(1),
        name="merge_sublayer",
    )(x2, *branches, wg, bg, wb, wo, g, b)


def _mem_kv_kernel(mem_ref, wk_ref, wvt_ref, kp_ref, vt_ref):
    mem = mem_ref[0].astype(BF16)
    k = _nn(mem, wk_ref[...])
    no_bias = jnp.zeros((k.shape[0], HEAD_DIM), F32)
    for h in range(N_HEADS):
        pair, parity = divmod(h, 2)
        kp_ref[0, h] = _stage_operand(k[:, pair * PAIR_WIDTH:(pair + 1) * PAIR_WIDTH], parity, no_bias)
    _stage_vt(vt_ref.at[0], _nt(wvt_ref[...], mem))


def _mem_kv(mem, w_k, w_vt, layer):
    B, M, D = mem.shape
    lay3 = lambda i: (layer, 0, 0)
    return pl.pallas_call(
        _mem_kv_kernel,
        grid=(B,),
        in_specs=[pl.BlockSpec((1, M, D), lambda i: (i, 0, 0)),
                  pl.BlockSpec((None, D, BRANCH_WIDTH), lay3),
                  pl.BlockSpec((None, BRANCH_WIDTH, D), lay3)],
        out_specs=[pl.BlockSpec((1, N_HEADS, M, QK_WIDTH), lambda i: (i, 0, 0, 0)),
                   pl.BlockSpec((1, N_HEADS, VT_ROWS, M), lambda i: (i, 0, 0, 0))],
        out_shape=[jax.ShapeDtypeStruct((B, N_HEADS, M, QK_WIDTH), BF16),
                   jax.ShapeDtypeStruct((B, N_HEADS, VT_ROWS, M), BF16)],
        compiler_params=_params(1),
        name="mem_kv",
    )(mem, w_k, w_vt)


def _xattn_kernel(x_ref, kp_ref, vt_ref, wq_ref, wo_ref, g_ref, b_ref, o_ref):
    rows = x_ref.shape[1] // XATTN_SPLIT
    xs = [x_ref[0, i * rows:(i + 1) * rows, :] for i in range(XATTN_SPLIT)]
    qs = [_nn(x.astype(BF16), wq_ref[...]) * (HEAD_DIM ** -0.5 * LOG2E) for x in xs]
    no_bias = jnp.zeros((rows, HEAD_DIM), F32)

    def logits_t(q, h):
        pair, parity = divmod(h, 2)
        return _nt(kp_ref[0, h], _stage_operand(q[:, pair * PAIR_WIDTH:(pair + 1) * PAIR_WIDTH], parity, no_bias))

    ss = [logits_t(q, 0) for q in qs]
    heads = [[] for _ in qs]
    for h in range(N_HEADS):
        ss_next = [logits_t(q, h + 1) for q in qs] if h + 1 < N_HEADS else None
        for i, s in enumerate(ss):
            p = jnp.exp2(s - jnp.max(s, axis=0, keepdims=True)).astype(BF16)
            res = _nn(vt_ref[0, h], p)
            heads[i].append(res[:HEAD_DIM] / res[HEAD_DIM:HEAD_DIM + 1])
        ss = ss_next
    atts = [jnp.concatenate(hs, axis=0).T.astype(BF16) for hs in heads]
    outs = [_layer_norm(ALPHA * x + _nn(att, wo_ref[...]), g_ref[...], b_ref[...]) for x, att in zip(xs, atts)]
    o_ref[0] = jnp.concatenate(outs, axis=0)


def _xattn_sublayer(x, kp, vt, wq, wo, g, b, layer):
    B, S, D = x.shape
    M = kp.shape[2]
    tm = XATTN_TILE
    lay3 = lambda i, j: (layer, 0, 0)
    return pl.pallas_call(
        _xattn_kernel,
        grid=(B, S // tm),
        in_specs=[
            pl.BlockSpec((1, tm, D), lambda i, j: (i, j, 0)),
            pl.BlockSpec((1, N_HEADS, M, QK_WIDTH), lambda i, j: (i, 0, 0, 0)),
            pl.BlockSpec((1, N_HEADS, VT_ROWS, M), lambda i, j: (i, 0, 0, 0)),
            pl.BlockSpec((None, D, BRANCH_WIDTH), lay3),
            pl.BlockSpec((None, BRANCH_WIDTH, D), lay3),
            pl.BlockSpec((1, D), lambda i, j: (0, 0)),
            pl.BlockSpec((1, D), lambda i, j: (0, 0)),
        ],
        out_specs=pl.BlockSpec((1, tm, D), lambda i, j: (i, j, 0)),
        out_shape=jax.ShapeDtypeStruct((B, S, D), F32),
        compiler_params=_params(2),
        name="xattn_sublayer",
    )(x, kp, vt, wq, wo, g, b)


def _pad_cols(w, n):
    return jnp.pad(w, [(0, 0)] * (w.ndim - 1) + [(0, n - w.shape[-1])])


def _split_w_in(w_in, b_in):
    hd, bw = HEAD_DIM, BRANCH_WIDTH
    sizes = (bw, bw, bw, bw, hd, hd, hd, hd, hd, hd, 3 * N_HEADS, bw, bw, bw, N_HEADS,
             MLA_Q_RANK, MLA_KV_RANK, MLA_ROPE, N_BRANCH * D_MODEL)
    names = ("a_q", "a_k", "a_v", "b_q", "b_kc", "b_vc", "b_ks", "b_vs", "b_kw", "b_vw", "b_g",
             "c_q", "c_k", "c_v", "c_f", "d_cq", "d_ckv", "d_kr", "g_merge")
    w, b, off = {}, {}, 0
    for name, size in zip(names, sizes):
        w[name] = w_in[:, :, off:off + size]
        b[name] = b_in[:, off:off + size]
        off += size
    return w, b


def _row_major(ws, bs, names, pad_to):
    w = _pad_cols(jnp.concatenate([ws[n] for n in names], axis=-1), pad_to).astype(BF16)
    b = _pad_cols(jnp.concatenate([bs[n] for n in names], axis=-1), pad_to)[:, None, :]
    return w, b


def _channel_major(ws, bs, names, pad_to):
    w = _pad_cols(jnp.concatenate([ws[n] for n in names], axis=-1), pad_to).transpose(0, 2, 1).astype(BF16)
    b = _pad_cols(jnp.concatenate([bs[n] for n in names], axis=-1), pad_to)[:, :, None]
    return w, b


def _rope_slab(a, swap):
    half = MLA_ROPE // 2
    first, second = a[..., :half], a[..., half:]
    body = jnp.concatenate([second, first] if swap else [first, second], axis=-1)
    return jnp.pad(body, [(0, 0)] * (a.ndim - 1) + [(MLA_NOPE, LANES - MLA_NOPE - MLA_ROPE)])


def _mla_weights(ws, bs, mla_w_uq, mla_w_ukv):
    L = mla_w_uq.shape[0]
    half = MLA_ROPE // 2
    def key_slab(a):
        return jnp.concatenate([_rope_slab(a, False)[..., :MLA_NOPE + MLA_ROPE], _rope_slab(a, True)[..., MLA_NOPE:MLA_NOPE + MLA_ROPE]],
                               axis=-1)

    w = jnp.concatenate([ws["d_cq"], ws["d_ckv"], key_slab(ws["d_kr"])], axis=-1).astype(BF16)
    b = jnp.concatenate([bs["d_cq"], bs["d_ckv"], key_slab(bs["d_kr"])], axis=-1)[:, None, :]
    uq = mla_w_uq.reshape(L, MLA_Q_RANK, N_HEADS, MLA_NOPE + MLA_ROPE).transpose(0, 2, 1, 3)
    wq = jnp.concatenate([uq[..., :MLA_NOPE], _rope_slab(uq[..., MLA_NOPE:], False)[..., MLA_NOPE:]],
                         axis=-1).astype(BF16)
    wqs = jnp.concatenate([jnp.zeros_like(uq[..., :MLA_NOPE]), _rope_slab(uq[..., MLA_NOPE:], True)[..., MLA_NOPE:]],
                          axis=-1).astype(BF16)
    ukv = mla_w_ukv.reshape(L, MLA_KV_RANK, N_HEADS, MLA_NOPE + HEAD_DIM).transpose(0, 2, 1, 3)
    wk = _pad_cols(ukv[..., :MLA_NOPE], QK_WIDTH).astype(BF16)
    wvt = ukv[..., MLA_NOPE:].transpose(0, 1, 3, 2).reshape(L, BRANCH_WIDTH, MLA_KV_RANK).astype(BF16)
    inv_freq = ROPE_BASE ** (-jnp.arange(0, MLA_ROPE, 2, dtype=F32) / MLA_ROPE)
    freq_row = _rope_slab(jnp.concatenate([inv_freq, inv_freq])[None, :], False)
    sign_row = _rope_slab(jnp.concatenate([-jnp.ones((half,), F32), jnp.ones((half,), F32)])[None, :], False)
    def by_pair(a):
        return jnp.concatenate([a[:, 0::2], a[:, 1::2]], axis=-1)

    return w, b, by_pair(wq), by_pair(wqs), by_pair(wk), wvt, freq_row, sign_row


def kernel(x, mem, positions, ln_g, ln_b, ffn1_w_in, ffn1_w_out, ffn2_w_in, ffn2_w_out, w_in, b_in, w_branch, w_o,
           mla_q_norm, mla_w_uq, mla_kv_norm, mla_w_ukv, nsa_cmp_pos, nsa_cmp_w1, nsa_cmp_w2, xa_w_q, xa_w_kv, xa_w_o):
    B, S, D = x.shape
    L = ln_g.shape[0]
    ffn1 = (ffn1_w_in.astype(BF16), ffn1_w_out.astype(BF16))
    ffn2 = (ffn2_w_in.astype(BF16), ffn2_w_out.astype(BF16))

    ws, bs = _split_w_in(w_in.astype(BF16), b_in)
    moba_w, moba_b = _row_major(ws, bs, ("a_q", "a_k"), 2 * BRANCH_WIDTH)
    moba_wt, moba_bt = _channel_major(ws, bs, ("a_v",), BRANCH_WIDTH)
    nsa_w, nsa_b = _row_major(ws, bs, ("b_q", "b_ks", "b_kw", "b_kc", "b_vc"), 2 * BRANCH_WIDTH)
    nsa_wt, nsa_bt = _channel_major(ws, bs, ("b_vs", "b_vw", "b_g"), 2 * HEAD_DIM + NSA_GATE_ROWS)
    fox_w, fox_b = _row_major(ws, bs, ("c_q", "c_k"), 2 * BRANCH_WIDTH)
    fox_wt, fox_bt = _channel_major(ws, bs, ("c_v", "c_f"), BRANCH_WIDTH + 2 * FOX_F_ROWS)

    mla_w, mla_b, mla_wq, mla_wqs, mla_wk, mla_wvt, freq_row, sign_row = _mla_weights(ws, bs, mla_w_uq, mla_w_ukv)
    cos, sin = _rope_tables(positions, freq_row, sign_row)

    merge_wg = ws["g_merge"].astype(BF16)
    merge_bg = bs["g_merge"][:, None, :]
    merge_wb = w_branch.astype(BF16)
    merge_wo = w_o.astype(BF16)
    nsa_w1 = nsa_cmp_w1.reshape(L, 2, NSA_CMP_LEN, HEAD_DIM, NSA_CMP_HIDDEN).astype(BF16)
    nsa_w2 = nsa_cmp_w2.astype(BF16)
    xa_wq, xa_wo = xa_w_q.astype(BF16), xa_w_o.astype(BF16)
    xa_wk = xa_w_kv[:, :, :BRANCH_WIDTH].astype(BF16)
    xa_wvt = xa_w_kv[:, :, BRANCH_WIDTH:].transpose(0, 2, 1).astype(BF16)

    for l in range(L):
        ln = lambda i: (ln_g[l, i][None, :], ln_b[l, i][None, :])
        x2, x16 = _ffn_sublayer(x.reshape(B * S, D), *ffn1, *ln(0), l, True)
        x16 = x16.reshape(B, S, D)
        branches = (
            _moba(x16, moba_w, moba_b, moba_wt, moba_bt, l),
            _nsa(x16, nsa_w, nsa_b, nsa_wt, nsa_bt, nsa_cmp_pos, nsa_w1, nsa_w2, l),
            _fox(x16, fox_w, fox_b, fox_wt, fox_bt, l),
            _mla(x16, cos, sin, mla_w, mla_b, mla_q_norm[:, None, :], mla_kv_norm[:, None, :],
                 mla_wq, mla_wqs, mla_wk, mla_wvt, l),
        )
        x2 = _merge_sublayer(x2, [o.reshape(B * S, BRANCH_WIDTH) for o in branches],
                             merge_wg, merge_bg, merge_wb, merge_wo, *ln(1), l)
        x = _xattn_sublayer(x2.reshape(B, S, D), *_mem_kv(mem, xa_wk, xa_wvt, l), xa_wq, xa_wo, *ln(2), l)
        x = _ffn_sublayer(x.reshape(B * S, D), *ffn2, *ln(3), l, False)[0].reshape(B, S, D)
    return x
```

```python
import functools

import jax
import jax.numpy as jnp
import numpy as np
from jax import lax
from jax.experimental import pallas as pl
from jax.experimental.pallas import tpu as pltpu

F32 = jnp.float32
BF16 = jnp.bfloat16

D_MODEL = 1024
DEPTH = 4
HEAD_DIM = 64
N_HEADS = 4
BRANCH_WIDTH = N_HEADS * HEAD_DIM
N_BRANCH = 4
MOBA_BLOCK = 256
MOBA_TOPK = 3
NSA_CMP_LEN = 32
NSA_CMP_STRIDE = 16
NSA_CMP_HIDDEN = 256
NSA_SLC_BLOCK = 64
NSA_TOP_N = 16
NSA_WINDOW = 512
NSA_FORCE_BONUS = 1.0e4
MLA_Q_RANK = 256
MLA_KV_RANK = 128
MLA_NOPE = 64
MLA_ROPE = 32
ROPE_BASE = 10000.0
MEM_LEN = 256
D_FF = 2816
ALPHA = (2 * DEPTH) ** 0.25
LN_EPS = 1e-5
RMS_EPS = 1e-6
NEG_INF = -1e30
MOBA_SLOPES = tuple(2.0 ** -(2 * h + 1) for h in range(N_HEADS))
NSA_SLOPES = tuple(2.0 ** -(2 * h + 2) for h in range(N_HEADS))

LOG2E = 1.4426950408889634
VT_ROWS = HEAD_DIM + 16
LANES = 128
TQ = 256
TK = 256
QK_WIDTH = 128
FOX_F_ROWS = 8
NSA_GATE_ROWS = 16
PAIR_WIDTH = 2 * HEAD_DIM
ROW_CHUNK = 512
FFN_TILE_F = 256
STATIC_HEADS_MAX_TILES = 6
TOKEN_TILE = 512
XATTN_TILE = 1024
XATTN_SPLIT = 4
VMEM_LIMIT = 56 * 1024 * 1024


def _nn(a, b):
    return jnp.dot(a, b, preferred_element_type=F32)


def _nt(a, b):
    return lax.dot_general(a, b, (((1,), (1,)), ((), ())), preferred_element_type=F32)


def _layer_norm(z, g, b):
    mu = jnp.mean(z, -1, keepdims=True)
    d = z - mu
    var = jnp.mean(d * d, -1, keepdims=True)
    return d * lax.rsqrt(var + LN_EPS) * g + b


def _params(n_parallel):
    return pltpu.CompilerParams(dimension_semantics=("parallel",) * n_parallel,
                                vmem_limit_bytes=VMEM_LIMIT)


def _ffn_kernel(x_ref, wi_ref, wo_ref, g_ref, b_ref, o_ref, *o16_ref):
    x = x_ref[...]
    xb = x.astype(BF16)
    acc = None
    for c in range(D_FF // FFN_TILE_F):
        lo = c * FFN_TILE_F
        hg = _nn(xb, wi_ref[:, lo:lo + FFN_TILE_F])
        hu = _nn(xb, wi_ref[:, D_FF + lo:D_FF + lo + FFN_TILE_F])
        a = (hg * jax.nn.sigmoid(hg) * hu).astype(BF16)
        part = _nn(a, wo_ref[lo:lo + FFN_TILE_F, :])
        acc = part if acc is None else acc + part
    out = _layer_norm(ALPHA * x + 0.5 * acc, g_ref[...], b_ref[...])
    o_ref[...] = out
    for ref in o16_ref:
        ref[...] = out.astype(BF16)


def _ffn_sublayer(x2, wi, wo, g, b, layer, with_bf16_copy):
    T, D = x2.shape
    tm = TOKEN_TILE
    out_spec = pl.BlockSpec((tm, D), lambda i: (i, 0))
    dtypes = (F32, BF16) if with_bf16_copy else (F32,)
    return pl.pallas_call(
        _ffn_kernel,
        grid=(T // tm,),
        in_specs=[
            pl.BlockSpec((tm, D), lambda i: (i, 0)),
            pl.BlockSpec((None, D, 2 * D_FF), lambda i: (layer, 0, 0)),
            pl.BlockSpec((None, D_FF, D), lambda i: (layer, 0, 0)),
            pl.BlockSpec((1, D), lambda i: (0, 0)),
            pl.BlockSpec((1, D), lambda i: (0, 0)),
        ],
        out_specs=[out_spec] * len(dtypes),
        out_shape=[jax.ShapeDtypeStruct((T, D), dt) for dt in dtypes],
        compiler_params=_params(1),
        name="ffn_sublayer",
    )(x2, wi, wo, g, b)


def _tile_iotas():
    r = lax.broadcasted_iota(jnp.int32, (TK, TQ), 0)
    c = lax.broadcasted_iota(jnp.int32, (TK, TQ), 1)
    return r, c


def _scores(s_ref, n_tiles, score_tile, h):
    m = None
    for j in range(n_tiles):
        s = score_tile(h, j)
        s_ref[j * TK:(j + 1) * TK, :] = s
        mj = jnp.max(s, axis=0, keepdims=True)
        m = mj if m is None else jnp.maximum(m, mj)
    return m


def _weighted_values(s_ref, n_tiles, m, vt_cols):
    p_all = jnp.concatenate([jnp.exp2(s_ref[j * TK:(j + 1) * TK, :] - m).astype(BF16) for j in range(n_tiles)],
                            axis=0)
    res = _nn(vt_cols, p_all)
    return res[:HEAD_DIM] / jnp.maximum(res[HEAD_DIM:HEAD_DIM + 1], 1e-30)


def _attend_heads(n_tiles, score_tile, vt_cols, out_ref, s_refs, first=None, ahead_of_last=None):
    if n_tiles <= STATIC_HEADS_MAX_TILES:
        queue = [first if first is not None else [score_tile(0, j) for j in range(n_tiles)],
                 [score_tile(1, j) for j in range(n_tiles)]]
        last = None
        for h in range(N_HEADS):
            if h + 2 < N_HEADS:
                queue.append([score_tile(h + 2, j) for j in range(n_tiles)])
            elif h + 2 == N_HEADS:
                last = ahead_of_last() if ahead_of_last is not None else None
            tiles = queue[h]
            m = functools.reduce(jnp.maximum, [jnp.max(s, axis=0, keepdims=True) for s in tiles])
            p_all = jnp.concatenate([jnp.exp2(s - m).astype(BF16) for s in tiles], axis=0)
            res = _nn(vt_cols(h, n_tiles), p_all)
            out_ref[h] = res[:HEAD_DIM] / jnp.maximum(res[HEAD_DIM:HEAD_DIM + 1], 1e-30)
        return last
    assert first is None, "the loop path computes its own first logits"
    s_a, s_b = s_refs
    m_first = _scores(s_a, n_tiles, score_tile, 0)

    def pair(i, m_a):
        h_a, h_b = 2 * i, 2 * i + 1
        h_next = jnp.minimum(2 * i + 2, N_HEADS - 1)
        m_b = _scores(s_b, n_tiles, score_tile, h_b)
        out_ref[h_a] = _weighted_values(s_a, n_tiles, m_a, vt_cols(h_a, n_tiles))
        m_next = _scores(s_a, n_tiles, score_tile, h_next)
        out_ref[h_b] = _weighted_values(s_b, n_tiles, m_b, vt_cols(h_b, n_tiles))
        return m_next

    lax.fori_loop(0, N_HEADS // 2, pair, m_first)
    return ahead_of_last() if ahead_of_last is not None else None


def _position_columns(n_rows, first, step):
    row = lax.broadcasted_iota(jnp.int32, (n_rows, HEAD_DIM), 0)
    lane = lax.broadcasted_iota(jnp.int32, (n_rows, HEAD_DIM), 1)
    pos = first + step * row
    hi = ((pos >> 7) << 7).astype(F32)
    lo = (pos & 127).astype(F32)
    return jnp.where(lane < 6, jnp.where((lane & 1) == 0, hi, lo), 0.0)


def _bf16_pieces(value):
    hi = float(np.asarray(value, np.float32).astype(BF16))
    mid = float(np.asarray(value - hi, np.float32).astype(BF16))
    return hi, mid, value - hi - mid


def _slope_columns(n_rows, slope):
    lane = lax.broadcasted_iota(jnp.int32, (n_rows, HEAD_DIM), 1)
    hi, mid, rest = _bf16_pieces(slope * LOG2E)
    return jnp.where(lane < 2, hi, jnp.where(lane < 4, mid, jnp.where(lane < 6, rest, 0.0))).astype(F32)


def _ones_columns(n_rows, n_cols):
    lane = lax.broadcasted_iota(jnp.int32, (n_rows, HEAD_DIM), 1)
    return jnp.where(lane < n_cols, 1.0, 0.0).astype(F32)


def _half_mask(n_rows, parity):
    lane = lax.broadcasted_iota(jnp.int32, (n_rows, PAIR_WIDTH), 1)
    return (lane // HEAD_DIM) == parity


def _bias_slab(cols, parity):
    zeros = jnp.zeros_like(cols)
    return jnp.concatenate([cols, zeros] if parity == 0 else [zeros, cols], axis=1)


def _stage_operand(pair_slab, parity, bias_cols, scale=None):
    feats = pair_slab if scale is None else pair_slab * scale
    own = _half_mask(pair_slab.shape[0], parity)
    return jnp.where(own, feats, _bias_slab(bias_cols, 1 - parity)).astype(BF16)


def _split3(a):
    hi = a.astype(BF16)
    rest = a - hi.astype(F32)
    mid = rest.astype(BF16)
    return hi, mid, (rest - mid.astype(F32)).astype(BF16)


def _project(x_ref, w_ref, b_ref, wt_ref, bt_ref, rm_ref, cm_ref):
    S = x_ref.shape[1]
    for i in range(S // ROW_CHUNK):
        rows = slice(i * ROW_CHUNK, (i + 1) * ROW_CHUNK)
        xb = x_ref[0, rows, :]
        rm_ref[rows, :] = _nn(xb, w_ref[...]) + b_ref[...]
        cm_ref[:, rows] = _nt(wt_ref[...], xb) + bt_ref[...]


def _stage_vt(vt_ref, v_t):
    G, _, S = vt_ref.shape
    vt_ref[:, 0:HEAD_DIM, :] = v_t.astype(BF16).reshape(G, HEAD_DIM, S)
    row = lax.broadcasted_iota(jnp.int32, (G, VT_ROWS - HEAD_DIM, S), 1)
    vt_ref[:, HEAD_DIM:VT_ROWS, :] = jnp.where(row == 0, 1.0, 0.0).astype(BF16)


def _causal_attention(o_ref, qp_ref, kp_ref, vt_ref, ot_ref, s_refs):
    S = qp_ref.shape[1]
    r, c = _tile_iotas()
    for qi in range(S // TQ):
        q0 = qi * TQ

        def score_tile(h, j, qi=qi, q0=q0):
            s = _nt(kp_ref[h, j * TK:(j + 1) * TK, :], qp_ref[h, q0:q0 + TQ, :])
            return jnp.where(r <= c, s, NEG_INF) if j == qi else s

        _attend_heads(qi + 1, score_tile, lambda h, n: vt_ref[h, :, 0:n * TK], ot_ref, s_refs)
        _store_heads(o_ref, q0, ot_ref[...].reshape(BRANCH_WIDTH, TQ))


def _store_heads(o_ref, q0, heads_t):
    o_ref[0, q0:q0 + TQ, :] = heads_t.T


def _attention_scratch(S):
    return [pltpu.VMEM((N_HEADS, HEAD_DIM, TQ), F32), pltpu.VMEM((S, TQ), F32), pltpu.VMEM((S, TQ), F32)]


def _moba_kernel(x_ref, w_ref, b_ref, wt_ref, bt_ref, o_ref, rm_ref, cm_ref, qp_ref, kp_ref, vt_ref, sel_ref,
                 ot_ref, sa_ref, sb_ref):
    S = x_ref.shape[1]
    nb = S // MOBA_BLOCK
    _project(x_ref, w_ref, b_ref, wt_ref, bt_ref, rm_ref, cm_ref)
    _stage_vt(vt_ref, cm_ref[...])
    kcols = _position_columns(S, 0, 1)
    blk = lax.broadcasted_iota(jnp.int32, (nb, S), 0)
    own = lax.broadcasted_iota(jnp.int32, (nb, S), 1) // MOBA_BLOCK
    cand = blk < own
    pair_gates = []
    for pair in range(N_HEADS // 2):
        q_pair = rm_ref[:, pair * PAIR_WIDTH:(pair + 1) * PAIR_WIDTH]
        k_pair = rm_ref[:, BRANCH_WIDTH + pair * PAIR_WIDTH:BRANCH_WIDTH + (pair + 1) * PAIR_WIDTH]
        for parity in range(2):
            h = 2 * pair + parity
            qp_ref[h] = _stage_operand(q_pair, parity, _slope_columns(S, MOBA_SLOPES[h]), HEAD_DIM ** -0.5 * LOG2E)
            kp_ref[h] = _stage_operand(k_pair, parity, kcols)
        k_mean = jnp.sum(k_pair.reshape(nb, MOBA_BLOCK, PAIR_WIDTH), axis=1) * (1.0 / MOBA_BLOCK)
        k_means = jnp.concatenate([jnp.where(_half_mask(nb, parity), k_mean, 0.0) for parity in range(2)], axis=0)
        km_hi, km_mid, _ = _split3(k_means)
        q_hi, q_mid, _ = _split3(q_pair)
        pair_gates.append(_nt(km_hi, q_hi) + _nt(km_hi, q_mid) + _nt(km_mid, q_hi))
    for h in range(N_HEADS):
        pair, parity = divmod(h, 2)
        gate = jnp.where(cand, pair_gates[pair][parity * nb:(parity + 1) * nb, :], NEG_INF)
        beaten = jnp.zeros((nb, S), F32)
        for j in range(nb):
            other = gate[j:j + 1, :]
            wins = (other > gate) | ((other == gate) & (blk > j))
            beaten = beaten + jnp.where(wins, 1.0, 0.0)
        sel = jnp.where(cand & (beaten < MOBA_TOPK), 1.0, 0.0)
        for qi in range(S // TQ):
            sel_ref[h, qi] = sel[:, qi * TQ:(qi + 1) * TQ]

    r, c = _tile_iotas()
    for qi in range(S // TQ):
        q0 = qi * TQ

        def score_tile(h, j, qi=qi, q0=q0):
            s = _nt(kp_ref[h, j * TK:(j + 1) * TK, :], qp_ref[h, q0:q0 + TQ, :])
            return jnp.where(r <= c if j == qi else sel_ref[h, qi, j:j + 1, :] > 0.5, s, NEG_INF)

        _attend_heads(qi + 1, score_tile, lambda h, n: vt_ref[h, :, 0:n * TK], ot_ref, (sa_ref, sb_ref))
        _store_heads(o_ref, q0, ot_ref[...].reshape(BRANCH_WIDTH, TQ))


def _moba(x, w, b, wt, bt, layer):
    B, S, D = x.shape
    n_rm = w.shape[-1]
    n_cm = wt.shape[1]
    nb = S // MOBA_BLOCK
    return pl.pallas_call(
        _moba_kernel,
        grid=(B,),
        in_specs=[
            pl.BlockSpec((1, S, D), lambda i: (i, 0, 0)),
            pl.BlockSpec((None, D, n_rm), lambda i: (layer, 0, 0)),
            pl.BlockSpec((None, 1, n_rm), lambda i: (layer, 0, 0)),
            pl.BlockSpec((None, n_cm, D), lambda i: (layer, 0, 0)),
            pl.BlockSpec((None, n_cm, 1), lambda i: (layer, 0, 0)),
        ],
        out_specs=pl.BlockSpec((1, S, BRANCH_WIDTH), lambda i: (i, 0, 0)),
        out_shape=jax.ShapeDtypeStruct((B, S, BRANCH_WIDTH), F32),
        scratch_shapes=[
            pltpu.VMEM((S, n_rm), F32),
            pltpu.VMEM((n_cm, S), F32),
            pltpu.VMEM((N_HEADS, S, QK_WIDTH), BF16),
            pltpu.VMEM((N_HEADS, S, QK_WIDTH), BF16),
            pltpu.VMEM((N_HEADS, VT_ROWS, S), BF16),
            pltpu.VMEM((N_HEADS, S // TQ, nb, TQ), F32),
        ] + _attention_scratch(S),
        compiler_params=_params(1),
        name="moba_mixer",
    )(x, w, b, wt, bt)


def _fox_kernel(x_ref, w_ref, b_ref, wt_ref, bt_ref, o_ref, rm_ref, cm_ref, qp_ref, kp_ref, vt_ref,
                ot_ref, sa_ref, sb_ref):
    S = x_ref.shape[1]
    _project(x_ref, w_ref, b_ref, wt_ref, bt_ref, rm_ref, cm_ref)
    _stage_vt(vt_ref, cm_ref[0:BRANCH_WIDTH, :])
    f_t = cm_ref[BRANCH_WIDTH:BRANCH_WIDTH + FOX_F_ROWS, :]
    log_sig = jnp.minimum(f_t, 0.0) - jnp.log1p(jnp.exp(-jnp.abs(f_t)))
    blk = MOBA_BLOCK
    tri = jnp.where(lax.broadcasted_iota(jnp.int32, (blk, blk), 0) <= lax.broadcasted_iota(jnp.int32, (blk, blk), 1),
                    1.0, 0.0).astype(BF16)
    stacked = jnp.concatenate([log_sig[:, i * blk:(i + 1) * blk] for i in range(S // blk)], axis=0)
    within = functools.reduce(jnp.add, [_nn(piece, tri) for piece in _split3(stacked)])
    carry = jnp.zeros((FOX_F_ROWS, 1), F32)
    pieces = []
    for i in range(S // blk):
        cs = within[i * FOX_F_ROWS:(i + 1) * FOX_F_ROWS, :] + carry
        pieces.append(cs)
        carry = cs[:, blk - 1:blk]
    key_bias_t = jnp.concatenate(pieces, axis=1) * (-LOG2E)
    key_bias = jnp.concatenate([key_bias_t, jnp.zeros((LANES - FOX_F_ROWS, S), F32)], axis=0).T
    lane = lax.broadcasted_iota(jnp.int32, (S, HEAD_DIM), 1)
    ones = _ones_columns(S, 3)
    for h in range(N_HEADS):
        pair, parity = divmod(h, 2)
        q_pair = rm_ref[:, pair * PAIR_WIDTH:(pair + 1) * PAIR_WIDTH]
        k_pair = rm_ref[:, BRANCH_WIDTH + pair * PAIR_WIDTH:BRANCH_WIDTH + (pair + 1) * PAIR_WIDTH]
        d = key_bias[:, h:h + 1]
        d_hi = d.astype(BF16).astype(F32)
        d_mid = (d - d_hi).astype(BF16).astype(F32)
        d_lo = d - d_hi - d_mid
        dcols = jnp.where(lane == 0, d_hi, jnp.where(lane == 1, d_mid, jnp.where(lane == 2, d_lo, 0.0)))
        qp_ref[h] = _stage_operand(q_pair, parity, ones, HEAD_DIM ** -0.5 * LOG2E)
        kp_ref[h] = _stage_operand(k_pair, parity, dcols)

    _causal_attention(o_ref, qp_ref, kp_ref, vt_ref, ot_ref, (sa_ref, sb_ref))


def _fox(x, w, b, wt, bt, layer):
    B, S, D = x.shape
    n_rm = w.shape[-1]
    n_cm = wt.shape[1]
    return pl.pallas_call(
        _fox_kernel,
        grid=(B,),
        in_specs=[
            pl.BlockSpec((1, S, D), lambda i: (i, 0, 0)),
            pl.BlockSpec((None, D, n_rm), lambda i: (layer, 0, 0)),
            pl.BlockSpec((None, 1, n_rm), lambda i: (layer, 0, 0)),
            pl.BlockSpec((None, n_cm, D), lambda i: (layer, 0, 0)),
            pl.BlockSpec((None, n_cm, 1), lambda i: (layer, 0, 0)),
        ],
        out_specs=pl.BlockSpec((1, S, BRANCH_WIDTH), lambda i: (i, 0, 0)),
        out_shape=jax.ShapeDtypeStruct((B, S, BRANCH_WIDTH), F32),
        scratch_shapes=[
            pltpu.VMEM((S, n_rm), F32),
            pltpu.VMEM((n_cm, S), F32),
            pltpu.VMEM((N_HEADS, S, QK_WIDTH), BF16),
            pltpu.VMEM((N_HEADS, S, QK_WIDTH), BF16),
            pltpu.VMEM((N_HEADS, VT_ROWS, S), BF16),
        ] + _attention_scratch(S),
        compiler_params=_params(1),
        name="fox_mixer",
    )(x, w, b, wt, bt)


def _rope_table_kernel(pos_ref, freq_ref, sign_ref, cos_ref, sin_ref):
    ang = pos_ref[0].astype(F32) * freq_ref[...]
    cos_ref[0] = jnp.cos(ang)
    sin_ref[0] = jnp.sin(ang) * sign_ref[...]


def _rope_tables(positions, freq_row, sign_row):
    B, S = positions.shape
    return pl.pallas_call(
        _rope_table_kernel,
        grid=(B,),
        in_specs=[
            pl.BlockSpec((1, S, 1), lambda i: (i, 0, 0)),
            pl.BlockSpec((1, LANES), lambda i: (0, 0)),
            pl.BlockSpec((1, LANES), lambda i: (0, 0)),
        ],
        out_specs=[pl.BlockSpec((1, S, LANES), lambda i: (i, 0, 0))] * 2,
        out_shape=[jax.ShapeDtypeStruct((B, S, LANES), F32)] * 2,
        compiler_params=_params(1),
        name="rope_tables",
    )(positions.reshape(B, S, 1), freq_row, sign_row)


def _rms_norm(x, g):
    return x * lax.rsqrt(jnp.mean(x * x, -1, keepdims=True) + RMS_EPS) * g


def _mla_kernel(x_ref, cos_ref, sin_ref, w_ref, b_ref, qn_ref, kvn_ref, wq_ref, wqs_ref, wk_ref, wvt_ref,
                o_ref, rm_ref, qp_ref, kp_ref, vt_ref, ot_ref, sa_ref, sb_ref):
    S = x_ref.shape[1]
    for i in range(S // ROW_CHUNK):
        rows = slice(i * ROW_CHUNK, (i + 1) * ROW_CHUNK)
        rm_ref[rows, :] = _nn(x_ref[0, rows, :], w_ref[...]) + b_ref[...]
    cos = cos_ref[0]
    sin = sin_ref[0]
    scale = (MLA_NOPE + MLA_ROPE) ** -0.5 * LOG2E
    q_lat = _rms_norm(rm_ref[:, 0:MLA_Q_RANK], qn_ref[...]).astype(BF16)
    kv_lat = _rms_norm(rm_ref[:, MLA_Q_RANK:MLA_Q_RANK + MLA_KV_RANK], kvn_ref[...]).astype(BF16)
    c0 = MLA_Q_RANK + MLA_KV_RANK
    kr = rm_ref[:, c0:c0 + LANES]
    k_rope = kr * cos + pltpu.roll(kr, LANES - MLA_ROPE, axis=1) * sin
    _stage_vt(vt_ref, _nt(wvt_ref[...], kv_lat))
    pairs = range(N_HEADS // 2)
    q_plain = [_nn(q_lat, wq_ref[p]) for p in pairs]
    q_twin = [_nn(q_lat, wqs_ref[p]) for p in pairs]
    k_nope = [_nn(kv_lat, wk_ref[p]) for p in pairs]
    for h in range(N_HEADS):
        p, lanes = h // 2, slice((h % 2) * QK_WIDTH, (h % 2 + 1) * QK_WIDTH)
        qp_ref[h] = ((q_plain[p][:, lanes] * cos + q_twin[p][:, lanes] * sin) * scale).astype(BF16)
        kp_ref[h] = (k_nope[p][:, lanes] + k_rope).astype(BF16)

    _causal_attention(o_ref, qp_ref, kp_ref, vt_ref, ot_ref, (sa_ref, sb_ref))


def _mla(x, cos, sin, w, b, qn, kvn, wq, wqs, wk, wvt, layer):
    B, S, D = x.shape
    n_rm = w.shape[-1]
    lay3 = lambda i: (layer, 0, 0)
    lay4 = lambda i: (layer, 0, 0, 0)
    return pl.pallas_call(
        _mla_kernel,
        grid=(B,),
        in_specs=[
            pl.BlockSpec((1, S, D), lambda i: (i, 0, 0)),
            pl.BlockSpec((1, S, LANES), lambda i: (i, 0, 0)),
            pl.BlockSpec((1, S, LANES), lambda i: (i, 0, 0)),
            pl.BlockSpec((None, D, n_rm), lay3),
            pl.BlockSpec((None, 1, n_rm), lay3),
            pl.BlockSpec((None, 1, MLA_Q_RANK), lay3),
            pl.BlockSpec((None, 1, MLA_KV_RANK), lay3),
            pl.BlockSpec((None, N_HEADS // 2, MLA_Q_RANK, 2 * QK_WIDTH), lay4),
            pl.BlockSpec((None, N_HEADS // 2, MLA_Q_RANK, 2 * QK_WIDTH), lay4),
            pl.BlockSpec((None, N_HEADS // 2, MLA_KV_RANK, 2 * QK_WIDTH), lay4),
            pl.BlockSpec((None, BRANCH_WIDTH, MLA_KV_RANK), lay3),
        ],
        out_specs=pl.BlockSpec((1, S, BRANCH_WIDTH), lambda i: (i, 0, 0)),
        out_shape=jax.ShapeDtypeStruct((B, S, BRANCH_WIDTH), F32),
        scratch_shapes=[
            pltpu.VMEM((S, n_rm), F32),
            pltpu.VMEM((N_HEADS, S, QK_WIDTH), BF16),
            pltpu.VMEM((N_HEADS, S, QK_WIDTH), BF16),
            pltpu.VMEM((N_HEADS, VT_ROWS, S), BF16),
        ] + _attention_scratch(S),
        compiler_params=_params(1),
        name="mla_mixer",
    )(x, cos, sin, w, b, qn, kvn, wq, wqs, wk, wvt)


N_CMP_PAD = 128


def _nsa_compress(src_ref, pos_ref, w1_ref, w2_ref):
    accs = [jnp.zeros((N_CMP_PAD, NSA_CMP_HIDDEN), F32) for _ in range(2)]
    n_cmp = N_CMP_PAD - 1
    for p in range(NSA_CMP_LEN):
        rows = src_ref[pl.ds(p, n_cmp, stride=NSA_CMP_STRIDE), :]
        for which in range(2):
            part = rows[:, which * HEAD_DIM:(which + 1) * HEAD_DIM] + pos_ref[which, p:p + 1, :]
            part = jnp.concatenate([part, jnp.zeros((1, HEAD_DIM), F32)], axis=0)
            accs[which] = accs[which] + _nn(part.astype(BF16), w1_ref[which, p])
    return [_nn(jax.nn.gelu(accs[which]).astype(BF16), w2_ref[which]) for which in range(2)]


def _nsa_kernel(x_ref, w_ref, b_ref, wt_ref, bt_ref, pos_ref, w1_ref, w2_ref, o_ref,
                rm_ref, cm_ref, cin_ref, qp_ref, ks_ref, kw_ref, kc_ref, vt_ref, vtc_ref, sel_ref, gate_ref,
                oslc_ref, owin_ref, sa_ref, sb_ref):
    S = x_ref.shape[1]
    n_slc = S // NSA_SLC_BLOCK
    _project(x_ref, w_ref, b_ref, wt_ref, bt_ref, rm_ref, cm_ref)
    _stage_vt(vt_ref, cm_ref[0:2 * HEAD_DIM, :])
    gate_ref[...] = jax.nn.sigmoid(cm_ref[2 * HEAD_DIM:2 * HEAD_DIM + NSA_GATE_ROWS, :])
    c_kk, c_kc = BRANCH_WIDTH, BRANCH_WIDTH + PAIR_WIDTH
    kcols = _position_columns(S, 0, 1)
    for h in range(N_HEADS):
        pair, parity = divmod(h, 2)
        qp_ref[h] = _stage_operand(rm_ref[:, pair * PAIR_WIDTH:(pair + 1) * PAIR_WIDTH], parity,
                                   _slope_columns(S, NSA_SLOPES[h]), HEAD_DIM ** -0.5 * LOG2E)
    slab = rm_ref[:, c_kk:c_kk + PAIR_WIDTH]
    swapped = pltpu.roll(slab, HEAD_DIM, axis=1)
    ks_ref[0] = _stage_operand(slab, 0, kcols)
    ks_ref[1] = _stage_operand(swapped, 1, kcols)
    kw_ref[0] = _stage_operand(swapped, 0, kcols)
    kw_ref[1] = _stage_operand(slab, 1, kcols)
    cin_ref[...] = rm_ref[:, c_kc:c_kc + 2 * HEAD_DIM]
    k_cmp, v_cmp = _nsa_compress(cin_ref, pos_ref, w1_ref, w2_ref)
    kv_cmp = jnp.concatenate([k_cmp, v_cmp], axis=1)
    ccols = _position_columns(N_CMP_PAD, NSA_CMP_LEN - 1, NSA_CMP_STRIDE)
    kc_ref[0] = _stage_operand(kv_cmp, 0, ccols)
    kc_ref[1] = _stage_operand(pltpu.roll(kv_cmp, HEAD_DIM, axis=1), 1, ccols)
    vtc_ref[...] = kv_cmp.T[HEAD_DIM:, :].astype(BF16)

    jj = lax.broadcasted_iota(jnp.int32, (n_slc, N_CMP_PAD), 0) * NSA_SLC_BLOCK
    cc = lax.broadcasted_iota(jnp.int32, (n_slc, N_CMP_PAD), 1) * NSA_CMP_STRIDE
    overlap_t = jnp.where((cc < jj + NSA_SLC_BLOCK) & (cc + NSA_CMP_LEN > jj), 1.0, 0.0).astype(BF16)

    r, c = _tile_iotas()
    per_tile = TK // NSA_SLC_BLOCK
    win_tiles = NSA_WINDOW // TK
    c_idx = lax.broadcasted_iota(jnp.int32, (N_CMP_PAD, TQ), 0)
    lane_cmp = lax.broadcasted_iota(jnp.int32, (N_CMP_PAD, TQ), 1)
    for qi in range(S // TQ):
        q0 = qi * TQ
        cmp_ok = (c_idx * NSA_CMP_STRIDE + (NSA_CMP_LEN - 1) <= q0 + lane_cmp) & (c_idx < N_CMP_PAD - 1)
        heads = range(N_HEADS)
        ss = [jnp.where(cmp_ok, _nt(kc_ref[h % 2], qp_ref[h, q0:q0 + TQ, :]), NEG_INF) for h in heads]
        es = [jnp.where(cmp_ok, jnp.exp2(s - jnp.max(s, axis=0, keepdims=True)), 0.0) for s in ss]
        ps = [e / jnp.maximum(jnp.sum(e, axis=0, keepdims=True), 1e-30) for e in es]
        o_cmp = [_nn(vtc_ref[...], p.astype(BF16)) for p in ps]
        imp = functools.reduce(jnp.add, [_nn(overlap_t, piece) for piece in _split3(functools.reduce(jnp.add, ps))])
        n_seen = per_tile * (qi + 1)
        select = n_seen > NSA_TOP_N
        if select:
            j_idx = lax.broadcasted_iota(jnp.int32, (n_seen, TQ), 0)
            own = (q0 + lax.broadcasted_iota(jnp.int32, (n_seen, TQ), 1)) // NSA_SLC_BLOCK
            forced = (j_idx == 0) | (j_idx == own) | (j_idx == own - 1)
            score = jnp.where(j_idx <= own, imp[:n_seen] + NSA_FORCE_BONUS * jnp.where(forced, 1.0, 0.0), NEG_INF)
            beaten = jnp.zeros((n_seen, TQ), F32)
            for j in range(n_seen):
                other = score[j:j + 1, :]
                wins = (other > score) | ((other == score) & (j_idx > j))
                beaten = beaten + jnp.where(wins, 1.0, 0.0)
            sel_ref[0:n_seen, :] = jnp.where(beaten < NSA_TOP_N, 1.0, 0.0)

        def slc_tile(h, j, qi=qi, q0=q0, select=select):
            s = _nt(ks_ref[h % 2, j * TK:(j + 1) * TK, :], qp_ref[h, q0:q0 + TQ, :])
            mask = r <= c if j == qi else None
            if select:
                rows = [jnp.broadcast_to(sel_ref[per_tile * j + u:per_tile * j + u + 1, :], (NSA_SLC_BLOCK, TQ))
                        for u in range(per_tile)]
                chosen = jnp.concatenate(rows, axis=0) > 0.5
                mask = chosen if mask is None else mask & chosen
            return s if mask is None else jnp.where(mask, s, NEG_INF)

        lo = max(qi - win_tiles, 0)
        n_win = qi - lo + 1

        def win_tile(h, j, qi=qi, q0=q0, lo=lo):
            t = lo + j
            s = _nt(kw_ref[h % 2, t * TK:(t + 1) * TK, :], qp_ref[h, q0:q0 + TQ, :])
            if t == qi:
                return jnp.where(r <= c, s, NEG_INF)
            return jnp.where(c < r, s, NEG_INF) if t == qi - win_tiles else s

        win_first = _attend_heads(qi + 1, slc_tile, lambda h, n: vt_ref[0, :, 0:n * TK], oslc_ref, (sa_ref, sb_ref),
                                  None, lambda: [win_tile(0, j) for j in range(n_win)])
        _attend_heads(n_win, win_tile, lambda h, n, lo=lo: vt_ref[1, :, lo * TK:(lo + n) * TK], owin_ref,
                      (sa_ref, sb_ref), win_first)

        gates = gate_ref[:, q0:q0 + TQ]
        outs = [gates[3 * h:3 * h + 1, :] * o_cmp[h] + gates[3 * h + 1:3 * h + 2, :] * oslc_ref[h]
                + gates[3 * h + 2:3 * h + 3, :] * owin_ref[h] for h in range(N_HEADS)]
        _store_heads(o_ref, q0, jnp.concatenate(outs, axis=0))


def _nsa(x, w, b, wt, bt, pos, w1, w2, layer):
    B, S, D = x.shape
    n_rm = w.shape[-1]
    n_cm = wt.shape[1]
    lay3 = lambda i: (layer, 0, 0)
    lay4 = lambda i: (layer, 0, 0, 0)
    lay5 = lambda i: (layer, 0, 0, 0, 0)
    return pl.pallas_call(
        _nsa_kernel,
        grid=(B,),
        in_specs=[
            pl.BlockSpec((1, S, D), lambda i: (i, 0, 0)),
            pl.BlockSpec((None, D, n_rm), lay3),
            pl.BlockSpec((None, 1, n_rm), lay3),
            pl.BlockSpec((None, n_cm, D), lay3),
            pl.BlockSpec((None, n_cm, 1), lay3),
            pl.BlockSpec((None, 2, NSA_CMP_LEN, HEAD_DIM), lay4),
            pl.BlockSpec((None, 2, NSA_CMP_LEN, HEAD_DIM, NSA_CMP_HIDDEN), lay5),
            pl.BlockSpec((None, 2, NSA_CMP_HIDDEN, HEAD_DIM), lay4),
        ],
        out_specs=pl.BlockSpec((1, S, BRANCH_WIDTH), lambda i: (i, 0, 0)),
        out_shape=jax.ShapeDtypeStruct((B, S, BRANCH_WIDTH), F32),
        scratch_shapes=[
            pltpu.VMEM((S, n_rm), F32),
            pltpu.VMEM((n_cm, S), F32),
            pltpu.VMEM((S, 2 * HEAD_DIM), F32),
            pltpu.VMEM((N_HEADS, S, QK_WIDTH), BF16),
            pltpu.VMEM((2, S, QK_WIDTH), BF16),
            pltpu.VMEM((2, S, QK_WIDTH), BF16),
            pltpu.VMEM((2, N_CMP_PAD, QK_WIDTH), BF16),
            pltpu.VMEM((2, VT_ROWS, S), BF16),
            pltpu.VMEM((HEAD_DIM, N_CMP_PAD), BF16),
            pltpu.VMEM((S // NSA_SLC_BLOCK, TQ), F32),
            pltpu.VMEM((NSA_GATE_ROWS, S), F32),
            pltpu.VMEM((N_HEADS, HEAD_DIM, TQ), F32),
        ] + _attention_scratch(S),
        compiler_params=_params(1),
        name="nsa_mixer",
    )(x, w, b, wt, bt, pos, w1, w2)


def _merge_kernel(x_ref, oa_ref, ob_ref, oc_ref, od_ref, wg_ref, bg_ref, wb_ref, wo_ref, g_ref, b_ref, o_ref):
    x = x_ref[...]
    xb = x.astype(BF16)
    D = x.shape[-1]
    acc = jnp.zeros(x.shape, F32)
    for n, branch in enumerate((oa_ref, ob_ref, oc_ref, od_ref)):
        gate = jax.nn.sigmoid(_nn(xb, wg_ref[:, n * D:(n + 1) * D]) + bg_ref[:, n * D:(n + 1) * D])
        acc = acc + gate * _nn(branch[...].astype(BF16), wb_ref[n])
    half = x.shape[0] // 2
    mixed = acc.astype(BF16)
    mixes = [_nn(mixed[i * half:(i + 1) * half], wo_ref[...]) for i in range(2)]
    o_ref[...] = jnp.concatenate([_layer_norm(ALPHA * x[i * half:(i + 1) * half] + mixes[i], g_ref[...], b_ref[...])
                                  for i in range(2)], axis=0)


def _merge_sublayer(x2, branches, wg, bg, wb, wo, g, b, layer):
    T, D = x2.shape
    tm = TOKEN_TILE
    lay3 = lambda i: (layer, 0, 0)
    return pl.pallas_call(
        _merge_kernel,
        grid=(T // tm,),
        in_specs=[pl.BlockSpec((tm, D), lambda i: (i, 0))]
        + [pl.BlockSpec((tm, BRANCH_WIDTH), lambda i: (i, 0))] * N_BRANCH
        + [
            pl.BlockSpec((None, D, N_BRANCH * D), lay3),
            pl.BlockSpec((None, 1, N_BRANCH * D), lay3),
            pl.BlockSpec((None, N_BRANCH, BRANCH_WIDTH, D), lambda i: (layer, 0, 0, 0)),
            pl.BlockSpec((None, D, D), lay3),
            pl.BlockSpec((1, D), lambda i: (0, 0)),
            pl.BlockSpec((1, D), lambda i: (0, 0)),
        ],
        out_specs=pl.BlockSpec((tm, D), lambda i: (i, 0)),
        out_shape=jax.ShapeDtypeStruct((T, D), F32),
        compiler_params=_params(1),
        name="merge_sublayer",
    )(x2, *branches, wg, bg, wb, wo, g, b)


def _mem_kv_kernel(mem_ref, wk_ref, wvt_ref, kp_ref, vt_ref):
    mem = mem_ref[0].astype(BF16)
    k = _nn(mem, wk_ref[...])
    no_bias = jnp.zeros((k.shape[0], HEAD_DIM), F32)
    for h in range(N_HEADS):
        pair, parity = divmod(h, 2)
        kp_ref[0, h] = _stage_operand(k[:, pair * PAIR_WIDTH:(pair + 1) * PAIR_WIDTH], parity, no_bias)
    _stage_vt(vt_ref.at[0], _nt(wvt_ref[...], mem))


def _mem_kv(mem, w_k, w_vt, layer):
    B, M, D = mem.shape
    lay3 = lambda i: (layer, 0, 0)
    return pl.pallas_call(
        _mem_kv_kernel,
        grid=(B,),
        in_specs=[pl.BlockSpec((1, M, D), lambda i: (i, 0, 0)),
                  pl.BlockSpec((None, D, BRANCH_WIDTH), lay3),
                  pl.BlockSpec((None, BRANCH_WIDTH, D), lay3)],
        out_specs=[pl.BlockSpec((1, N_HEADS, M, QK_WIDTH), lambda i: (i, 0, 0, 0)),
                   pl.BlockSpec((1, N_HEADS, VT_ROWS, M), lambda i: (i, 0, 0, 0))],
        out_shape=[jax.ShapeDtypeStruct((B, N_HEADS, M, QK_WIDTH), BF16),
                   jax.ShapeDtypeStruct((B, N_HEADS, VT_ROWS, M), BF16)],
        compiler_params=_params(1),
        name="mem_kv",
    )(mem, w_k, w_vt)


def _xattn_kernel(x_ref, kp_ref, vt_ref, wq_ref, wo_ref, g_ref, b_ref, o_ref):
    rows = x_ref.shape[1] // XATTN_SPLIT
    xs = [x_ref[0, i * rows:(i + 1) * rows, :] for i in range(XATTN_SPLIT)]
    qs = [_nn(x.astype(BF16), wq_ref[...]) * (HEAD_DIM ** -0.5 * LOG2E) for x in xs]
    no_bias = jnp.zeros((rows, HEAD_DIM), F32)

    def logits_t(q, h):
        pair, parity = divmod(h, 2)
        return _nt(kp_ref[0, h], _stage_operand(q[:, pair * PAIR_WIDTH:(pair + 1) * PAIR_WIDTH], parity, no_bias))

    ss = [logits_t(q, 0) for q in qs]
    heads = [[] for _ in qs]
    for h in range(N_HEADS):
        ss_next = [logits_t(q, h + 1) for q in qs] if h + 1 < N_HEADS else None
        for i, s in enumerate(ss):
            p = jnp.exp2(s - jnp.max(s, axis=0, keepdims=True)).astype(BF16)
            res = _nn(vt_ref[0, h], p)
            heads[i].append(res[:HEAD_DIM] / res[HEAD_DIM:HEAD_DIM + 1])
        ss = ss_next
    atts = [jnp.concatenate(hs, axis=0).T.astype(BF16) for hs in heads]
    outs = [_layer_norm(ALPHA * x + _nn(att, wo_ref[...]), g_ref[...], b_ref[...]) for x, att in zip(xs, atts)]
    o_ref[0] = jnp.concatenate(outs, axis=0)


def _xattn_sublayer(x, kp, vt, wq, wo, g, b, layer):
    B, S, D = x.shape
    M = kp.shape[2]
    tm = XATTN_TILE
    lay3 = lambda i, j: (layer, 0, 0)
    return pl.pallas_call(
        _xattn_kernel,
        grid=(B, S // tm),
        in_specs=[
            pl.BlockSpec((1, tm, D), lambda i, j: (i, j, 0)),
            pl.BlockSpec((1, N_HEADS, M, QK_WIDTH), lambda i, j: (i, 0, 0, 0)),
            pl.BlockSpec((1, N_HEADS, VT_ROWS, M), lambda i, j: (i, 0, 0, 0)),
            pl.BlockSpec((None, D, BRANCH_WIDTH), lay3),
            pl.BlockSpec((None, BRANCH_WIDTH, D), lay3),
            pl.BlockSpec((1, D), lambda i, j: (0, 0)),
            pl.BlockSpec((1, D), lambda i, j: (0, 0)),
        ],
        out_specs=pl.BlockSpec((1, tm, D), lambda i, j: (i, j, 0)),
        out_shape=jax.ShapeDtypeStruct((B, S, D), F32),
        compiler_params=_params(2),
        name="xattn_sublayer",
    )(x, kp, vt, wq, wo, g, b)


def _pad_cols(w, n):
    return jnp.pad(w, [(0, 0)] * (w.ndim - 1) + [(0, n - w.shape[-1])])


def _split_w_in(w_in, b_in):
    hd, bw = HEAD_DIM, BRANCH_WIDTH
    sizes = (bw, bw, bw, bw, hd, hd, hd, hd, hd, hd, 3 * N_HEADS, bw, bw, bw, N_HEADS,
             MLA_Q_RANK, MLA_KV_RANK, MLA_ROPE, N_BRANCH * D_MODEL)
    names = ("a_q", "a_k", "a_v", "b_q", "b_kc", "b_vc", "b_ks", "b_vs", "b_kw", "b_vw", "b_g",
             "c_q", "c_k", "c_v", "c_f", "d_cq", "d_ckv", "d_kr", "g_merge")
    w, b, off = {}, {}, 0
    for name, size in zip(names, sizes):
        w[name] = w_in[:, :, off:off + size]
        b[name] = b_in[:, off:off + size]
        off += size
    return w, b


def _row_major(ws, bs, names, pad_to):
    w = _pad_cols(jnp.concatenate([ws[n] for n in names], axis=-1), pad_to).astype(BF16)
    b = _pad_cols(jnp.concatenate([bs[n] for n in names], axis=-1), pad_to)[:, None, :]
    return w, b


def _channel_major(ws, bs, names, pad_to):
    w = _pad_cols(jnp.concatenate([ws[n] for n in names], axis=-1), pad_to).transpose(0, 2, 1).astype(BF16)
    b = _pad_cols(jnp.concatenate([bs[n] for n in names], axis=-1), pad_to)[:, :, None]
    return w, b


def _rope_slab(a, swap):
    half = MLA_ROPE // 2
    first, second = a[..., :half], a[..., half:]
    body = jnp.concatenate([second, first] if swap else [first, second], axis=-1)
    return jnp.pad(body, [(0, 0)] * (a.ndim - 1) + [(MLA_NOPE, LANES - MLA_NOPE - MLA_ROPE)])


def _mla_weights(ws, bs, mla_w_uq, mla_w_ukv):
    L = mla_w_uq.shape[0]
    half = MLA_ROPE // 2
    def key_slab(a):
        return jnp.concatenate([_rope_slab(a, False)[..., :MLA_NOPE + MLA_ROPE], _rope_slab(a, True)[..., MLA_NOPE:MLA_NOPE + MLA_ROPE]],
                               axis=-1)

    w = jnp.concatenate([ws["d_cq"], ws["d_ckv"], key_slab(ws["d_kr"])], axis=-1).astype(BF16)
    b = jnp.concatenate([bs["d_cq"], bs["d_ckv"], key_slab(bs["d_kr"])], axis=-1)[:, None, :]
    uq = mla_w_uq.reshape(L, MLA_Q_RANK, N_HEADS, MLA_NOPE + MLA_ROPE).transpose(0, 2, 1, 3)
    wq = jnp.concatenate([uq[..., :MLA_NOPE], _rope_slab(uq[..., MLA_NOPE:], False)[..., MLA_NOPE:]],
                         axis=-1).astype(BF16)
    wqs = jnp.concatenate([jnp.zeros_like(uq[..., :MLA_NOPE]), _rope_slab(uq[..., MLA_NOPE:], True)[..., MLA_NOPE:]],
                          axis=-1).astype(BF16)
    ukv = mla_w_ukv.reshape(L, MLA_KV_RANK, N_HEADS, MLA_NOPE + HEAD_DIM).transpose(0, 2, 1, 3)
    wk = _pad_cols(ukv[..., :MLA_NOPE], QK_WIDTH).astype(BF16)
    wvt = ukv[..., MLA_NOPE:].transpose(0, 1, 3, 2).reshape(L, BRANCH_WIDTH, MLA_KV_RANK).astype(BF16)
    inv_freq = ROPE_BASE ** (-jnp.arange(0, MLA_ROPE, 2, dtype=F32) / MLA_ROPE)
    freq_row = _rope_slab(jnp.concatenate([inv_freq, inv_freq])[None, :], False)
    sign_row = _rope_slab(jnp.concatenate([-jnp.ones((half,), F32), jnp.ones((half,), F32)])[None, :], False)
    def by_pair(a):
        return jnp.concatenate([a[:, 0::2], a[:, 1::2]], axis=-1)

    return w, b, by_pair(wq), by_pair(wqs), by_pair(wk), wvt, freq_row, sign_row


def kernel(x, mem, positions, ln_g, ln_b, ffn1_w_in, ffn1_w_out, ffn2_w_in, ffn2_w_out, w_in, b_in, w_branch, w_o,
           mla_q_norm, mla_w_uq, mla_kv_norm, mla_w_ukv, nsa_cmp_pos, nsa_cmp_w1, nsa_cmp_w2, xa_w_q, xa_w_kv, xa_w_o):
    B, S, D = x.shape
    L = ln_g.shape[0]
    ffn1 = (ffn1_w_in.astype(BF16), ffn1_w_out.astype(BF16))
    ffn2 = (ffn2_w_in.astype(BF16), ffn2_w_out.astype(BF16))

    ws, bs = _split_w_in(w_in.astype(BF16), b_in)
    moba_w, moba_b = _row_major(ws, bs, ("a_q", "a_k"), 2 * BRANCH_WIDTH)
    moba_wt, moba_bt = _channel_major(ws, bs, ("a_v",), BRANCH_WIDTH)
    nsa_w, nsa_b = _row_major(ws, bs, ("b_q", "b_ks", "b_kw", "b_kc", "b_vc"), 2 * BRANCH_WIDTH)
    nsa_wt, nsa_bt = _channel_major(ws, bs, ("b_vs", "b_vw", "b_g"), 2 * HEAD_DIM + NSA_GATE_ROWS)
    fox_w, fox_b = _row_major(ws, bs, ("c_q", "c_k"), 2 * BRANCH_WIDTH)
    fox_wt, fox_bt = _channel_major(ws, bs, ("c_v", "c_f"), BRANCH_WIDTH + 2 * FOX_F_ROWS)

    mla_w, mla_b, mla_wq, mla_wqs, mla_wk, mla_wvt, freq_row, sign_row = _mla_weights(ws, bs, mla_w_uq, mla_w_ukv)
    cos, sin = _rope_tables(positions, freq_row, sign_row)

    merge_wg = ws["g_merge"].astype(BF16)
    merge_bg = bs["g_merge"][:, None, :]
    merge_wb = w_branch.astype(BF16)
    merge_wo = w_o.astype(BF16)
    nsa_w1 = nsa_cmp_w1.reshape(L, 2, NSA_CMP_LEN, HEAD_DIM, NSA_CMP_HIDDEN).astype(BF16)
    nsa_w2 = nsa_cmp_w2.astype(BF16)
    xa_wq, xa_wo = xa_w_q.astype(BF16), xa_w_o.astype(BF16)
    xa_wk = xa_w_kv[:, :, :BRANCH_WIDTH].astype(BF16)
    xa_wvt = xa_w_kv[:, :, BRANCH_WIDTH:].transpose(0, 2, 1).astype(BF16)

    for l in range(L):
        ln = lambda i: (ln_g[l, i][None, :], ln_b[l, i][None, :])
        x2, x16 = _ffn_sublayer(x.reshape(B * S, D), *ffn1, *ln(0), l, True)
        x16 = x16.reshape(B, S, D)
        branches = (
            _moba(x16, moba_w, moba_b, moba_wt, moba_bt, l),
            _nsa(x16, nsa_w, nsa_b, nsa_wt, nsa_bt, nsa_cmp_pos, nsa_w1, nsa_w2, l),
            _fox(x16, fox_w, fox_b, fox_wt, fox_bt, l),
            _mla(x16, cos, sin, mla_w, mla_b, mla_q_norm[:, None, :], mla_kv_norm[:, None, :],
                 mla_wq, mla_wqs, mla_wk, mla_wvt, l),
        )
        x2 = _merge_sublayer(x2, [o.reshape(B * S, BRANCH_WIDTH) for o in branches],
                             merge_wg, merge_bg, merge_wb, merge_wo, *ln(1), l)
        x = _xattn_sublayer(x2.reshape(B, S, D), *_mem_kv(mem, xa_wk, xa_wvt, l), xa_wq, xa_wo, *ln(2), l)
        x = _ffn_sublayer(x.reshape(B * S, D), *ffn2, *ln(3), l, False)[0].reshape(B, S, D)
    return x
```
